```python
import jax, jax.numpy as jnp
from jax import lax
import numpy as np

D_MODEL = 1024
BATCH = 2
SEQ = 8192
DEPTH = 4
DEC_BATCH = 128
DEC_SEQ = 1
PAST_LEN = 8192
PAGE_SIZE = 128

N_A = DEPTH // 2
N_B = DEPTH - N_A
N_HEADS = 16
HEAD_DIM = 64
N_KV = 2
GROUP = N_HEADS // N_KV
D_MAIN = N_HEADS * HEAD_DIM
D_RNN = D_MAIN
N_BLOCKS = 16
BLOCK_W = D_RNN // N_BLOCKS
CONV_W = 4
LRU_C = 8.0
WINDOW = 128
ROPE_THETA = 10000.0
N_MEM = 256
MEM_HEADS = 4
MEM_HD = 128
D_MEMW = MEM_HEADS * MEM_HD
D_IN = 2 * D_MAIN + 2 * D_MEMW
D_CAT = D_MAIN + D_MEMW
EPS = 1e-6
NEG = -1e30

kernel_name = "hawk_yoco_swa_sink_memxattn_step"


def rmsnorm(x, g):
    xf = x.astype(jnp.float32)
    y = xf * lax.rsqrt(jnp.mean(xf * xf, axis=-1, keepdims=True) + EPS)
    return (y * g.astype(jnp.float32)).astype(x.dtype)


def rope(x, pos):
    half = HEAD_DIM // 2
    inv = ROPE_THETA ** (-jnp.arange(half, dtype=jnp.float32) / half)
    ang = pos.astype(jnp.float32)[:, None] * inv[None, :]
    cos = jnp.cos(ang)[:, None, :]
    sin = jnp.sin(ang)[:, None, :]
    xf = x.astype(jnp.float32)
    x1, x2 = xf[..., :half], xf[..., half:]
    return jnp.concatenate([x1 * cos - x2 * sin, x2 * cos + x1 * sin], axis=-1).astype(x.dtype)


def causal_conv(u, buf, w, b):
    T = u.shape[1]
    up = jnp.concatenate([buf, u], axis=1)
    y = b + up[:, 0:T] * w[0]
    for k in range(1, CONV_W):
        y = y + up[:, k:k + T] * w[k]
    return y, up[:, T:]


def block_diag(u, w):
    ub = u.reshape(u.shape[:-1] + (N_BLOCKS, BLOCK_W))
    return jnp.einsum('btnc,ncd->btnd', ub, w).reshape(u.shape)


def rglru(u, h0, wa, ba, wx, bx, lam):
    f32 = jnp.float32
    uf = u.astype(f32)
    r = jax.nn.sigmoid(block_diag(uf, wa.astype(f32)) + ba.astype(f32))
    i = jax.nn.sigmoid(block_diag(uf, wx.astype(f32)) + bx.astype(f32))
    log_a = -LRU_C * r * jax.nn.softplus(-lam.astype(f32))
    a = jnp.exp(log_a)
    bterm = jnp.sqrt(-jnp.expm1(2.0 * log_a)) * (i * uf)

    def combine(c1, c2):
        a1, b1 = c1
        a2, b2 = c2
        return a1 * a2, a2 * b1 + b2

    acum, bcum = lax.associative_scan(combine, (a, bterm), axis=1)
    h = acum * h0.astype(f32)[:, None, :] + bcum
    return h.astype(u.dtype), h[:, -1].astype(u.dtype)


def mem_attend(q, mk, mv):
    s = jnp.einsum('bthd,bmhd->bhtm', q, mk, preferred_element_type=jnp.float32) * (MEM_HD ** -0.5)
    p = jax.nn.softmax(s, axis=-1).astype(mv.dtype)
    return jnp.einsum('bhtm,bmhd->bthd', p, mv)


def sink_attend(q, k, v, mask, sink):
    s = jnp.einsum('...qkgd,...skd->...kgqs', q, k, preferred_element_type=jnp.float32) * (HEAD_DIM ** -0.5)
    s = jnp.where(mask, s, NEG)
    sk = sink.astype(jnp.float32).reshape(N_KV, GROUP, 1, 1)
    m = jnp.maximum(jnp.max(s, axis=-1, keepdims=True), sk)
    p = jnp.exp(s - m)
    p = p / (jnp.sum(p, axis=-1, keepdims=True) + jnp.exp(sk - m))
    return jnp.einsum('...kgqs,...skd->...qkgd', p.astype(v.dtype), v)


def band_mask(nb):
    qi = jnp.arange(WINDOW)[:, None] + WINDOW
    ki = jnp.arange(2 * WINDOW)[None, :]
    local = (ki <= qi) & (qi - ki < WINDOW)
    valid = (jnp.arange(nb)[:, None, None] * WINDOW - WINDOW + ki[None]) >= 0
    return local[None] & valid


def shift_blocks(xb):
    return jnp.concatenate([jnp.zeros_like(xb[:, :1]), xb[:, :-1]], axis=1)


def trunk(x, pos, mem_k, mem_v, h0, conv0, win_k0, win_v0, prompt,
          norm_g, w_in, w_out, conv_w, conv_b, lru_wa, lru_ba, lru_wx, lru_bx, lru_lambda,
          kv_norm_g, w_kv, sinks, final_g):
    B, T, _ = x.shape
    new_h, new_conv = [], []
    for l in range(DEPTH):
        hn = rmsnorm(x, norm_g[l])
        proj = hn @ w_in[l]
        u, g_main, q_mem, g_mem = jnp.split(proj, [D_MAIN, 2 * D_MAIN, 2 * D_MAIN + D_MEMW], axis=-1)
        if l < N_A:
            uc, cbuf = causal_conv(u, conv0[l], conv_w[l], conv_b[l])
            y, h_last = rglru(uc, h0[l], lru_wa[l], lru_ba[l], lru_wx[l], lru_bx[l], lru_lambda[l])
            new_h.append(h_last)
            new_conv.append(cbuf)
        else:
            if l == N_A:
                kv = rmsnorm(x, kv_norm_g) @ w_kv
                k_new, v_new = jnp.split(kv, 2, axis=-1)
                k_new = rope(k_new.reshape(B, T, N_KV, HEAD_DIM), pos)
                v_new = v_new.reshape(B, T, N_KV, HEAD_DIM)
                if prompt:
                    nb = T // WINDOW
                    kb = k_new.reshape(B, nb, WINDOW, N_KV, HEAD_DIM)
                    vb = v_new.reshape(B, nb, WINDOW, N_KV, HEAD_DIM)
                    k_att = jnp.concatenate([shift_blocks(kb), kb], axis=2)
                    v_att = jnp.concatenate([shift_blocks(vb), vb], axis=2)
                    mask = band_mask(nb)[:, None, None]
                    win_k, win_v = k_new[:, T - WINDOW:], v_new[:, T - WINDOW:]
                else:
                    k_att = jnp.concatenate([win_k0, k_new], axis=1)
                    v_att = jnp.concatenate([win_v0, v_new], axis=1)
                    kpos = PAST_LEN - WINDOW + jnp.arange(WINDOW + T)
                    mask = (kpos[None, :] <= pos[:, None]) & (pos[:, None] - kpos[None, :] < WINDOW)
                    win_k, win_v = k_att[:, T:], v_att[:, T:]
            q = rope(u.reshape(B, T, N_HEADS, HEAD_DIM), pos)
            if prompt:
                q = q.reshape(B, T // WINDOW, WINDOW, N_KV, GROUP, HEAD_DIM)
            else:
                q = q.reshape(B, T, N_KV, GROUP, HEAD_DIM)
            y = sink_attend(q, k_att, v_att, mask, sinks[l - N_A]).reshape(B, T, D_MAIN)
        om = mem_attend(q_mem.reshape(B, T, MEM_HEADS, MEM_HD), mem_k[l], mem_v[l]).reshape(B, T, D_MEMW)
        z = jnp.concatenate([y * jax.nn.silu(g_main), om * jax.nn.silu(g_mem)], axis=-1)
        x = x + z @ w_out[l]
    return rmsnorm(x, final_g), jnp.stack(new_h), jnp.stack(new_conv), win_k, win_v


def setup_inputs(seed: int = 0) -> dict:
    key = jax.random.key(seed)
    ks = jax.random.split(key, 25)
    nrm = jax.random.normal
    a0 = jax.random.uniform(ks[20], (N_A, D_RNN), minval=0.9, maxval=0.999)
    return {
        'x_prompt': nrm(ks[0], (BATCH, SEQ, D_MODEL), jnp.float32),
        'x_sample': nrm(ks[1], (DEC_BATCH, DEC_SEQ, D_MODEL), jnp.float32),
        'cache_mem_k': nrm(ks[2], (DEPTH, DEC_BATCH, N_MEM, MEM_HEADS, MEM_HD), jnp.float32),
        'cache_mem_v': nrm(ks[3], (DEPTH, DEC_BATCH, N_MEM, MEM_HEADS, MEM_HD), jnp.float32),
        'state_lru_h': 0.5 * nrm(ks[4], (N_A, DEC_BATCH, D_RNN), jnp.float32),
        'state_conv': nrm(ks[5], (N_A, DEC_BATCH, CONV_W - 1, D_RNN), jnp.float32),
        'cache_win_k': nrm(ks[6], (DEC_BATCH, WINDOW, N_KV, HEAD_DIM), jnp.float32),
        'cache_win_v': nrm(ks[7], (DEC_BATCH, WINDOW, N_KV, HEAD_DIM), jnp.float32),
        'mem_prompt': nrm(ks[8], (BATCH, N_MEM, D_MODEL), jnp.float32),
        'norm_g': 1.0 + 0.05 * nrm(ks[9], (DEPTH, D_MODEL), jnp.float32),
        'w_in': nrm(ks[10], (DEPTH, D_MODEL, D_IN), jnp.float32) * D_MODEL ** -0.5,
        'w_out': nrm(ks[11], (DEPTH, D_CAT, D_MODEL), jnp.float32) * D_CAT ** -0.5,
        'mem_norm_g': 1.0 + 0.05 * nrm(ks[12], (DEPTH, D_MODEL), jnp.float32),
        'w_mem_kv': nrm(ks[13], (DEPTH, D_MODEL, 2 * D_MEMW), jnp.float32) * D_MODEL ** -0.5,
        'conv_w': nrm(ks[14], (N_A, CONV_W, D_RNN), jnp.float32) * CONV_W ** -0.5,
        'conv_b': 0.02 * nrm(ks[15], (N_A, D_RNN), jnp.float32),
        'lru_wa': nrm(ks[16], (N_A, N_BLOCKS, BLOCK_W, BLOCK_W), jnp.float32) * BLOCK_W ** -0.5,
        'lru_ba': 0.1 * nrm(ks[17], (N_A, D_RNN), jnp.float32),
        'lru_wx': nrm(ks[18], (N_A, N_BLOCKS, BLOCK_W, BLOCK_W), jnp.float32) * BLOCK_W ** -0.5,
        'lru_bx': 0.1 * nrm(ks[19], (N_A, D_RNN), jnp.float32),
        'lru_lambda': jnp.log(a0) - jnp.log1p(-a0),
        'kv_norm_g': 1.0 + 0.05 * nrm(ks[21], (D_MODEL,), jnp.float32),
        'w_kv': nrm(ks[22], (D_MODEL, 2 * N_KV * HEAD_DIM), jnp.float32) * D_MODEL ** -0.5,
        'sinks': 0.5 * nrm(ks[23], (N_B, N_HEADS), jnp.float32),
        'final_g': 1.0 + 0.05 * nrm(ks[24], (D_MODEL,), jnp.float32),
    }


def reference(x_prompt, x_sample, cache_mem_k, cache_mem_v, state_lru_h, state_conv,
              cache_win_k, cache_win_v, mem_prompt, norm_g, w_in, w_out, mem_norm_g, w_mem_kv,
              conv_w, conv_b, lru_wa, lru_ba, lru_wx, lru_bx, lru_lambda, kv_norm_g, w_kv,
              sinks, final_g):
    weights = (norm_g, w_in, w_out, conv_w, conv_b, lru_wa, lru_ba, lru_wx, lru_bx, lru_lambda,
               kv_norm_g, w_kv, sinks, final_g)
    mem_n = rmsnorm(mem_prompt[None], mem_norm_g[:, None, None, :])
    mkv = jnp.einsum('lbmd,lde->lbme', mem_n, w_mem_kv)
    mk_p, mv_p = jnp.split(mkv, 2, axis=-1)
    mk_p = mk_p.reshape(mk_p.shape[:3] + (MEM_HEADS, MEM_HD))
    mv_p = mv_p.reshape(mv_p.shape[:3] + (MEM_HEADS, MEM_HD))
    bp, tp = x_prompt.shape[0], x_prompt.shape[1]
    h0_p = jnp.zeros((N_A, bp, D_RNN), x_prompt.dtype)
    c0_p = jnp.zeros((N_A, bp, CONV_W - 1, D_RNN), x_prompt.dtype)
    pos_p = jnp.arange(tp, dtype=jnp.int32)
    y_p, h_p, conv_p, wk_p, wv_p = trunk(x_prompt, pos_p, mk_p, mv_p, h0_p, c0_p, None, None, True, *weights)
    pos_s = PAST_LEN + jnp.arange(x_sample.shape[1], dtype=jnp.int32)
    y_s, h_s, conv_s, wk_s, wv_s = trunk(x_sample, pos_s, cache_mem_k, cache_mem_v, state_lru_h, state_conv,
                                         cache_win_k, cache_win_v, False, *weights)
    return (y_p, y_s, mk_p, mv_p, h_p, conv_p, wk_p, wv_p, h_s, conv_s, wk_s, wv_s)
```

```python
import functools

import jax
import jax.numpy as jnp
from jax import lax
from jax.experimental import pallas as pl
from jax.experimental.pallas import tpu as pltpu

F32 = jnp.float32
BF16 = jnp.bfloat16

D_MODEL = 1024
DEPTH = 4
N_A = DEPTH // 2
N_HEADS = 16
HEAD_DIM = 64
N_KV = 2
GROUP = N_HEADS // N_KV
D_MAIN = N_HEADS * HEAD_DIM
N_BLOCKS = 16
BLOCK_W = D_MAIN // N_BLOCKS
CONV_W = 4
LRU_C = 8.0
WINDOW = 128
ROPE_THETA = 10000.0
N_MEM = 256
MEM_HEADS = 4
MEM_HD = 128
D_MEMW = MEM_HEADS * MEM_HD
D_IN = 2 * D_MAIN + 2 * D_MEMW
D_CAT = D_MAIN + D_MEMW
D_KV = N_KV * HEAD_DIM
EPS = 1e-6
NEG = -1e30
PAST_LEN = 8192

SUBLANES = 8
LANES = 128
N_SLABS = D_MAIN // LANES
GATE_TILE = 256
N_GATE_TILES = D_MAIN // GATE_TILE
VMEM_LIMIT_BYTES = 56 * 1024 * 1024

_NT = (((1,), (1,)), ((), ()))


def _const_spec(shape):
    zeros = (0,) * len(shape)
    return pl.BlockSpec(shape, lambda *_: zeros, pipeline_mode=pl.Buffered(1))


def _rms(x, g):
    return x * lax.rsqrt(jnp.mean(x * x, axis=-1, keepdims=True) + EPS) * g


def _sigmoid(x):
    return 0.5 * jnp.tanh(0.5 * x) + 0.5


def _silu(x):
    return x * _sigmoid(x)


def _softplus(x):
    return jnp.maximum(x, 0.0) + jnp.log1p(jnp.exp(-jnp.abs(x)))


def _rope(x, cos, sin_signed):
    lane = lax.broadcasted_iota(jnp.int32, x.shape, 1)
    first_half = (lane % HEAD_DIM) < (HEAD_DIM // 2)
    swapped = jnp.where(first_half,
                        pltpu.roll(x, LANES - HEAD_DIM // 2, 1),
                        pltpu.roll(x, HEAD_DIM // 2, 1))
    return x * cos + swapped * sin_signed


def _rglru_coeffs(uc, pre, ba, bx, sp):
    half = pre.shape[-1] // 2
    r = _sigmoid(pre[:, :half] + ba)
    i = _sigmoid(pre[:, half:] + bx)
    log_a = (-LRU_C * r) * sp
    a = jnp.exp(log_a)
    mult = jnp.sqrt(-jnp.tanh(log_a) * (1.0 + a * a))
    return a, mult * (i * uc)


def _mem_attend(q, mk, mv):
    s = lax.dot_general(q.astype(BF16), mk, _NT, preferred_element_type=F32) * (MEM_HD ** -0.5)
    m = jnp.max(s, axis=-1, keepdims=True)
    p = jnp.exp(s - m)
    den = jnp.sum(p, axis=-1, keepdims=True)
    o = jnp.dot(p.astype(BF16), mv, preferred_element_type=F32)
    return o / den


def _block_diag_pair(x, x_rolled, kv_head):
    lane = lax.broadcasted_iota(jnp.int32, x.shape, 1)
    lo = lane < HEAD_DIM
    if kv_head == 0:
        top = jnp.where(lo, x, 0.0)
        bot = jnp.where(lo, 0.0, x_rolled)
    else:
        top = jnp.where(lo, x_rolled, 0.0)
        bot = jnp.where(lo, 0.0, x)
    return jnp.concatenate([top, bot], axis=0).astype(BF16)


def _gate_tiles(wa, wx):
    per_tile = GATE_TILE // BLOCK_W
    eye = jnp.eye(per_tile, dtype=wa.dtype)

    def dense(w):
        w4 = w.reshape(N_GATE_TILES, per_tile, BLOCK_W, BLOCK_W)
        return jnp.einsum("jicd,ik->jickd", w4, eye).reshape(N_GATE_TILES, GATE_TILE, GATE_TILE)

    return jnp.concatenate([dense(wa), dense(wx)], axis=-1).astype(BF16)


def _prepare_weights(norm_g, w_in, w_out, conv_w, conv_b, lru_wa, lru_ba, lru_wx, lru_bx, lru_lambda,
                     kv_norm_g, w_kv, sinks, final_g):
    w_in_b = w_in.astype(BF16)
    w_out_b = w_out.astype(BF16)
    layers = []
    for l in range(DEPTH):
        w = dict(norm_g=norm_g[l].reshape(1, D_MODEL), w_in=w_in_b[l], w_out=w_out_b[l])
        if l < N_A:
            w.update(conv_w=conv_w[l], conv_b=conv_b[l].reshape(1, D_MAIN),
                     w_gate=_gate_tiles(lru_wa[l], lru_wx[l]),
                     ba=lru_ba[l].reshape(1, D_MAIN), bx=lru_bx[l].reshape(1, D_MAIN),
                     lam=lru_lambda[l].reshape(1, D_MAIN))
        else:
            w.update(sinks=sinks[l - N_A])
        layers.append(w)
    half = HEAD_DIM // 2
    inv = ROPE_THETA ** (-jnp.arange(half, dtype=F32) / half)
    inv128 = jnp.tile(inv, LANES // half).reshape(1, LANES)
    sign128 = jnp.tile(jnp.concatenate([-jnp.ones((half,), F32), jnp.ones((half,), F32)]),
                       LANES // HEAD_DIM).reshape(1, LANES)
    return dict(layers=layers, kv_norm_g=kv_norm_g.reshape(1, D_MODEL), w_kv=w_kv.astype(BF16),
                final_g=final_g.reshape(1, D_MODEL), inv128=inv128, sign128=sign128)


def _rope_table_kernel(inv_ref, sign_ref, cos_ref, sin_ref):
    rows = cos_ref.shape[0]
    pos = (pl.program_id(0) * rows + lax.broadcasted_iota(jnp.int32, (rows, LANES), 0)).astype(F32)
    ang = pos * inv_ref[...]
    cos_ref[...] = jnp.cos(ang)
    sin_ref[...] = jnp.sin(ang) * sign_ref[...]


def _rope_tables(seq, inv128, sign128, rows=1024):
    rows = min(rows, seq)
    return pl.pallas_call(
        _rope_table_kernel,
        grid=(seq // rows,),
        in_specs=[_const_spec((1, LANES)), _const_spec((1, LANES))],
        out_specs=[pl.BlockSpec((rows, LANES), lambda i: (i, 0)),
                   pl.BlockSpec((rows, LANES), lambda i: (i, 0))],
        out_shape=[jax.ShapeDtypeStruct((seq, LANES), F32)] * 2,
        name="rope_tables",
    )(inv128, sign128)


def _mem_kv_kernel(m_ref, g_ref, w_ref, k_ref, v_ref, kb_ref, vb_ref):
    mn = _rms(m_ref[0], g_ref[0]).astype(BF16)
    kv = jnp.dot(mn, w_ref[0], preferred_element_type=F32)
    k, v = kv[:, :D_MEMW], kv[:, D_MEMW:]
    k_ref[0, 0] = k
    v_ref[0, 0] = v
    kb_ref[0, 0] = k.astype(BF16)
    vb_ref[0, 0] = v.astype(BF16)


def _mem_kv(mem_prompt, mem_norm_g, w_mem_kv_bf16):
    batch = mem_prompt.shape[0]
    out_spec = pl.BlockSpec((1, 1, N_MEM, D_MEMW), lambda l, b: (l, b, 0, 0))
    return pl.pallas_call(
        _mem_kv_kernel,
        grid=(DEPTH, batch),
        in_specs=[pl.BlockSpec((1, N_MEM, D_MODEL), lambda l, b: (b, 0, 0)),
                  pl.BlockSpec((1, 1, D_MODEL), lambda l, b: (l, 0, 0)),
                  pl.BlockSpec((1, D_MODEL, 2 * D_MEMW), lambda l, b: (l, 0, 0))],
        out_specs=[out_spec] * 4,
        out_shape=[jax.ShapeDtypeStruct((DEPTH, batch, N_MEM, D_MEMW), F32)] * 2
                  + [jax.ShapeDtypeStruct((DEPTH, batch, N_MEM, D_MEMW), BF16)] * 2,
        name="mem_kv",
    )(mem_prompt, mem_norm_g.reshape(DEPTH, 1, D_MODEL), w_mem_kv_bf16)


def _prompt_lru_kernel(emit_kv, tile, *refs):
    n_in = 16 if emit_kv else 12
    (x_ref, ng_ref, win_ref, wout_ref, cw_ref, cb_ref, wg_ref, ba_ref, bx_ref, lam_ref,
     mk_ref, mv_ref) = refs[:12]
    if emit_kv:
        kvg_ref, wkv_ref, cos_ref, sin_ref = refs[12:16]
        xo_ref, hl_ref, ct_ref, k_ref, v_ref = refs[n_in:n_in + 5]
        scratch = refs[n_in + 5:]
    else:
        xo_ref, hl_ref, ct_ref = refs[n_in:n_in + 3]
        scratch = refs[n_in + 3:]
    ubuf, a3, b3, h3, hc, z = scratch
    groups = tile // SUBLANES
    t = pl.program_id(1)

    @pl.when(t == 0)
    def _():
        ubuf[0:SUBLANES, :] = jnp.zeros((SUBLANES, D_MAIN), F32)
        hc[...] = jnp.zeros((SUBLANES, LANES), F32)

    x = x_ref[0]
    hn = _rms(x, ng_ref[...]).astype(BF16)
    proj = jnp.dot(hn, win_ref[...], preferred_element_type=F32)

    u = proj[:, :D_MAIN]
    ubuf[SUBLANES:SUBLANES + tile, :] = u
    uc = cb_ref[...] + ubuf[SUBLANES - 3:SUBLANES - 3 + tile, :] * cw_ref[0:1, :]
    uc = uc + ubuf[SUBLANES - 2:SUBLANES - 2 + tile, :] * cw_ref[1:2, :]
    uc = uc + ubuf[SUBLANES - 1:SUBLANES - 1 + tile, :] * cw_ref[2:3, :]
    uc = uc + u * cw_ref[3:4, :]
    ubuf[0:SUBLANES, :] = ubuf[tile:tile + SUBLANES, :]

    sp = _softplus(-lam_ref[...])
    ucb = uc.astype(BF16)
    for j in range(N_GATE_TILES):
        sl = slice(GATE_TILE * j, GATE_TILE * (j + 1))
        pre = jnp.dot(ucb[:, sl], wg_ref[j], preferred_element_type=F32)
        a, b = _rglru_coeffs(uc[:, sl], pre, ba_ref[:, sl], bx_ref[:, sl], sp[:, sl])
        for jj in range(GATE_TILE // LANES):
            slab = j * (GATE_TILE // LANES) + jj
            rows = slice(SUBLANES * slab, SUBLANES * (slab + 1))
            cols = slice(LANES * jj, LANES * (jj + 1))
            a3[:, rows, :] = a[:, cols].reshape(groups, SUBLANES, LANES)
            b3[:, rows, :] = b[:, cols].reshape(groups, SUBLANES, LANES)

    def scan_group(g, h):
        for r in range(SUBLANES):
            step = pl.ds(r, N_SLABS, stride=SUBLANES)
            h = a3[g, step, :] * h + b3[g, step, :]
            h3[g, step, :] = h
        return h

    h = lax.fori_loop(0, groups, scan_group, hc[...])
    hc[...] = h

    for j in range(N_SLABS):
        cols = slice(LANES * j, LANES * (j + 1))
        y = h3[:, SUBLANES * j:SUBLANES * (j + 1), :].reshape(tile, LANES)
        g_main = proj[:, D_MAIN + LANES * j:D_MAIN + LANES * (j + 1)]
        z[:, cols] = (y * _silu(g_main)).astype(BF16)

    for hh in range(MEM_HEADS):
        cols = slice(MEM_HD * hh, MEM_HD * (hh + 1))
        q = proj[:, 2 * D_MAIN + MEM_HD * hh:2 * D_MAIN + MEM_HD * (hh + 1)]
        om = _mem_attend(q, mk_ref[0, :, cols], mv_ref[0, :, cols])
        g_mem = proj[:, 2 * D_MAIN + D_MEMW + MEM_HD * hh:2 * D_MAIN + D_MEMW + MEM_HD * (hh + 1)]
        z[:, D_MAIN + MEM_HD * hh:D_MAIN + MEM_HD * (hh + 1)] = (om * _silu(g_mem)).astype(BF16)

    xn = x + jnp.dot(z[...], wout_ref[...], preferred_element_type=F32)
    xo_ref[0] = xn

    @pl.when(t == pl.num_programs(1) - 1)
    def _():
        hl_ref[0] = h
        ct_ref[0] = ubuf[0:SUBLANES, :]

    if emit_kv:
        kvn = _rms(xn, kvg_ref[...]).astype(BF16)
        kv = jnp.dot(kvn, wkv_ref[...], preferred_element_type=F32)
        k_ref[0] = _rope(kv[:, :D_KV], cos_ref[...], sin_ref[...])
        v_ref[0] = kv[:, D_KV:]


def _prompt_lru_layer(x, w, mk, mv, kv_w=None, tile=256):
    batch, seq, _ = x.shape
    emit_kv = kv_w is not None
    tok = lambda b, t: (b, t, 0)
    per_b = lambda b, t: (b, 0, 0)
    in_specs = [
        pl.BlockSpec((1, tile, D_MODEL), tok),
        _const_spec((1, D_MODEL)), _const_spec((D_MODEL, D_IN)), _const_spec((D_CAT, D_MODEL)),
        _const_spec((CONV_W, D_MAIN)), _const_spec((1, D_MAIN)),
        _const_spec((N_GATE_TILES, GATE_TILE, 2 * GATE_TILE)),
        _const_spec((1, D_MAIN)), _const_spec((1, D_MAIN)), _const_spec((1, D_MAIN)),
        pl.BlockSpec((1, N_MEM, D_MEMW), per_b), pl.BlockSpec((1, N_MEM, D_MEMW), per_b),
    ]
    args = [x, w["norm_g"], w["w_in"], w["w_out"], w["conv_w"], w["conv_b"], w["w_gate"],
            w["ba"], w["bx"], w["lam"], mk, mv]
    out_specs = [pl.BlockSpec((1, tile, D_MODEL), tok),
                 pl.BlockSpec((1, SUBLANES, LANES), per_b),
                 pl.BlockSpec((1, SUBLANES, D_MAIN), per_b)]
    out_shape = [jax.ShapeDtypeStruct((batch, seq, D_MODEL), F32),
                 jax.ShapeDtypeStruct((batch, SUBLANES, LANES), F32),
                 jax.ShapeDtypeStruct((batch, SUBLANES, D_MAIN), F32)]
    if emit_kv:
        in_specs += [_const_spec((1, D_MODEL)), _const_spec((D_MODEL, 2 * D_KV)),
                     pl.BlockSpec((tile, LANES), lambda b, t: (t, 0)),
                     pl.BlockSpec((tile, LANES), lambda b, t: (t, 0))]
        args += [kv_w["kv_norm_g"], kv_w["w_kv"], kv_w["cos"], kv_w["sin"]]
        out_specs += [pl.BlockSpec((1, tile, D_KV), tok)] * 2
        out_shape += [jax.ShapeDtypeStruct((batch, seq, D_KV), F32)] * 2
    groups = tile // SUBLANES
    return pl.pallas_call(
        functools.partial(_prompt_lru_kernel, emit_kv, tile),
        grid=(batch, seq // tile),
        in_specs=in_specs,
        out_specs=out_specs,
        out_shape=out_shape,
        scratch_shapes=[pltpu.VMEM((tile + SUBLANES, D_MAIN), F32),
                        pltpu.VMEM((groups, SUBLANES * N_SLABS, LANES), F32),
                        pltpu.VMEM((groups, SUBLANES * N_SLABS, LANES), F32),
                        pltpu.VMEM((groups, SUBLANES * N_SLABS, LANES), F32),
                        pltpu.VMEM((SUBLANES, LANES), F32),
                        pltpu.VMEM((tile, D_CAT), BF16)],
        compiler_params=pltpu.CompilerParams(
            dimension_semantics=("arbitrary", "arbitrary"), vmem_limit_bytes=VMEM_LIMIT_BYTES),
        name="prompt_lru_kv" if emit_kv else "prompt_lru",
    )(*args)


def _sink_softmax(s, sink):
    m = jnp.maximum(jnp.max(s, axis=-1, keepdims=True), sink)
    p = jnp.exp(s - m)
    den = jnp.sum(p, axis=-1, keepdims=True) + jnp.exp(sink - m)
    return p, den


def _prompt_swa_kernel(final, tile, *refs):
    n_in = 14 if final else 13
    (x_ref, ng_ref, win_ref, wout_ref, mk_ref, mv_ref, sink_ref, kp_ref, kc_ref, vp_ref, vc_ref,
     cos_ref, sin_ref) = refs[:13]
    if final:
        fg_ref = refs[13]
        xo_ref, y_ref = refs[n_in:n_in + 2]
        scratch = refs[n_in + 2:]
    else:
        xo_ref, = refs[n_in:n_in + 1]
        scratch = refs[n_in + 1:]
    z, = scratch
    blocks = tile // WINDOW
    t = pl.program_id(1)

    x = x_ref[0]
    hn = _rms(x, ng_ref[...]).astype(BF16)
    proj = jnp.dot(hn, win_ref[...], preferred_element_type=F32)

    qi = lax.broadcasted_iota(jnp.int32, (WINDOW, 2 * WINDOW), 0)
    ki = lax.broadcasted_iota(jnp.int32, (WINDOW, 2 * WINDOW), 1)
    band = (ki > qi) & (ki <= qi + WINDOW)
    lane = lax.broadcasted_iota(jnp.int32, (WINDOW, LANES), 1)

    for n in range(blocks):
        rows = slice(WINDOW * n, WINDOW * (n + 1))
        if n == 0:
            k_prev, v_prev = kp_ref[0], vp_ref[0]
            mask = band & ((ki >= WINDOW) | (t > 0))
        else:
            prev = slice(WINDOW * (n - 1), WINDOW * n)
            k_prev, v_prev = kc_ref[0, prev, :], vc_ref[0, prev, :]
            mask = band
        keys = jnp.concatenate([k_prev, kc_ref[0, rows, :]], axis=0)
        vals = jnp.concatenate([v_prev, vc_ref[0, rows, :]], axis=0)
        keys_r = pltpu.roll(keys, HEAD_DIM, 1)
        vals_r = pltpu.roll(vals, HEAD_DIM, 1)
        mask2 = jnp.concatenate([mask, mask], axis=1)
        cos = cos_ref[rows, :]
        sin = sin_ref[rows, :]
        for c in range(N_KV):
            kbd = _block_diag_pair(keys, keys_r, c)
            vbd = _block_diag_pair(vals, vals_r, c)
            for jj in range(GROUP // 2):
                j = c * (GROUP // 2) + jj
                cols = slice(LANES * j, LANES * (j + 1))
                q = _rope(proj[rows, cols], cos, sin).astype(BF16)
                s = lax.dot_general(q, kbd, _NT, preferred_element_type=F32) * (HEAD_DIM ** -0.5)
                s = jnp.where(mask2, s, NEG)
                p0, den0 = _sink_softmax(s[:, :2 * WINDOW], sink_ref[2 * j])
                p1, den1 = _sink_softmax(s[:, 2 * WINDOW:], sink_ref[2 * j + 1])
                p = jnp.concatenate([p0, p1], axis=1).astype(BF16)
                o = jnp.dot(p, vbd, preferred_element_type=F32)
                y = o * jnp.where(lane < HEAD_DIM, 1.0 / den0, 1.0 / den1)
                g_main = proj[rows, D_MAIN + LANES * j:D_MAIN + LANES * (j + 1)]
                z[rows, cols] = (y * _silu(g_main)).astype(BF16)

    for hh in range(MEM_HEADS):
        cols = slice(MEM_HD * hh, MEM_HD * (hh + 1))
        q = proj[:, 2 * D_MAIN + MEM_HD * hh:2 * D_MAIN + MEM_HD * (hh + 1)]
        om = _mem_attend(q, mk_ref[0, :, cols], mv_ref[0, :, cols])
        g_mem = proj[:, 2 * D_MAIN + D_MEMW + MEM_HD * hh:2 * D_MAIN + D_MEMW + MEM_HD * (hh + 1)]
        z[:, D_MAIN + MEM_HD * hh:D_MAIN + MEM_HD * (hh + 1)] = (om * _silu(g_mem)).astype(BF16)

    xn = x + jnp.dot(z[...], wout_ref[...], preferred_element_type=F32)
    xo_ref[0] = xn
    if final:
        y_ref[0] = _rms(xn, fg_ref[...])


def _prompt_swa_layer(x, w, mk, mv, k, v, cos, sin, final_g=None, tile=256):
    batch, seq, _ = x.shape
    final = final_g is not None
    blocks = tile // WINDOW
    tok = lambda b, t: (b, t, 0)
    per_b = lambda b, t: (b, 0, 0)
    prev_blk = lambda b, t: (b, jnp.maximum(t * blocks - 1, 0), 0)
    in_specs = [
        pl.BlockSpec((1, tile, D_MODEL), tok),
        _const_spec((1, D_MODEL)), _const_spec((D_MODEL, D_IN)), _const_spec((D_CAT, D_MODEL)),
        pl.BlockSpec((1, N_MEM, D_MEMW), per_b), pl.BlockSpec((1, N_MEM, D_MEMW), per_b),
        pl.BlockSpec(memory_space=pltpu.SMEM),
        pl.BlockSpec((1, WINDOW, D_KV), prev_blk), pl.BlockSpec((1, tile, D_KV), tok),
        pl.BlockSpec((1, WINDOW, D_KV), prev_blk), pl.BlockSpec((1, tile, D_KV), tok),
        pl.BlockSpec((tile, LANES), lambda b, t: (t, 0)), pl.BlockSpec((tile, LANES), lambda b, t: (t, 0)),
    ]
    args = [x, w["norm_g"], w["w_in"], w["w_out"], mk, mv, w["sinks"], k, k, v, v, cos, sin]
    out_specs = [pl.BlockSpec((1, tile, D_MODEL), tok)]
    out_shape = [jax.ShapeDtypeStruct((batch, seq, D_MODEL), F32)]
    if final:
        in_specs.append(_const_spec((1, D_MODEL)))
        args.append(final_g)
        out_specs.append(pl.BlockSpec((1, tile, D_MODEL), tok))
        out_shape.append(jax.ShapeDtypeStruct((batch, seq, D_MODEL), F32))
    return pl.pallas_call(
        functools.partial(_prompt_swa_kernel, final, tile),
        grid=(batch, seq // tile),
        in_specs=in_specs,
        out_specs=out_specs,
        out_shape=out_shape,
        scratch_shapes=[pltpu.VMEM((tile, D_CAT), BF16)],
        compiler_params=pltpu.CompilerParams(
            dimension_semantics=("arbitrary", "arbitrary"), vmem_limit_bytes=VMEM_LIMIT_BYTES),
        name="prompt_swa_final" if final else "prompt_swa",
    )(*args)


def _sample_mem_attend(b, bb, qm_s, mk_ref, mv_ref, om_s):
    row = lax.broadcasted_iota(jnp.int32, (SUBLANES, MEM_HD), 0)
    q = qm_s[pl.ds(b, 1), :]
    qh = jnp.zeros((SUBLANES, MEM_HD), F32)
    for hh in range(MEM_HEADS):
        q_head = jnp.broadcast_to(q[:, MEM_HD * hh:MEM_HD * (hh + 1)], (SUBLANES, MEM_HD))
        qh = jnp.where(row == hh, q_head, qh)
    s = lax.dot_general(qh.astype(BF16), mk_ref[bb].astype(BF16), _NT,
                        preferred_element_type=F32) * (MEM_HD ** -0.5)
    srow = lax.broadcasted_iota(jnp.int32, s.shape, 0)
    scol = lax.broadcasted_iota(jnp.int32, s.shape, 1)
    s = jnp.where((scol % MEM_HEADS) == srow, s, NEG)
    m = jnp.max(s, axis=-1, keepdims=True)
    p = jnp.exp(s - m)
    den = jnp.sum(p, axis=-1, keepdims=True)
    o = jnp.dot(p.astype(BF16), mv_ref[bb].astype(BF16), preferred_element_type=F32) / den
    om_s[pl.ds(b, 1), :] = jnp.concatenate([o[hh:hh + 1, :] for hh in range(MEM_HEADS)], axis=1)


def _sample_lru_kernel(seqs, x_ref, ng_ref, win_ref, wout_ref, cw_ref, cb_ref, wg_ref, ba_ref, bx_ref,
                       lam_ref, h0_ref, c0_ref, mk_ref, mv_ref, xo_ref, ho_ref, co_ref,
                       z_s, qm_s, gm_s, om_s):
    i = pl.program_id(0)

    @pl.when(i == 0)
    def _():
        hn = _rms(x_ref[...], ng_ref[...]).astype(BF16)
        proj = jnp.dot(hn, win_ref[...], preferred_element_type=F32)
        u = proj[:, :D_MAIN]
        c0 = c0_ref[...]
        taps = [c0[:, D_MAIN * k:D_MAIN * (k + 1)] for k in range(CONV_W - 1)] + [u]
        uc = cb_ref[...] + taps[0] * cw_ref[0:1, :]
        for k in range(1, CONV_W):
            uc = uc + taps[k] * cw_ref[k:k + 1, :]
        co_ref[...] = jnp.concatenate(taps[1:], axis=1)
        sp = _softplus(-lam_ref[...])
        ucb = uc.astype(BF16)
        for j in range(N_GATE_TILES):
            sl = slice(GATE_TILE * j, GATE_TILE * (j + 1))
            pre = jnp.dot(ucb[:, sl], wg_ref[j], preferred_element_type=F32)
            a, b = _rglru_coeffs(uc[:, sl], pre, ba_ref[:, sl], bx_ref[:, sl], sp[:, sl])
            h = a * h0_ref[:, sl] + b
            ho_ref[:, sl] = h
            z_s[:, sl] = (h * _silu(proj[:, D_MAIN + GATE_TILE * j:D_MAIN + GATE_TILE * (j + 1)])).astype(BF16)
        qm_s[...] = proj[:, 2 * D_MAIN:2 * D_MAIN + D_MEMW]
        gm_s[...] = _silu(proj[:, 2 * D_MAIN + D_MEMW:])

    for bb in range(seqs):
        _sample_mem_attend(i * seqs + bb, bb, qm_s, mk_ref, mv_ref, om_s)

    @pl.when(i == pl.num_programs(0) - 1)
    def _():
        z_s[:, D_MAIN:] = (om_s[...] * gm_s[...]).astype(BF16)
        xo_ref[...] = x_ref[...] + jnp.dot(z_s[...], wout_ref[...], preferred_element_type=F32)


def _cache_spec(layer, seqs):
    return pl.BlockSpec((None, seqs, N_MEM * MEM_HEADS, MEM_HD), lambda i: (layer, i, 0, 0))


def _sample_lru_layer(x, w, h0, c0, cache_k, cache_v, layer, seqs=8):
    batch = x.shape[0]
    cache_spec = _cache_spec(layer, seqs)
    return pl.pallas_call(
        functools.partial(_sample_lru_kernel, seqs),
        grid=(batch // seqs,),
        in_specs=[_const_spec((batch, D_MODEL)), _const_spec((1, D_MODEL)),
                  _const_spec((D_MODEL, D_IN)), _const_spec((D_CAT, D_MODEL)),
                  _const_spec((CONV_W, D_MAIN)), _const_spec((1, D_MAIN)),
                  _const_spec((N_GATE_TILES, GATE_TILE, 2 * GATE_TILE)),
                  _const_spec((1, D_MAIN)), _const_spec((1, D_MAIN)), _const_spec((1, D_MAIN)),
                  _const_spec((batch, D_MAIN)), _const_spec((batch, (CONV_W - 1) * D_MAIN)),
                  cache_spec, cache_spec],
        out_specs=[pl.BlockSpec((batch, D_MODEL), lambda i: (0, 0)),
                   pl.BlockSpec((batch, D_MAIN), lambda i: (0, 0)),
                   pl.BlockSpec((batch, (CONV_W - 1) * D_MAIN), lambda i: (0, 0))],
        out_shape=[jax.ShapeDtypeStruct((batch, D_MODEL), F32),
                   jax.ShapeDtypeStruct((batch, D_MAIN), F32),
                   jax.ShapeDtypeStruct((batch, (CONV_W - 1) * D_MAIN), F32)],
        scratch_shapes=[pltpu.VMEM((batch, D_CAT), BF16), pltpu.VMEM((batch, D_MEMW), F32),
                        pltpu.VMEM((batch, D_MEMW), F32), pltpu.VMEM((batch, D_MEMW), F32)],
        compiler_params=pltpu.CompilerParams(
            dimension_semantics=("arbitrary",), vmem_limit_bytes=VMEM_LIMIT_BYTES),
        name="sample_lru",
    )(x, w["norm_g"], w["w_in"], w["w_out"], w["conv_w"], w["conv_b"], w["w_gate"],
      w["ba"], w["bx"], w["lam"], h0, c0, cache_k, cache_v)


def _sample_swa_kernel(first, final, seqs, *refs):
    (x_ref, ng_ref, win_ref, wout_ref, sink_ref, mk_ref, mv_ref, wk_ref, wv_ref) = refs[:9]
    pos = 9
    if first:
        kvg_ref, wkv_ref, inv_ref, sign_ref = refs[pos:pos + 4]
        pos += 4
    else:
        inv_ref, sign_ref = refs[pos:pos + 2]
        pos += 2
    if final:
        fg_ref = refs[pos]
        pos += 1
    xo_ref = refs[pos]
    pos += 1
    if first:
        wko_ref, wvo_ref = refs[pos:pos + 2]
        pos += 2
    if final:
        y_ref = refs[pos]
        pos += 1
    z_s, q_s, gmain_s, y_s, qm_s, gm_s, om_s, kn_s, vn_s = refs[pos:]
    i = pl.program_id(0)

    @pl.when(i == 0)
    def _():
        x = x_ref[...]
        hn = _rms(x, ng_ref[...]).astype(BF16)
        proj = jnp.dot(hn, win_ref[...], preferred_element_type=F32)
        ang = float(PAST_LEN) * inv_ref[...]
        cos = jnp.cos(ang)
        sin = jnp.sin(ang) * sign_ref[...]
        for j in range(N_SLABS):
            cols = slice(LANES * j, LANES * (j + 1))
            q_s[:, cols] = _rope(proj[:, cols], cos, sin)
        gmain_s[...] = _silu(proj[:, D_MAIN:2 * D_MAIN])
        qm_s[...] = proj[:, 2 * D_MAIN:2 * D_MAIN + D_MEMW]
        gm_s[...] = _silu(proj[:, 2 * D_MAIN + D_MEMW:])
        if first:
            kvn = _rms(x, kvg_ref[...]).astype(BF16)
            kv = jnp.dot(kvn, wkv_ref[...], preferred_element_type=F32)
            kn_s[...] = _rope(kv[:, :D_KV], cos, sin)
            vn_s[...] = kv[:, D_KV:]

    row = lax.broadcasted_iota(jnp.int32, (N_HEADS, LANES), 0)
    lane = lax.broadcasted_iota(jnp.int32, (N_HEADS, LANES), 1)
    kv_half = (lane // HEAD_DIM) == (row // GROUP)
    in_place = (row % 2) == (row // GROUP)
    for bb in range(seqs):
        b = i * seqs + bb
        if first:
            kw = jnp.concatenate([wk_ref[bb, 1:, :], kn_s[pl.ds(b, 1), :]], axis=0)
            vw = jnp.concatenate([wv_ref[bb, 1:, :], vn_s[pl.ds(b, 1), :]], axis=0)
            wko_ref[bb] = kw
            wvo_ref[bb] = vw
        else:
            kw, vw = wk_ref[bb], wv_ref[bb]
        q = jnp.broadcast_to(q_s[pl.ds(b, 1), :], (N_HEADS, D_MAIN))
        e = jnp.zeros((N_HEADS, LANES), F32)
        for j in range(N_SLABS):
            e = e + jnp.where((row // 2) == j, q[:, LANES * j:LANES * (j + 1)], 0.0)
        qh = jnp.where(kv_half, jnp.where(in_place, e, pltpu.roll(e, HEAD_DIM, 1)), 0.0).astype(BF16)
        s = lax.dot_general(qh, kw.astype(BF16), _NT, preferred_element_type=F32) * (HEAD_DIM ** -0.5)
        p, den = _sink_softmax(s, sink_ref[...])
        o = jnp.dot(p.astype(BF16), vw.astype(BF16), preferred_element_type=F32) / den
        f = jnp.where(in_place, o, pltpu.roll(o, HEAD_DIM, 1))
        lane1 = lax.broadcasted_iota(jnp.int32, (1, LANES), 1)
        y_s[pl.ds(b, 1), :] = jnp.concatenate(
            [jnp.where(lane1 < HEAD_DIM, f[2 * j:2 * j + 1, :], f[2 * j + 1:2 * j + 2, :])
             for j in range(N_SLABS)], axis=1)
        _sample_mem_attend(b, bb, qm_s, mk_ref, mv_ref, om_s)

    @pl.when(i == pl.num_programs(0) - 1)
    def _():
        z_s[:, :D_MAIN] = (y_s[...] * gmain_s[...]).astype(BF16)
        z_s[:, D_MAIN:] = (om_s[...] * gm_s[...]).astype(BF16)
        xn = x_ref[...] + jnp.dot(z_s[...], wout_ref[...], preferred_element_type=F32)
        xo_ref[...] = xn
        if final:
            y_ref[...] = _rms(xn, fg_ref[...])


def _sample_swa_layer(x, w, cache_k, cache_v, layer, win_k, win_v, rope_w, kv_w=None, final_g=None, seqs=8):
    batch = x.shape[0]
    first = kv_w is not None
    final = final_g is not None
    cache_spec = _cache_spec(layer, seqs)
    win_spec = pl.BlockSpec((seqs, WINDOW, D_KV), lambda i: (i, 0, 0))
    whole = lambda shape: pl.BlockSpec(shape, lambda i: (0,) * len(shape))
    in_specs = [_const_spec((batch, D_MODEL)), _const_spec((1, D_MODEL)),
                _const_spec((D_MODEL, D_IN)), _const_spec((D_CAT, D_MODEL)),
                _const_spec((N_HEADS, 1)), cache_spec, cache_spec, win_spec, win_spec]
    args = [x, w["norm_g"], w["w_in"], w["w_out"], w["sinks"].reshape(N_HEADS, 1),
            cache_k, cache_v, win_k, win_v]
    if first:
        in_specs += [_const_spec((1, D_MODEL)), _const_spec((D_MODEL, 2 * D_KV))]
        args += [kv_w["kv_norm_g"], kv_w["w_kv"]]
    in_specs += [_const_spec((1, LANES)), _const_spec((1, LANES))]
    args += [rope_w["inv128"], rope_w["sign128"]]
    if final:
        in_specs.append(_const_spec((1, D_MODEL)))
        args.append(final_g)
    out_specs = [whole((batch, D_MODEL))]
    out_shape = [jax.ShapeDtypeStruct((batch, D_MODEL), F32)]
    if first:
        out_specs += [win_spec, win_spec]
        out_shape += [jax.ShapeDtypeStruct((batch, WINDOW, D_KV), F32)] * 2
    if final:
        out_specs.append(whole((batch, D_MODEL)))
        out_shape.append(jax.ShapeDtypeStruct((batch, D_MODEL), F32))
    return pl.pallas_call(
        functools.partial(_sample_swa_kernel, first, final, seqs),
        grid=(batch // seqs,),
        in_specs=in_specs,
        out_specs=out_specs,
        out_shape=out_shape,
        scratch_shapes=[pltpu.VMEM((batch, D_CAT), BF16), pltpu.VMEM((batch, D_MAIN), F32),
                        pltpu.VMEM((batch, D_MAIN), F32), pltpu.VMEM((batch, D_MAIN), F32),
                        pltpu.VMEM((batch, D_MEMW), F32), pltpu.VMEM((batch, D_MEMW), F32),
                        pltpu.VMEM((batch, D_MEMW), F32),
                        pltpu.VMEM((batch, D_KV), F32), pltpu.VMEM((batch, D_KV), F32)],
        compiler_params=pltpu.CompilerParams(
            dimension_semantics=("arbitrary",), vmem_limit_bytes=VMEM_LIMIT_BYTES),
        name="sample_swa" + ("_first" if first else "") + ("_final" if final else ""),
    )(*args)


def _sample_trunk(x, W, cache_mem_k, cache_mem_v, state_h, state_conv, win_k, win_v, seqs=8):
    batch = x.shape[0]
    x = x.reshape(batch, D_MODEL)
    ck = cache_mem_k.reshape(DEPTH, batch, N_MEM * MEM_HEADS, MEM_HD)
    cv = cache_mem_v.reshape(DEPTH, batch, N_MEM * MEM_HEADS, MEM_HD)
    hs, convs = [], []
    for l in range(N_A):
        c0 = state_conv[l].reshape(batch, (CONV_W - 1) * D_MAIN)
        x, h, c = _sample_lru_layer(x, W["layers"][l], state_h[l], c0, ck, cv, l, seqs=seqs)
        hs.append(h)
        convs.append(c.reshape(batch, CONV_W - 1, D_MAIN))
    wk = win_k.reshape(batch, WINDOW, D_KV)
    wv = win_v.reshape(batch, WINDOW, D_KV)
    kv_w = dict(kv_norm_g=W["kv_norm_g"], w_kv=W["w_kv"])
    y = None
    for l in range(N_A, DEPTH):
        outs = _sample_swa_layer(x, W["layers"][l], ck, cv, l, wk, wv, W,
                                 kv_w=kv_w if l == N_A else None,
                                 final_g=W["final_g"] if l == DEPTH - 1 else None, seqs=seqs)
        x = outs[0]
        if l == N_A:
            wk, wv = outs[1], outs[2]
        if l == DEPTH - 1:
            y = outs[-1]
    return (y.reshape(batch, 1, D_MODEL), jnp.stack(hs), jnp.stack(convs),
            wk.reshape(batch, WINDOW, N_KV, HEAD_DIM), wv.reshape(batch, WINDOW, N_KV, HEAD_DIM))


def _prompt_trunk(x, W, mkb, mvb, tile=256):
    batch, seq, _ = x.shape
    cos, sin = _rope_tables(seq, W["inv128"], W["sign128"])
    kv_w = dict(kv_norm_g=W["kv_norm_g"], w_kv=W["w_kv"], cos=cos, sin=sin)
    hs, convs = [], []
    k = v = None
    for l in range(N_A):
        outs = _prompt_lru_layer(x, W["layers"][l], mkb[l], mvb[l],
                                 kv_w=kv_w if l == N_A - 1 else None, tile=tile)
        x, h_last, conv_tail = outs[:3]
        if l == N_A - 1:
            k, v = outs[3:]
        hs.append(h_last.reshape(batch, D_MAIN))
        convs.append(conv_tail[:, SUBLANES - (CONV_W - 1):, :])
    y = None
    for l in range(N_A, DEPTH):
        last = l == DEPTH - 1
        outs = _prompt_swa_layer(x, W["layers"][l], mkb[l], mvb[l], k, v, cos, sin,
                                 final_g=W["final_g"] if last else None, tile=tile)
        x = outs[0]
        if last:
            y = outs[1]
    win_k = k[:, seq - WINDOW:, :].reshape(batch, WINDOW, N_KV, HEAD_DIM)
    win_v = v[:, seq - WINDOW:, :].reshape(batch, WINDOW, N_KV, HEAD_DIM)
    return y, jnp.stack(hs), jnp.stack(convs), win_k, win_v


def kernel(x_prompt, x_sample, cache_mem_k, cache_mem_v, state_lru_h, state_conv, cache_win_k, cache_win_v, mem_prompt, norm_g, w_in, w_out, mem_norm_g, w_mem_kv, conv_w, conv_b, lru_wa, lru_ba, lru_wx, lru_bx, lru_lambda, kv_norm_g, w_kv, sinks, final_g):
    W = _prepare_weights(norm_g, w_in, w_out, conv_w, conv_b, lru_wa, lru_ba, lru_wx, lru_bx, lru_lambda, kv_norm_g, w_kv, sinks, final_g)
    mk, mv, mkb, mvb = _mem_kv(mem_prompt, mem_norm_g, w_mem_kv.astype(BF16))
    y_p, h_p, conv_p, wk_p, wv_p = _prompt_trunk(x_prompt, W, mkb, mvb)
    y_s, h_s, conv_s, wk_s, wv_s = _sample_trunk(x_sample, W, cache_mem_k, cache_mem_v, state_lru_h,
                                                 state_conv, cache_win_k, cache_win_v)
    batch = x_prompt.shape[0]
    mk_p = mk.reshape(DEPTH, batch, N_MEM, MEM_HEADS, MEM_HD)
    mv_p = mv.reshape(DEPTH, batch, N_MEM, MEM_HEADS, MEM_HD)
    return (y_p, y_s, mk_p, mv_p, h_p, conv_p, wk_p, wv_p, h_s, conv_s, wk_s, wv_s)
```

```python
import functools

import jax
import jax.numpy as jnp
from jax import lax
from jax.experimental import pallas as pl
from jax.experimental.pallas import tpu as pltpu

F32 = jnp.float32
BF16 = jnp.bfloat16

D_MODEL = 1024
DEPTH = 4
N_A = DEPTH // 2
N_HEADS = 16
HEAD_DIM = 64
N_KV = 2
GROUP = N_HEADS // N_KV
D_MAIN = N_HEADS * HEAD_DIM
N_BLOCKS = 16
BLOCK_W = D_MAIN // N_BLOCKS
CONV_W = 4
LRU_C = 8.0
WINDOW = 128
ROPE_THETA = 10000.0
N_MEM = 256
MEM_HEADS = 4
MEM_HD = 128
D_MEMW = MEM_HEADS * MEM_HD
D_IN = 2 * D_MAIN + 2 * D_MEMW
D_CAT = D_MAIN + D_MEMW
D_KV = N_KV * HEAD_DIM
EPS = 1e-6
NEG = -1e30
LOG2_E = 1.4426950408889634
PAST_LEN = 8192

SUBLANES = 8
LANES = 128
N_SLABS = D_MAIN // LANES
GATE_TILE = 256
N_GATE_TILES = D_MAIN // GATE_TILE
VMEM_LIMIT_BYTES = 56 * 1024 * 1024
ATTN_SKEW = 4
MEM_CHUNK = 256
SCAN_CHUNKS = 16

_NT = (((1,), (1,)), ((), ()))


def _const_spec(shape):
    zeros = (0,) * len(shape)
    return pl.BlockSpec(shape, lambda *_: zeros, pipeline_mode=pl.Buffered(1))


def _rms(x, g):
    return x * lax.rsqrt(jnp.mean(x * x, axis=-1, keepdims=True) + EPS) * g


def _silu_of_half(hx):
    return hx * jnp.tanh(hx) + hx


def _softplus(x):
    return jnp.maximum(x, 0.0) + jnp.log1p(jnp.exp(-jnp.abs(x)))


def _decay_exponent(lam):
    return (-0.5 * LRU_C * LOG2_E) * _softplus(-lam)


def _rope(x, cos, sin_signed):
    lane = lax.broadcasted_iota(jnp.int32, x.shape, 1)
    first_half = (lane % HEAD_DIM) < (HEAD_DIM // 2)
    swapped = jnp.where(first_half,
                        pltpu.roll(x, LANES - HEAD_DIM // 2, 1),
                        pltpu.roll(x, HEAD_DIM // 2, 1))
    return x * cos + swapped * sin_signed


def _rglru_coeffs(uc, half_pre, half_ba, half_bx, c):
    half = half_pre.shape[-1] // 2
    t_r = jnp.tanh(half_pre[:, :half] + half_ba)
    t_i = jnp.tanh(half_pre[:, half:] + half_bx)
    a = jnp.exp2(c * t_r + c)
    w = 1.0 - a * a
    root = jnp.where(w > 0.0, w * lax.rsqrt(w), 0.0)
    return a, root * ((0.5 * t_i + 0.5) * uc)


def _mem_units(proj, mk_ref, mv_ref, z, tile):
    chunk = min(tile, MEM_CHUNK)
    units = []
    for ch in range(tile // chunk):
        for hh in range(MEM_HEADS):
            rows = slice(chunk * ch, chunk * (ch + 1))
            cols = slice(MEM_HD * hh, MEM_HD * (hh + 1))
            q_cols = slice(2 * D_MAIN + MEM_HD * hh, 2 * D_MAIN + MEM_HD * (hh + 1))
            g_cols = slice(2 * D_MAIN + D_MEMW + MEM_HD * hh, 2 * D_MAIN + D_MEMW + MEM_HD * (hh + 1))
            z_cols = slice(D_MAIN + MEM_HD * hh, D_MAIN + MEM_HD * (hh + 1))

            def scores(rows=rows, cols=cols, q_cols=q_cols):
                q = proj[rows, q_cols].astype(BF16)
                s = lax.dot_general(mk_ref[0, :, cols], q, _NT, preferred_element_type=F32) * (MEM_HD ** -0.5)
                m = jnp.max(s, axis=0, keepdims=True)
                p = jnp.exp(s - m)
                return p.astype(BF16), jnp.sum(p, axis=0, keepdims=True)

            def outputs(p, den, rows=rows, cols=cols, g_cols=g_cols, z_cols=z_cols):
                o_t = jnp.dot(mv_ref[0, cols, :], p, preferred_element_type=F32)
                z[rows, z_cols] = ((o_t / den).T * _silu_of_half(proj[rows, g_cols])).astype(BF16)

            units.append((scores, outputs))
    return units


def _run_skewed(units, skew, fillers=()):
    calls = []
    pending = []
    for scores, outputs in units:
        calls.append((scores, None))
        pending.append(outputs)
        if len(pending) > skew:
            calls.append((None, pending.pop(0)))
    calls += [(None, out) for out in pending]
    fillers = list(fillers)
    per_call = -(-len(fillers) // max(len(calls), 1))
    results = []
    for scores, outputs in calls:
        if scores is not None:
            results.append(scores())
        else:
            outputs(*results.pop(0))
        for _ in range(per_call):
            if fillers:
                fillers.pop(0)()
    for f in fillers:
        f()


def _block_diag_pair(x, x_rolled, kv_head):
    lane = lax.broadcasted_iota(jnp.int32, x.shape, 1)
    lo = lane < HEAD_DIM
    if kv_head == 0:
        top = jnp.where(lo, x, 0.0)
        bot = jnp.where(lo, 0.0, x_rolled)
    else:
        top = jnp.where(lo, x_rolled, 0.0)
        bot = jnp.where(lo, 0.0, x)
    return jnp.concatenate([top, bot], axis=0).astype(BF16)


def _gate_tiles(wa, wx):
    per_tile = GATE_TILE // BLOCK_W
    eye = jnp.eye(per_tile, dtype=wa.dtype)

    def dense(w):
        w4 = w.reshape(N_GATE_TILES, per_tile, BLOCK_W, BLOCK_W)
        return jnp.einsum("jicd,ik->jickd", w4, eye).reshape(N_GATE_TILES, GATE_TILE, GATE_TILE)

    return jnp.concatenate([dense(wa), dense(wx)], axis=-1).astype(BF16)


def _prepare_weights(norm_g, w_in, w_out, conv_w, conv_b, lru_wa, lru_ba, lru_wx, lru_bx, lru_lambda,
                     kv_norm_g, w_kv, sinks, final_g):
    col = jnp.arange(D_IN)
    is_gate = ((col >= D_MAIN) & (col < 2 * D_MAIN)) | (col >= 2 * D_MAIN + D_MEMW)
    w_in_b = (w_in * jnp.where(is_gate, 0.5, 1.0).astype(F32)).astype(BF16)
    w_out_b = w_out.astype(BF16)
    layers = []
    for l in range(DEPTH):
        w = dict(norm_g=norm_g[l].reshape(1, D_MODEL), w_in=w_in_b[l], w_out=w_out_b[l])
        if l < N_A:
            w.update(conv_w=conv_w[l], conv_b=conv_b[l].reshape(1, D_MAIN),
                     w_gate=_gate_tiles(0.5 * lru_wa[l], 0.5 * lru_wx[l]),
                     ba=0.5 * lru_ba[l].reshape(1, D_MAIN), bx=0.5 * lru_bx[l].reshape(1, D_MAIN),
                     lam=lru_lambda[l].reshape(1, D_MAIN))
        else:
            w.update(sinks=sinks[l - N_A])
        layers.append(w)
    half = HEAD_DIM // 2
    inv = ROPE_THETA ** (-jnp.arange(half, dtype=F32) / half)
    inv128 = jnp.tile(inv, LANES // half).reshape(1, LANES)
    sign128 = jnp.tile(jnp.concatenate([-jnp.ones((half,), F32), jnp.ones((half,), F32)]),
                       LANES // HEAD_DIM).reshape(1, LANES)
    return dict(layers=layers, kv_norm_g=kv_norm_g.reshape(1, D_MODEL), w_kv=w_kv.astype(BF16),
                final_g=final_g.reshape(1, D_MODEL), inv128=inv128, sign128=sign128)


def _rope_table_kernel(inv_ref, sign_ref, cos_ref, sin_ref):
    rows = cos_ref.shape[0]
    pos = (pl.program_id(0) * rows + lax.broadcasted_iota(jnp.int32, (rows, LANES), 0)).astype(F32)
    ang = pos * inv_ref[...]
    cos_ref[...] = jnp.cos(ang)
    sin_ref[...] = jnp.sin(ang) * sign_ref[...]


def _rope_tables(seq, inv128, sign128, rows=1024):
    rows = min(rows, seq)
    return pl.pallas_call(
        _rope_table_kernel,
        grid=(seq // rows,),
        in_specs=[_const_spec((1, LANES)), _const_spec((1, LANES))],
        out_specs=[pl.BlockSpec((rows, LANES), lambda i: (i, 0)),
                   pl.BlockSpec((rows, LANES), lambda i: (i, 0))],
        out_shape=[jax.ShapeDtypeStruct((seq, LANES), F32)] * 2,
        name="rope_tables",
    )(inv128, sign128)


def _mem_kv_kernel(m_ref, g_ref, w_ref, k_ref, v_ref, kb_ref, vb_ref):
    mn = _rms(m_ref[0], g_ref[0]).astype(BF16)
    kv = jnp.dot(mn, w_ref[0], preferred_element_type=F32)
    k, v = kv[:, :D_MEMW], kv[:, D_MEMW:]
    k_ref[0, 0] = k
    v_ref[0, 0] = v
    kb_ref[0, 0] = k.astype(BF16)
    vb_ref[0, 0] = v.T.astype(BF16)


def _mem_kv(mem_prompt, mem_norm_g, w_mem_kv_bf16):
    batch = mem_prompt.shape[0]
    out_spec = pl.BlockSpec((1, 1, N_MEM, D_MEMW), lambda l, b: (l, b, 0, 0))
    out_spec_t = pl.BlockSpec((1, 1, D_MEMW, N_MEM), lambda l, b: (l, b, 0, 0))
    return pl.pallas_call(
        _mem_kv_kernel,
        grid=(DEPTH, batch),
        in_specs=[pl.BlockSpec((1, N_MEM, D_MODEL), lambda l, b: (b, 0, 0)),
                  pl.BlockSpec((1, 1, D_MODEL), lambda l, b: (l, 0, 0)),
                  pl.BlockSpec((1, D_MODEL, 2 * D_MEMW), lambda l, b: (l, 0, 0))],
        out_specs=[out_spec, out_spec, out_spec, out_spec_t],
        out_shape=[jax.ShapeDtypeStruct((DEPTH, batch, N_MEM, D_MEMW), F32)] * 2
                  + [jax.ShapeDtypeStruct((DEPTH, batch, N_MEM, D_MEMW), BF16),
                     jax.ShapeDtypeStruct((DEPTH, batch, D_MEMW, N_MEM), BF16)],
        name="mem_kv",
    )(mem_prompt, mem_norm_g.reshape(DEPTH, 1, D_MODEL), w_mem_kv_bf16)


def _prompt_lru_kernel(emit_kv, tile, *refs):
    n_in = 16 if emit_kv else 12
    (x_ref, ng_ref, win_ref, wout_ref, cw_ref, cb_ref, wg_ref, ba_ref, bx_ref, lam_ref,
     mk_ref, mv_ref) = refs[:12]
    if emit_kv:
        kvg_ref, wkv_ref, cos_ref, sin_ref = refs[12:16]
        xo_ref, hl_ref, ct_ref, k_ref, v_ref = refs[n_in:n_in + 5]
        scratch = refs[n_in + 5:]
    else:
        xo_ref, hl_ref, ct_ref = refs[n_in:n_in + 3]
        scratch = refs[n_in + 3:]
    ubuf, a3, b3, h3, hc, z = scratch
    groups = tile // SUBLANES
    t = pl.program_id(1)

    @pl.when(t == 0)
    def _():
        ubuf[0:SUBLANES, :] = jnp.zeros((SUBLANES, D_MAIN), F32)
        hc[...] = jnp.zeros((SUBLANES, LANES), F32)

    x = x_ref[0]
    hn = _rms(x, ng_ref[...]).astype(BF16)
    proj = jnp.dot(hn, win_ref[...], preferred_element_type=F32)

    u = proj[:, :D_MAIN]
    ubuf[SUBLANES:SUBLANES + tile, :] = u
    uc = cb_ref[...] + ubuf[SUBLANES - 3:SUBLANES - 3 + tile, :] * cw_ref[0:1, :]
    uc = uc + ubuf[SUBLANES - 2:SUBLANES - 2 + tile, :] * cw_ref[1:2, :]
    uc = uc + ubuf[SUBLANES - 1:SUBLANES - 1 + tile, :] * cw_ref[2:3, :]
    uc = uc + u * cw_ref[3:4, :]
    ubuf[0:SUBLANES, :] = ubuf[tile:tile + SUBLANES, :]

    decay_c = _decay_exponent(lam_ref[...])
    ucb = uc.astype(BF16)
    for j in range(N_GATE_TILES):
        sl = slice(GATE_TILE * j, GATE_TILE * (j + 1))
        pre = jnp.dot(ucb[:, sl], wg_ref[j], preferred_element_type=F32)
        a, b = _rglru_coeffs(uc[:, sl], pre, ba_ref[:, sl], bx_ref[:, sl], decay_c[:, sl])
        for jj in range(GATE_TILE // LANES):
            slab = j * (GATE_TILE // LANES) + jj
            rows = slice(SUBLANES * slab, SUBLANES * (slab + 1))
            cols = slice(LANES * jj, LANES * (jj + 1))
            a3[:, rows, :] = a[:, cols].reshape(groups, SUBLANES, LANES)
            b3[:, rows, :] = b[:, cols].reshape(groups, SUBLANES, LANES)

    carry = [hc[...]]

    def scan_groups(first, count):
        def run():
            h = carry[0]
            for g in range(first, first + count):
                for r in range(SUBLANES):
                    step = pl.ds(r, N_SLABS, stride=SUBLANES)
                    h = a3[g, step, :] * h + b3[g, step, :]
                    h3[g, step, :] = h
            carry[0] = h
        return run

    scan_chunk = max(groups // SCAN_CHUNKS, 1)
    _run_skewed(_mem_units(proj, mk_ref, mv_ref, z, tile), ATTN_SKEW,
                fillers=[scan_groups(g0, scan_chunk) for g0 in range(0, groups, scan_chunk)])
    h = carry[0]
    hc[...] = h

    for j in range(N_SLABS):
        cols = slice(LANES * j, LANES * (j + 1))
        y = h3[:, SUBLANES * j:SUBLANES * (j + 1), :].reshape(tile, LANES)
        g_main = proj[:, D_MAIN + LANES * j:D_MAIN + LANES * (j + 1)]
        z[:, cols] = (y * _silu_of_half(g_main)).astype(BF16)

    xn = x + jnp.dot(z[...], wout_ref[...], preferred_element_type=F32)
    xo_ref[0] = xn

    @pl.when(t == pl.num_programs(1) - 1)
    def _():
        hl_ref[0] = h
        ct_ref[0] = ubuf[0:SUBLANES, :]

    if emit_kv:
        kvn = _rms(xn, kvg_ref[...]).astype(BF16)
        kv = jnp.dot(kvn, wkv_ref[...], preferred_element_type=F32)
        k_ref[0] = _rope(kv[:, :D_KV], cos_ref[...], sin_ref[...])
        v_ref[0] = kv[:, D_KV:]


def _prompt_lru_layer(x, w, mk, mv, kv_w=None, tile=256):
    batch, seq, _ = x.shape
    emit_kv = kv_w is not None
    tok = lambda b, t: (b, t, 0)
    per_b = lambda b, t: (b, 0, 0)
    in_specs = [
        pl.BlockSpec((1, tile, D_MODEL), tok),
        _const_spec((1, D_MODEL)), _const_spec((D_MODEL, D_IN)), _const_spec((D_CAT, D_MODEL)),
        _const_spec((CONV_W, D_MAIN)), _const_spec((1, D_MAIN)),
        _const_spec((N_GATE_TILES, GATE_TILE, 2 * GATE_TILE)),
        _const_spec((1, D_MAIN)), _const_spec((1, D_MAIN)), _const_spec((1, D_MAIN)),
        pl.BlockSpec((1, N_MEM, D_MEMW), per_b), pl.BlockSpec((1, D_MEMW, N_MEM), per_b),
    ]
    args = [x, w["norm_g"], w["w_in"], w["w_out"], w["conv_w"], w["conv_b"], w["w_gate"],
            w["ba"], w["bx"], w["lam"], mk, mv]
    out_specs = [pl.BlockSpec((1, tile, D_MODEL), tok),
                 pl.BlockSpec((1, SUBLANES, LANES), per_b),
                 pl.BlockSpec((1, SUBLANES, D_MAIN), per_b)]
    out_shape = [jax.ShapeDtypeStruct((batch, seq, D_MODEL), F32),
                 jax.ShapeDtypeStruct((batch, SUBLANES, LANES), F32),
                 jax.ShapeDtypeStruct((batch, SUBLANES, D_MAIN), F32)]
    if emit_kv:
        in_specs += [_const_spec((1, D_MODEL)), _const_spec((D_MODEL, 2 * D_KV)),
                     pl.BlockSpec((tile, LANES), lambda b, t: (t, 0)),
                     pl.BlockSpec((tile, LANES), lambda b, t: (t, 0))]
        args += [kv_w["kv_norm_g"], kv_w["w_kv"], kv_w["cos"], kv_w["sin"]]
        out_specs += [pl.BlockSpec((1, tile, D_KV), tok)] * 2
        out_shape += [jax.ShapeDtypeStruct((batch, seq, D_KV), F32)] * 2
    groups = tile // SUBLANES
    return pl.pallas_call(
        functools.partial(_prompt_lru_kernel, emit_kv, tile),
        grid=(batch, seq // tile),
        in_specs=in_specs,
        out_specs=out_specs,
        out_shape=out_shape,
        scratch_shapes=[pltpu.VMEM((tile + SUBLANES, D_MAIN), F32),
                        pltpu.VMEM((groups, SUBLANES * N_SLABS, LANES), F32),
                        pltpu.VMEM((groups, SUBLANES * N_SLABS, LANES), F32),
                        pltpu.VMEM((groups, SUBLANES * N_SLABS, LANES), F32),
                        pltpu.VMEM((SUBLANES, LANES), F32),
                        pltpu.VMEM((tile, D_CAT), BF16)],
        compiler_params=pltpu.CompilerParams(
            dimension_semantics=("arbitrary", "arbitrary"), vmem_limit_bytes=VMEM_LIMIT_BYTES),
        name="prompt_lru_kv" if emit_kv else "prompt_lru",
    )(*args)


def _sink_softmax(s, sink, axis=-1):
    m = jnp.maximum(jnp.max(s, axis=axis, keepdims=True), sink)
    p = jnp.exp(s - m)
    den = jnp.sum(p, axis=axis, keepdims=True) + jnp.exp(sink - m)
    return p, den


def _prompt_swa_kernel(final, tile, *refs):
    n_in = 14 if final else 13
    (x_ref, ng_ref, win_ref, wout_ref, mk_ref, mv_ref, sink_ref, kp_ref, kc_ref, vp_ref, vc_ref,
     cos_ref, sin_ref) = refs[:13]
    if final:
        fg_ref = refs[13]
        xo_ref, y_ref = refs[n_in:n_in + 2]
        scratch = refs[n_in + 2:]
    else:
        xo_ref, = refs[n_in:n_in + 1]
        scratch = refs[n_in + 1:]
    z, qz, kcat, vcat_t = scratch
    blocks = tile // WINDOW
    t = pl.program_id(1)

    x = x_ref[0]
    hn = _rms(x, ng_ref[...]).astype(BF16)
    proj = jnp.dot(hn, win_ref[...], preferred_element_type=F32)

    cos = cos_ref[...]
    sin = sin_ref[...]
    lane = lax.broadcasted_iota(jnp.int32, (tile, LANES), 1)
    lo = lane < HEAD_DIM
    first_half = (lane % HEAD_DIM) < (HEAD_DIM // 2)
    for j in range(N_SLABS):
        c = j // (GROUP // 2)
        xq = proj[:, LANES * j:LANES * (j + 1)] * (HEAD_DIM ** -0.5)
        r32 = pltpu.roll(xq, HEAD_DIM // 2, 1)
        r96 = pltpu.roll(xq, LANES - HEAD_DIM // 2, 1)
        r64 = pltpu.roll(xq, HEAD_DIM, 1)
        stay = xq * cos + jnp.where(first_half, r96, r32) * sin
        move = r64 * cos + jnp.where(first_half, r32, r96) * sin
        keep = lo if c == 0 else ~lo
        even, odd = (stay, move) if c == 0 else (move, stay)
        qz[:, 2 * LANES * j:2 * LANES * j + LANES] = jnp.where(keep, even, 0.0).astype(BF16)
        qz[:, 2 * LANES * j + LANES:2 * LANES * (j + 1)] = jnp.where(keep, odd, 0.0).astype(BF16)

    kcat[0:WINDOW, :] = kp_ref[0].astype(BF16)
    kcat[WINDOW:, :] = kc_ref[0].astype(BF16)
    vcat_t[:, 0:WINDOW] = vp_ref[0].T.astype(BF16)
    vcat_t[:, WINDOW:] = vc_ref[0].T.astype(BF16)

    ki = lax.broadcasted_iota(jnp.int32, (2 * WINDOW, WINDOW), 0)
    qi = lax.broadcasted_iota(jnp.int32, (2 * WINDOW, WINDOW), 1)
    band = (ki > qi) & (ki <= qi + WINDOW)

    mask_first = jnp.concatenate([band & ((ki >= WINDOW) | (t > 0))] * 2, axis=1)
    mask_rest = jnp.concatenate([band] * 2, axis=1)

    def scores(n, j):
        rows = slice(WINDOW * n, WINDOW * (n + 1))
        keys = kcat[WINDOW * n:WINDOW * (n + 2), :]
        q2 = jnp.concatenate([qz[rows, 2 * LANES * j:2 * LANES * j + LANES],
                              qz[rows, 2 * LANES * j + LANES:2 * LANES * (j + 1)]], axis=0)
        s = lax.dot_general(keys, q2, _NT, preferred_element_type=F32)
        s = jnp.where(mask_first if n == 0 else mask_rest, s, NEG)
        p0, den0 = _sink_softmax(s[:, :WINDOW], sink_ref[2 * j], axis=0)
        p1, den1 = _sink_softmax(s[:, WINDOW:], sink_ref[2 * j + 1], axis=0)
        return jnp.concatenate([p0, p1], axis=1).astype(BF16), den0, den1

    def outputs(n, j, p, den0, den1):
        rows = slice(WINDOW * n, WINDOW * (n + 1))
        vals_t = vcat_t[:, WINDOW * n:WINDOW * (n + 2)]
        o_t = jnp.dot(vals_t, p, preferred_element_type=F32)
        c = j // (GROUP // 2)
        own = slice(HEAD_DIM * c, HEAD_DIM * (c + 1))
        y_t = jnp.concatenate([o_t[own, :WINDOW] / den0, o_t[own, WINDOW:] / den1], axis=0)
        g_main = proj[rows, D_MAIN + LANES * j:D_MAIN + LANES * (j + 1)]
        z[rows, LANES * j:LANES * (j + 1)] = (y_t.T * _silu_of_half(g_main)).astype(BF16)

    units = [(functools.partial(scores, n, j), functools.partial(outputs, n, j))
             for n in range(blocks) for j in range(N_SLABS)]
    _run_skewed(units + _mem_units(proj, mk_ref, mv_ref, z, tile), ATTN_SKEW)

    xn = x + jnp.dot(z[...], wout_ref[...], preferred_element_type=F32)
    xo_ref[0] = xn
    if final:
        y_ref[0] = _rms(xn, fg_ref[...])


def _prompt_swa_layer(x, w, mk, mv, k, v, cos, sin, final_g=None, tile=256):
    batch, seq, _ = x.shape
    final = final_g is not None
    blocks = tile // WINDOW
    tok = lambda b, t: (b, t, 0)
    per_b = lambda b, t: (b, 0, 0)
    prev_blk = lambda b, t: (b, jnp.maximum(t * blocks - 1, 0), 0)
    in_specs = [
        pl.BlockSpec((1, tile, D_MODEL), tok),
        _const_spec((1, D_MODEL)), _const_spec((D_MODEL, D_IN)), _const_spec((D_CAT, D_MODEL)),
        pl.BlockSpec((1, N_MEM, D_MEMW), per_b), pl.BlockSpec((1, D_MEMW, N_MEM), per_b),
        pl.BlockSpec(memory_space=pltpu.SMEM),
        pl.BlockSpec((1, WINDOW, D_KV), prev_blk), pl.BlockSpec((1, tile, D_KV), tok),
        pl.BlockSpec((1, WINDOW, D_KV), prev_blk), pl.BlockSpec((1, tile, D_KV), tok),
        pl.BlockSpec((tile, LANES), lambda b, t: (t, 0)), pl.BlockSpec((tile, LANES), lambda b, t: (t, 0)),
    ]
    args = [x, w["norm_g"], w["w_in"], w["w_out"], mk, mv, w["sinks"], k, k, v, v, cos, sin]
    out_specs = [pl.BlockSpec((1, tile, D_MODEL), tok)]
    out_shape = [jax.ShapeDtypeStruct((batch, seq, D_MODEL), F32)]
    if final:
        in_specs.append(_const_spec((1, D_MODEL)))
        args.append(final_g)
        out_specs.append(pl.BlockSpec((1, tile, D_MODEL), tok))
        out_shape.append(jax.ShapeDtypeStruct((batch, seq, D_MODEL), F32))
    return pl.pallas_call(
        functools.partial(_prompt_swa_kernel, final, tile),
        grid=(batch, seq // tile),
        in_specs=in_specs,
        out_specs=out_specs,
        out_shape=out_shape,
        scratch_shapes=[pltpu.VMEM((tile, D_CAT), BF16),
                        pltpu.VMEM((tile, 2 * D_MAIN), BF16),
                        pltpu.VMEM((WINDOW + tile, D_KV), BF16),
                        pltpu.VMEM((D_KV, WINDOW + tile), BF16)],
        compiler_params=pltpu.CompilerParams(
            dimension_semantics=("arbitrary", "arbitrary"), vmem_limit_bytes=VMEM_LIMIT_BYTES),
        name="prompt_swa_final" if final else "prompt_swa",
    )(*args)


def _sample_mem_attend(b, bb, qm_s, mk_ref, mv_ref, om_s):
    row = lax.broadcasted_iota(jnp.int32, (SUBLANES, MEM_HD), 0)
    q = qm_s[pl.ds(b, 1), :]
    qh = jnp.zeros((SUBLANES, MEM_HD), F32)
    for hh in range(MEM_HEADS):
        q_head = jnp.broadcast_to(q[:, MEM_HD * hh:MEM_HD * (hh + 1)], (SUBLANES, MEM_HD))
        qh = jnp.where(row == hh, q_head, qh)
    s = lax.dot_general(qh.astype(BF16), mk_ref[bb].astype(BF16), _NT,
                        preferred_element_type=F32) * (MEM_HD ** -0.5)
    srow = lax.broadcasted_iota(jnp.int32, s.shape, 0)
    scol = lax.broadcasted_iota(jnp.int32, s.shape, 1)
    s = jnp.where((scol % MEM_HEADS) == srow, s, NEG)
    m = jnp.max(s, axis=-1, keepdims=True)
    p = jnp.exp(s - m)
    den = jnp.sum(p, axis=-1, keepdims=True)
    o = jnp.dot(p.astype(BF16), mv_ref[bb].astype(BF16), preferred_element_type=F32) / den
    om_s[pl.ds(b, 1), :] = jnp.concatenate([o[hh:hh + 1, :] for hh in range(MEM_HEADS)], axis=1)


def _sample_lru_kernel(seqs, x_ref, ng_ref, win_ref, wout_ref, cw_ref, cb_ref, wg_ref, ba_ref, bx_ref,
                       lam_ref, h0_ref, c0_ref, mk_ref, mv_ref, xo_ref, ho_ref, co_ref,
                       z_s, qm_s, gm_s, om_s):
    i = pl.program_id(0)

    @pl.when(i == 0)
    def _():
        hn = _rms(x_ref[...], ng_ref[...]).astype(BF16)
        proj = jnp.dot(hn, win_ref[...], preferred_element_type=F32)
        u = proj[:, :D_MAIN]
        c0 = c0_ref[...]
        taps = [c0[:, D_MAIN * k:D_MAIN * (k + 1)] for k in range(CONV_W - 1)] + [u]
        uc = cb_ref[...] + taps[0] * cw_ref[0:1, :]
        for k in range(1, CONV_W):
            uc = uc + taps[k] * cw_ref[k:k + 1, :]
        co_ref[...] = jnp.concatenate(taps[1:], axis=1)
        decay_c = _decay_exponent(lam_ref[...])
        ucb = uc.astype(BF16)
        for j in range(N_GATE_TILES):
            sl = slice(GATE_TILE * j, GATE_TILE * (j + 1))
            pre = jnp.dot(ucb[:, sl], wg_ref[j], preferred_element_type=F32)
            a, b = _rglru_coeffs(uc[:, sl], pre, ba_ref[:, sl], bx_ref[:, sl], decay_c[:, sl])
            h = a * h0_ref[:, sl] + b
            ho_ref[:, sl] = h
            z_s[:, sl] = (h * _silu_of_half(proj[:, D_MAIN + GATE_TILE * j:D_MAIN + GATE_TILE * (j + 1)])).astype(BF16)
        qm_s[...] = proj[:, 2 * D_MAIN:2 * D_MAIN + D_MEMW]
        gm_s[...] = _silu_of_half(proj[:, 2 * D_MAIN + D_MEMW:])

    for bb in range(seqs):
        _sample_mem_attend(i * seqs + bb, bb, qm_s, mk_ref, mv_ref, om_s)

    @pl.when(i == pl.num_programs(0) - 1)
    def _():
        z_s[:, D_MAIN:] = (om_s[...] * gm_s[...]).astype(BF16)
        xo_ref[...] = x_ref[...] + jnp.dot(z_s[...], wout_ref[...], preferred_element_type=F32)


def _cache_spec(layer, seqs):
    return pl.BlockSpec((None, seqs, N_MEM * MEM_HEADS, MEM_HD), lambda i: (layer, i, 0, 0))


def _sample_lru_layer(x, w, h0, c0, cache_k, cache_v, layer, seqs=8):
    batch = x.shape[0]
    cache_spec = _cache_spec(layer, seqs)
    return pl.pallas_call(
        functools.partial(_sample_lru_kernel, seqs),
        grid=(batch // seqs,),
        in_specs=[_const_spec((batch, D_MODEL)), _const_spec((1, D_MODEL)),
                  _const_spec((D_MODEL, D_IN)), _const_spec((D_CAT, D_MODEL)),
                  _const_spec((CONV_W, D_MAIN)), _const_spec((1, D_MAIN)),
                  _const_spec((N_GATE_TILES, GATE_TILE, 2 * GATE_TILE)),
                  _const_spec((1, D_MAIN)), _const_spec((1, D_MAIN)), _const_spec((1, D_MAIN)),
                  _const_spec((batch, D_MAIN)), _const_spec((batch, (CONV_W - 1) * D_MAIN)),
                  cache_spec, cache_spec],
        out_specs=[pl.BlockSpec((batch, D_MODEL), lambda i: (0, 0)),
                   pl.BlockSpec((batch, D_MAIN), lambda i: (0, 0)),
                   pl.BlockSpec((batch, (CONV_W - 1) * D_MAIN), lambda i: (0, 0))],
        out_shape=[jax.ShapeDtypeStruct((batch, D_MODEL), F32),
                   jax.ShapeDtypeStruct((batch, D_MAIN), F32),
                   jax.ShapeDtypeStruct((batch, (CONV_W - 1) * D_MAIN), F32)],
        scratch_shapes=[pltpu.VMEM((batch, D_CAT), BF16), pltpu.VMEM((batch, D_MEMW), F32),
                        pltpu.VMEM((batch, D_MEMW), F32), pltpu.VMEM((batch, D_MEMW), F32)],
        compiler_params=pltpu.CompilerParams(
            dimension_semantics=("arbitrary",), vmem_limit_bytes=VMEM_LIMIT_BYTES),
        name="sample_lru",
    )(x, w["norm_g"], w["w_in"], w["w_out"], w["conv_w"], w["conv_b"], w["w_gate"],
      w["ba"], w["bx"], w["lam"], h0, c0, cache_k, cache_v)


def _sample_swa_kernel(first, final, seqs, *refs):
    (x_ref, ng_ref, win_ref, wout_ref, sink_ref, mk_ref, mv_ref, wk_ref, wv_ref) = refs[:9]
    pos = 9
    if first:
        kvg_ref, wkv_ref, inv_ref, sign_ref = refs[pos:pos + 4]
        pos += 4
    else:
        inv_ref, sign_ref = refs[pos:pos + 2]
        pos += 2
    if final:
        fg_ref = refs[pos]
        pos += 1
    xo_ref = refs[pos]
    pos += 1
    if first:
        wko_ref, wvo_ref = refs[pos:pos + 2]
        pos += 2
    if final:
        y_ref = refs[pos]
        pos += 1
    z_s, q_s, gmain_s, y_s, qm_s, gm_s, om_s, kn_s, vn_s = refs[pos:]
    i = pl.program_id(0)

    @pl.when(i == 0)
    def _():
        x = x_ref[...]
        hn = _rms(x, ng_ref[...]).astype(BF16)
        proj = jnp.dot(hn, win_ref[...], preferred_element_type=F32)
        ang = float(PAST_LEN) * inv_ref[...]
        cos = jnp.cos(ang)
        sin = jnp.sin(ang) * sign_ref[...]
        for j in range(N_SLABS):
            cols = slice(LANES * j, LANES * (j + 1))
            q_s[:, cols] = _rope(proj[:, cols], cos, sin)
        gmain_s[...] = _silu_of_half(proj[:, D_MAIN:2 * D_MAIN])
        qm_s[...] = proj[:, 2 * D_MAIN:2 * D_MAIN + D_MEMW]
        gm_s[...] = _silu_of_half(proj[:, 2 * D_MAIN + D_MEMW:])
        if first:
            kvn = _rms(x, kvg_ref[...]).astype(BF16)
            kv = jnp.dot(kvn, wkv_ref[...], preferred_element_type=F32)
            kn_s[...] = _rope(kv[:, :D_KV], cos, sin)
            vn_s[...] = kv[:, D_KV:]

    row = lax.broadcasted_iota(jnp.int32, (N_HEADS, LANES), 0)
    lane = lax.broadcasted_iota(jnp.int32, (N_HEADS, LANES), 1)
    kv_half = (lane // HEAD_DIM) == (row // GROUP)
    in_place = (row % 2) == (row // GROUP)
    for bb in range(seqs):
        b = i * seqs + bb
        if first:
            kw = jnp.concatenate([wk_ref[bb, 1:, :], kn_s[pl.ds(b, 1), :]], axis=0)
            vw = jnp.concatenate([wv_ref[bb, 1:, :], vn_s[pl.ds(b, 1), :]], axis=0)
            wko_ref[bb] = kw
            wvo_ref[bb] = vw
        else:
            kw, vw = wk_ref[bb], wv_ref[bb]
        q = jnp.broadcast_to(q_s[pl.ds(b, 1), :], (N_HEADS, D_MAIN))
        e = jnp.zeros((N_HEADS, LANES), F32)
        for j in range(N_SLABS):
            e = e + jnp.where((row // 2) == j, q[:, LANES * j:LANES * (j + 1)], 0.0)
        qh = jnp.where(kv_half, jnp.where(in_place, e, pltpu.roll(e, HEAD_DIM, 1)), 0.0).astype(BF16)
        s = lax.dot_general(qh, kw.astype(BF16), _NT, preferred_element_type=F32) * (HEAD_DIM ** -0.5)
        p, den = _sink_softmax(s, sink_ref[...])
        o = jnp.dot(p.astype(BF16), vw.astype(BF16), preferred_element_type=F32) / den
        f = jnp.where(in_place, o, pltpu.roll(o, HEAD_DIM, 1))
        lane1 = lax.broadcasted_iota(jnp.int32, (1, LANES), 1)
        y_s[pl.ds(b, 1), :] = jnp.concatenate(
            [jnp.where(lane1 < HEAD_DIM, f[2 * j:2 * j + 1, :], f[2 * j + 1:2 * j + 2, :])
             for j in range(N_SLABS)], axis=1)
        _sample_mem_attend(b, bb, qm_s, mk_ref, mv_ref, om_s)

    @pl.when(i == pl.num_programs(0) - 1)
    def _():
        z_s[:, :D_MAIN] = (y_s[...] * gmain_s[...]).astype(BF16)
        z_s[:, D_MAIN:] = (om_s[...] * gm_s[...]).astype(BF16)
        xn = x_ref[...] + jnp.dot(z_s[...], wout_ref[...], preferred_element_type=F32)
        xo_ref[...] = xn
        if final:
            y_ref[...] = _rms(xn, fg_ref[...])


def _sample_swa_layer(x, w, cache_k, cache_v, layer, win_k, win_v, rope_w, kv_w=None, final_g=None, seqs=8):
    batch = x.shape[0]
    first = kv_w is not None
    final = final_g is not None
    cache_spec = _cache_spec(layer, seqs)
    win_spec = pl.BlockSpec((seqs, WINDOW, D_KV), lambda i: (i, 0, 0))
    whole = lambda shape: pl.BlockSpec(shape, lambda i: (0,) * len(shape))
    in_specs = [_const_spec((batch, D_MODEL)), _const_spec((1, D_MODEL)),
                _const_spec((D_MODEL, D_IN)), _const_spec((D_CAT, D_MODEL)),
                _const_spec((N_HEADS, 1)), cache_spec, cache_spec, win_spec, win_spec]
    args = [x, w["norm_g"], w["w_in"], w["w_out"], w["sinks"].reshape(N_HEADS, 1),
            cache_k, cache_v, win_k, win_v]
    if first:
        in_specs += [_const_spec((1, D_MODEL)), _const_spec((D_MODEL, 2 * D_KV))]
        args += [kv_w["kv_norm_g"], kv_w["w_kv"]]
    in_specs += [_const_spec((1, LANES)), _const_spec((1, LANES))]
    args += [rope_w["inv128"], rope_w["sign128"]]
    if final:
        in_specs.append(_const_spec((1, D_MODEL)))
        args.append(final_g)
    out_specs = [whole((batch, D_MODEL))]
    out_shape = [jax.ShapeDtypeStruct((batch, D_MODEL), F32)]
    if first:
        out_specs += [win_spec, win_spec]
        out_shape += [jax.ShapeDtypeStruct((batch, WINDOW, D_KV), F32)] * 2
    if final:
        out_specs.append(whole((batch, D_MODEL)))
        out_shape.append(jax.ShapeDtypeStruct((batch, D_MODEL), F32))
    return pl.pallas_call(
        functools.partial(_sample_swa_kernel, first, final, seqs),
        grid=(batch // seqs,),
        in_specs=in_specs,
        out_specs=out_specs,
        out_shape=out_shape,
        scratch_shapes=[pltpu.VMEM((batch, D_CAT), BF16), pltpu.VMEM((batch, D_MAIN), F32),
                        pltpu.VMEM((batch, D_MAIN), F32), pltpu.VMEM((batch, D_MAIN), F32),
                        pltpu.VMEM((batch, D_MEMW), F32), pltpu.VMEM((batch, D_MEMW), F32),
                        pltpu.VMEM((batch, D_MEMW), F32),
                        pltpu.VMEM((batch, D_KV), F32), pltpu.VMEM((batch, D_KV), F32)],
        compiler_params=pltpu.CompilerParams(
            dimension_semantics=("arbitrary",), vmem_limit_bytes=VMEM_LIMIT_BYTES),
        name="sample_swa" + ("_first" if first else "") + ("_final" if final else ""),
    )(*args)


def _sample_trunk(x, W, cache_mem_k, cache_mem_v, state_h, state_conv, win_k, win_v, seqs=8):
    batch = x.shape[0]
    x = x.reshape(batch, D_MODEL)
    ck = cache_mem_k.reshape(DEPTH, batch, N_MEM * MEM_HEADS, MEM_HD)
    cv = cache_mem_v.reshape(DEPTH, batch, N_MEM * MEM_HEADS, MEM_HD)
    hs, convs = [], []
    for l in range(N_A):
        c0 = state_conv[l].reshape(batch, (CONV_W - 1) * D_MAIN)
        x, h, c = _sample_lru_layer(x, W["layers"][l], state_h[l], c0, ck, cv, l, seqs=seqs)
        hs.append(h)
        convs.append(c.reshape(batch, CONV_W - 1, D_MAIN))
    wk = win_k.reshape(batch, WINDOW, D_KV)
    wv = win_v.reshape(batch, WINDOW, D_KV)
    kv_w = dict(kv_norm_g=W["kv_norm_g"], w_kv=W["w_kv"])
    y = None
    for l in range(N_A, DEPTH):
        outs = _sample_swa_layer(x, W["layers"][l], ck, cv, l, wk, wv, W,
                                 kv_w=kv_w if l == N_A else None,
                                 final_g=W["final_g"] if l == DEPTH - 1 else None, seqs=seqs)
        x = outs[0]
        if l == N_A:
            wk, wv = outs[1], outs[2]
        if l == DEPTH - 1:
            y = outs[-1]
    return (y.reshape(batch, 1, D_MODEL), jnp.stack(hs), jnp.stack(convs),
            wk.reshape(batch, WINDOW, N_KV, HEAD_DIM), wv.reshape(batch, WINDOW, N_KV, HEAD_DIM))


def _prompt_trunk(x, W, mkb, mvb, tile=512):
    batch, seq, _ = x.shape
    cos, sin = _rope_tables(seq, W["inv128"], W["sign128"])
    kv_w = dict(kv_norm_g=W["kv_norm_g"], w_kv=W["w_kv"], cos=cos, sin=sin)
    hs, convs = [], []
    k = v = None
    for l in range(N_A):
        outs = _prompt_lru_layer(x, W["layers"][l], mkb[l], mvb[l],
                                 kv_w=kv_w if l == N_A - 1 else None, tile=tile)
        x, h_last, conv_tail = outs[:3]
        if l == N_A - 1:
            k, v = outs[3:]
        hs.append(h_last.reshape(batch, D_MAIN))
        convs.append(conv_tail[:, SUBLANES - (CONV_W - 1):, :])
    y = None
    for l in range(N_A, DEPTH):
        last = l == DEPTH - 1
        outs = _prompt_swa_layer(x, W["layers"][l], mkb[l], mvb[l], k, v, cos, sin,
                                 final_g=W["final_g"] if last else None, tile=tile)
        x = outs[0]
        if last:
            y = outs[1]
    win_k = k[:, seq - WINDOW:, :].reshape(batch, WINDOW, N_KV, HEAD_DIM)
    win_v = v[:, seq - WINDOW:, :].reshape(batch, WINDOW, N_KV, HEAD_DIM)
    return y, jnp.stack(hs), jnp.stack(convs), win_k, win_v


def kernel(x_prompt, x_sample, cache_mem_k, cache_mem_v, state_lru_h, state_conv, cache_win_k, cache_win_v, mem_prompt, norm_g, w_in, w_out, mem_norm_g, w_mem_kv, conv_w, conv_b, lru_wa, lru_ba, lru_wx, lru_bx, lru_lambda, kv_norm_g, w_kv, sinks, final_g):
    W = _prepare_weights(norm_g, w_in, w_out, conv_w, conv_b, lru_wa, lru_ba, lru_wx, lru_bx, lru_lambda, kv_norm_g, w_kv, sinks, final_g)
    mk, mv, mkb, mvb = _mem_kv(mem_prompt, mem_norm_g, w_mem_kv.astype(BF16))
    y_p, h_p, conv_p, wk_p, wv_p = _prompt_trunk(x_prompt, W, mkb, mvb)
    y_s, h_s, conv_s, wk_s, wv_s = _sample_trunk(x_sample, W, cache_mem_k, cache_mem_v, state_lru_h,
                                                 state_conv, cache_win_k, cache_win_v)
    batch = x_prompt.shape[0]
    mk_p = mk.reshape(DEPTH, batch, N_MEM, MEM_HEADS, MEM_HD)
    mv_p = mv.reshape(DEPTH, batch, N_MEM, MEM_HEADS, MEM_HD)
    return (y_p, y_s, mk_p, mv_p, h_p, conv_p, wk_p, wv_p, h_s, conv_s, wk_s, wv_s)
```

```python
import functools

import jax
import jax.numpy as jnp
from jax import lax
from jax.experimental import pallas as pl
from jax.experimental.pallas import tpu as pltpu

F32 = jnp.float32
BF16 = jnp.bfloat16

D_MODEL = 1024
DEPTH = 4
N_A = DEPTH // 2
N_HEADS = 16
HEAD_DIM = 64
N_KV = 2
GROUP = N_HEADS // N_KV
D_MAIN = N_HEADS * HEAD_DIM
N_BLOCKS = 16
BLOCK_W = D_MAIN // N_BLOCKS
CONV_W = 4
LRU_C = 8.0
WINDOW = 128
ROPE_THETA = 10000.0
N_MEM = 256
MEM_HEADS = 4
MEM_HD = 128
D_MEMW = MEM_HEADS * MEM_HD
D_IN = 2 * D_MAIN + 2 * D_MEMW
D_CAT = D_MAIN + D_MEMW
D_KV = N_KV * HEAD_DIM
EPS = 1e-6
NEG = -1e30
LOG2_E = 1.4426950408889634
PAST_LEN = 8192

SUBLANES = 8
LANES = 128
N_SLABS = D_MAIN // LANES
GATE_TILE = 256
N_GATE_TILES = D_MAIN // GATE_TILE
VMEM_LIMIT_BYTES = 56 * 1024 * 1024
ATTN_SKEW = 4
MEM_CHUNK = 256
SCAN_CHUNKS = 16

_NT = (((1,), (1,)), ((), ()))


def _const_spec(shape):
    zeros = (0,) * len(shape)
    return pl.BlockSpec(shape, lambda *_: zeros, pipeline_mode=pl.Buffered(1))


def _layer_spec(shape, layer):
    zeros = (0,) * len(shape)
    return pl.BlockSpec((None,) + tuple(shape), lambda *_: (layer,) + zeros, pipeline_mode=pl.Buffered(1))


def _rms(x, g):
    return x * lax.rsqrt(jnp.mean(x * x, axis=-1, keepdims=True) + EPS) * g


def _silu_of_half(hx):
    return hx * jnp.tanh(hx) + hx


def _softplus(x):
    return jnp.maximum(x, 0.0) + jnp.log1p(jnp.exp(-jnp.abs(x)))


def _decay_exponent(lam):
    return (-0.5 * LRU_C * LOG2_E) * _softplus(-lam)


def _rope(x, cos, sin_signed):
    lane = lax.broadcasted_iota(jnp.int32, x.shape, 1)
    first_half = (lane % HEAD_DIM) < (HEAD_DIM // 2)
    swapped = jnp.where(first_half,
                        pltpu.roll(x, LANES - HEAD_DIM // 2, 1),
                        pltpu.roll(x, HEAD_DIM // 2, 1))
    return x * cos + swapped * sin_signed


def _rglru_coeffs(uc, half_pre, half_ba, half_bx, c):
    half = half_pre.shape[-1] // 2
    t_r = jnp.tanh(half_pre[:, :half] + half_ba)
    t_i = jnp.tanh(half_pre[:, half:] + half_bx)
    a = jnp.exp2(c * t_r + c)
    w = 1.0 - a * a
    root = jnp.where(w > 0.0, w * lax.rsqrt(w), 0.0)
    return a, root * ((0.5 * t_i + 0.5) * uc)


def _mem_units(proj, mk_ref, mv_ref, z, tile):
    chunk = min(tile, MEM_CHUNK)
    units = []
    for ch in range(tile // chunk):
        for hh in range(MEM_HEADS):
            rows = slice(chunk * ch, chunk * (ch + 1))
            cols = slice(MEM_HD * hh, MEM_HD * (hh + 1))
            q_cols = slice(2 * D_MAIN + MEM_HD * hh, 2 * D_MAIN + MEM_HD * (hh + 1))
            g_cols = slice(2 * D_MAIN + D_MEMW + MEM_HD * hh, 2 * D_MAIN + D_MEMW + MEM_HD * (hh + 1))
            z_cols = slice(D_MAIN + MEM_HD * hh, D_MAIN + MEM_HD * (hh + 1))

            def scores(rows=rows, cols=cols, q_cols=q_cols):
                q = proj[rows, q_cols].astype(BF16)
                s = lax.dot_general(mk_ref[0, :, cols], q, _NT, preferred_element_type=F32) * (MEM_HD ** -0.5)
                m = jnp.max(s, axis=0, keepdims=True)
                p = jnp.exp(s - m)
                return p.astype(BF16), jnp.sum(p, axis=0, keepdims=True)

            def outputs(p, den, rows=rows, cols=cols, g_cols=g_cols, z_cols=z_cols):
                o_t = jnp.dot(mv_ref[0, cols, :], p, preferred_element_type=F32)
                z[rows, z_cols] = ((o_t / den).T * _silu_of_half(proj[rows, g_cols])).astype(BF16)

            units.append((scores, outputs))
    return units


def _run_skewed(units, skew, fillers=()):
    calls = []
    pending = []
    for scores, outputs in units:
        calls.append((scores, None))
        pending.append(outputs)
        if len(pending) > skew:
            calls.append((None, pending.pop(0)))
    calls += [(None, out) for out in pending]
    fillers = list(fillers)
    per_call = -(-len(fillers) // max(len(calls), 1))
    results = []
    for scores, outputs in calls:
        if scores is not None:
            results.append(scores())
        else:
            outputs(*results.pop(0))
        for _ in range(per_call):
            if fillers:
                fillers.pop(0)()
    for f in fillers:
        f()


def _block_diag_pair(x, x_rolled, kv_head):
    lane = lax.broadcasted_iota(jnp.int32, x.shape, 1)
    lo = lane < HEAD_DIM
    if kv_head == 0:
        top = jnp.where(lo, x, 0.0)
        bot = jnp.where(lo, 0.0, x_rolled)
    else:
        top = jnp.where(lo, x_rolled, 0.0)
        bot = jnp.where(lo, 0.0, x)
    return jnp.concatenate([top, bot], axis=0).astype(BF16)


def _gate_tiles(wa, wx):
    per_tile = GATE_TILE // BLOCK_W
    eye = jnp.eye(per_tile, dtype=wa.dtype)

    def dense(w):
        w5 = w.reshape(N_A, N_GATE_TILES, per_tile, BLOCK_W, BLOCK_W)
        return jnp.einsum("ljicd,ik->ljickd", w5, eye).reshape(N_A, N_GATE_TILES, GATE_TILE, GATE_TILE)

    return jnp.concatenate([dense(wa), dense(wx)], axis=-1).astype(BF16)


def _prepare_weights(norm_g, w_in, w_out, conv_w, conv_b, lru_wa, lru_ba, lru_wx, lru_bx, lru_lambda,
                     kv_norm_g, w_kv, sinks, final_g):
    col = jnp.arange(D_IN)
    is_gate = ((col >= D_MAIN) & (col < 2 * D_MAIN)) | (col >= 2 * D_MAIN + D_MEMW)
    half = HEAD_DIM // 2
    inv = ROPE_THETA ** (-jnp.arange(half, dtype=F32) / half)
    sign = jnp.concatenate([-jnp.ones((half,), F32), jnp.ones((half,), F32)])
    return dict(
        norm_g=norm_g.reshape(DEPTH, 1, D_MODEL),
        w_in=(w_in * jnp.where(is_gate, 0.5, 1.0).astype(F32)).astype(BF16),
        w_out=w_out.astype(BF16),
        conv_w=conv_w, conv_b=conv_b.reshape(N_A, 1, D_MAIN),
        w_gate=_gate_tiles(0.5 * lru_wa, 0.5 * lru_wx),
        ba=0.5 * lru_ba.reshape(N_A, 1, D_MAIN), bx=0.5 * lru_bx.reshape(N_A, 1, D_MAIN),
        lam=lru_lambda.reshape(N_A, 1, D_MAIN),
        sinks=sinks, sinks_col=sinks.reshape(DEPTH - N_A, N_HEADS, 1),
        kv_norm_g=kv_norm_g.reshape(1, D_MODEL), w_kv=w_kv.astype(BF16),
        final_g=final_g.reshape(1, D_MODEL),
        inv128=jnp.tile(inv, LANES // half).reshape(1, LANES),
        sign128=jnp.tile(sign, LANES // HEAD_DIM).reshape(1, LANES))


def _rope_table_kernel(inv_ref, sign_ref, cos_ref, sin_ref):
    rows = cos_ref.shape[0]
    pos = (pl.program_id(0) * rows + lax.broadcasted_iota(jnp.int32, (rows, LANES), 0)).astype(F32)
    ang = pos * inv_ref[...]
    cos_ref[...] = jnp.cos(ang)
    sin_ref[...] = jnp.sin(ang) * sign_ref[...]


def _rope_tables(seq, inv128, sign128, rows=1024):
    rows = min(rows, seq)
    return pl.pallas_call(
        _rope_table_kernel,
        grid=(seq // rows,),
        in_specs=[_const_spec((1, LANES)), _const_spec((1, LANES))],
        out_specs=[pl.BlockSpec((rows, LANES), lambda i: (i, 0)),
                   pl.BlockSpec((rows, LANES), lambda i: (i, 0))],
        out_shape=[jax.ShapeDtypeStruct((seq, LANES), F32)] * 2,
        name="rope_tables",
    )(inv128, sign128)


def _mem_kv_kernel(m_ref, g_ref, w_ref, k_ref, v_ref, kb_ref, vb_ref):
    mn = _rms(m_ref[0], g_ref[0]).astype(BF16)
    kv = jnp.dot(mn, w_ref[0], preferred_element_type=F32)
    k, v = kv[:, :D_MEMW], kv[:, D_MEMW:]
    k_ref[0, 0] = k
    v_ref[0, 0] = v
    kb_ref[0, 0] = k.astype(BF16)
    vb_ref[0, 0] = v.T.astype(BF16)


def _mem_kv(mem_prompt, mem_norm_g, w_mem_kv_bf16):
    batch = mem_prompt.shape[0]
    out_spec = pl.BlockSpec((1, 1, N_MEM, D_MEMW), lambda l, b: (l, b, 0, 0))
    out_spec_t = pl.BlockSpec((1, 1, D_MEMW, N_MEM), lambda l, b: (l, b, 0, 0))
    return pl.pallas_call(
        _mem_kv_kernel,
        grid=(DEPTH, batch),
        in_specs=[pl.BlockSpec((1, N_MEM, D_MODEL), lambda l, b: (b, 0, 0)),
                  pl.BlockSpec((1, 1, D_MODEL), lambda l, b: (l, 0, 0)),
                  pl.BlockSpec((1, D_MODEL, 2 * D_MEMW), lambda l, b: (l, 0, 0))],
        out_specs=[out_spec, out_spec, out_spec, out_spec_t],
        out_shape=[jax.ShapeDtypeStruct((DEPTH, batch, N_MEM, D_MEMW), F32)] * 2
                  + [jax.ShapeDtypeStruct((DEPTH, batch, N_MEM, D_MEMW), BF16),
                     jax.ShapeDtypeStruct((DEPTH, batch, D_MEMW, N_MEM), BF16)],
        name="mem_kv",
    )(mem_prompt, mem_norm_g.reshape(DEPTH, 1, D_MODEL), w_mem_kv_bf16)


def _prompt_lru_kernel(emit_kv, tile, *refs):
    n_in = 16 if emit_kv else 12
    (x_ref, ng_ref, win_ref, wout_ref, cw_ref, cb_ref, wg_ref, ba_ref, bx_ref, lam_ref,
     mk_ref, mv_ref) = refs[:12]
    if emit_kv:
        kvg_ref, wkv_ref, cos_ref, sin_ref = refs[12:16]
        xo_ref, hl_ref, ct_ref, k_ref, v_ref = refs[n_in:n_in + 5]
        scratch = refs[n_in + 5:]
    else:
        xo_ref, hl_ref, ct_ref = refs[n_in:n_in + 3]
        scratch = refs[n_in + 3:]
    ubuf, a3, b3, h3, hc, z = scratch
    groups = tile // SUBLANES
    t = pl.program_id(1)

    @pl.when(t == 0)
    def _():
        ubuf[0:SUBLANES, :] = jnp.zeros((SUBLANES, D_MAIN), F32)
        hc[...] = jnp.zeros((SUBLANES, LANES), F32)

    x = x_ref[0]
    hn = _rms(x, ng_ref[...]).astype(BF16)
    proj = jnp.dot(hn, win_ref[...], preferred_element_type=F32)

    u = proj[:, :D_MAIN]
    ubuf[SUBLANES:SUBLANES + tile, :] = u
    uc = cb_ref[...] + ubuf[SUBLANES - 3:SUBLANES - 3 + tile, :] * cw_ref[0:1, :]
    uc = uc + ubuf[SUBLANES - 2:SUBLANES - 2 + tile, :] * cw_ref[1:2, :]
    uc = uc + ubuf[SUBLANES - 1:SUBLANES - 1 + tile, :] * cw_ref[2:3, :]
    uc = uc + u * cw_ref[3:4, :]
    ubuf[0:SUBLANES, :] = ubuf[tile:tile + SUBLANES, :]

    decay_c = _decay_exponent(lam_ref[...])
    ucb = uc.astype(BF16)
    for j in range(N_GATE_TILES):
        sl = slice(GATE_TILE * j, GATE_TILE * (j + 1))
        pre = jnp.dot(ucb[:, sl], wg_ref[j], preferred_element_type=F32)
        a, b = _rglru_coeffs(uc[:, sl], pre, ba_ref[:, sl], bx_ref[:, sl], decay_c[:, sl])
        for jj in range(GATE_TILE // LANES):
            slab = j * (GATE_TILE // LANES) + jj
            rows = slice(SUBLANES * slab, SUBLANES * (slab + 1))
            cols = slice(LANES * jj, LANES * (jj + 1))
            a3[:, rows, :] = a[:, cols].reshape(groups, SUBLANES, LANES)
            b3[:, rows, :] = b[:, cols].reshape(groups, SUBLANES, LANES)

    carry = [hc[...]]

    def scan_groups(first, count):
        def run():
            h = carry[0]
            for g in range(first, first + count):
                for r in range(SUBLANES):
                    step = pl.ds(r, N_SLABS, stride=SUBLANES)
                    h = a3[g, step, :] * h + b3[g, step, :]
                    h3[g, step, :] = h
            carry[0] = h
        return run

    scan_chunk = max(groups // SCAN_CHUNKS, 1)
    _run_skewed(_mem_units(proj, mk_ref, mv_ref, z, tile), ATTN_SKEW,
                fillers=[scan_groups(g0, scan_chunk) for g0 in range(0, groups, scan_chunk)])
    h = carry[0]
    hc[...] = h

    for j in range(N_SLABS):
        cols = slice(LANES * j, LANES * (j + 1))
        y = h3[:, SUBLANES * j:SUBLANES * (j + 1), :].reshape(tile, LANES)
        g_main = proj[:, D_MAIN + LANES * j:D_MAIN + LANES * (j + 1)]
        z[:, cols] = (y * _silu_of_half(g_main)).astype(BF16)

    xn = x + jnp.dot(z[...], wout_ref[...], preferred_element_type=F32)
    xo_ref[0] = xn

    @pl.when(t == pl.num_programs(1) - 1)
    def _():
        hl_ref[0] = h
        ct_ref[0] = ubuf[0:SUBLANES, :]

    if emit_kv:
        kvn = _rms(xn, kvg_ref[...]).astype(BF16)
        kv = jnp.dot(kvn, wkv_ref[...], preferred_element_type=F32)
        k_ref[0] = _rope(kv[:, :D_KV], cos_ref[...], sin_ref[...])
        v_ref[0] = kv[:, D_KV:]


def _prompt_lru_layer(x, W, layer, mk, mv_t, kv_w=None, tile=256):
    batch, seq, _ = x.shape
    emit_kv = kv_w is not None
    tok = lambda b, t: (b, t, 0)
    per_b = lambda b, t: (b, 0, 0)
    mem = lambda b, t: (layer, b, 0, 0)
    in_specs = [
        pl.BlockSpec((1, tile, D_MODEL), tok),
        _layer_spec((1, D_MODEL), layer), _layer_spec((D_MODEL, D_IN), layer),
        _layer_spec((D_CAT, D_MODEL), layer),
        _layer_spec((CONV_W, D_MAIN), layer), _layer_spec((1, D_MAIN), layer),
        _layer_spec((N_GATE_TILES, GATE_TILE, 2 * GATE_TILE), layer),
        _layer_spec((1, D_MAIN), layer), _layer_spec((1, D_MAIN), layer), _layer_spec((1, D_MAIN), layer),
        pl.BlockSpec((None, 1, N_MEM, D_MEMW), mem), pl.BlockSpec((None, 1, D_MEMW, N_MEM), mem),
    ]
    args = [x, W["norm_g"], W["w_in"], W["w_out"], W["conv_w"], W["conv_b"], W["w_gate"],
            W["ba"], W["bx"], W["lam"], mk, mv_t]
    out_specs = [pl.BlockSpec((1, tile, D_MODEL), tok),
                 pl.BlockSpec((1, SUBLANES, LANES), per_b),
                 pl.BlockSpec((1, SUBLANES, D_MAIN), per_b)]
    out_shape = [jax.ShapeDtypeStruct((batch, seq, D_MODEL), F32),
                 jax.ShapeDtypeStruct((batch, SUBLANES, LANES), F32),
                 jax.ShapeDtypeStruct((batch, SUBLANES, D_MAIN), F32)]
    if emit_kv:
        in_specs += [_const_spec((1, D_MODEL)), _const_spec((D_MODEL, 2 * D_KV)),
                     pl.BlockSpec((tile, LANES), lambda b, t: (t, 0)),
                     pl.BlockSpec((tile, LANES), lambda b, t: (t, 0))]
        args += [kv_w["kv_norm_g"], kv_w["w_kv"], kv_w["cos"], kv_w["sin"]]
        out_specs += [pl.BlockSpec((1, tile, D_KV), tok)] * 2
        out_shape += [jax.ShapeDtypeStruct((batch, seq, D_KV), F32)] * 2
    groups = tile // SUBLANES
    return pl.pallas_call(
        functools.partial(_prompt_lru_kernel, emit_kv, tile),
        grid=(batch, seq // tile),
        in_specs=in_specs,
        out_specs=out_specs,
        out_shape=out_shape,
        scratch_shapes=[pltpu.VMEM((tile + SUBLANES, D_MAIN), F32),
                        pltpu.VMEM((groups, SUBLANES * N_SLABS, LANES), F32),
                        pltpu.VMEM((groups, SUBLANES * N_SLABS, LANES), F32),
                        pltpu.VMEM((groups, SUBLANES * N_SLABS, LANES), F32),
                        pltpu.VMEM((SUBLANES, LANES), F32),
                        pltpu.VMEM((tile, D_CAT), BF16)],
        compiler_params=pltpu.CompilerParams(
            dimension_semantics=("arbitrary", "arbitrary"), vmem_limit_bytes=VMEM_LIMIT_BYTES),
        name="prompt_lru_kv" if emit_kv else "prompt_lru",
    )(*args)


def _sink_softmax(s, sink, axis=-1):
    m = jnp.maximum(jnp.max(s, axis=axis, keepdims=True), sink)
    p = jnp.exp(s - m)
    den = jnp.sum(p, axis=axis, keepdims=True) + jnp.exp(sink - m)
    return p, den


def _prompt_swa_kernel(final, tile, sink_row, *refs):
    n_in = 14 if final else 13
    (x_ref, ng_ref, win_ref, wout_ref, mk_ref, mv_ref, sink_ref, kp_ref, kc_ref, vp_ref, vc_ref,
     cos_ref, sin_ref) = refs[:13]
    if final:
        fg_ref = refs[13]
        xo_ref, y_ref = refs[n_in:n_in + 2]
        scratch = refs[n_in + 2:]
    else:
        xo_ref, = refs[n_in:n_in + 1]
        scratch = refs[n_in + 1:]
    z, qz, kcat, vcat_t = scratch
    blocks = tile // WINDOW
    t = pl.program_id(1)

    x = x_ref[0]
    hn = _rms(x, ng_ref[...]).astype(BF16)
    proj = jnp.dot(hn, win_ref[...], preferred_element_type=F32)

    cos = cos_ref[...]
    sin = sin_ref[...]
    lane = lax.broadcasted_iota(jnp.int32, (tile, LANES), 1)
    lo = lane < HEAD_DIM
    first_half = (lane % HEAD_DIM) < (HEAD_DIM // 2)
    for j in range(N_SLABS):
        c = j // (GROUP // 2)
        xq = proj[:, LANES * j:LANES * (j + 1)] * (HEAD_DIM ** -0.5)
        r32 = pltpu.roll(xq, HEAD_DIM // 2, 1)
        r96 = pltpu.roll(xq, LANES - HEAD_DIM // 2, 1)
        r64 = pltpu.roll(xq, HEAD_DIM, 1)
        stay = xq * cos + jnp.where(first_half, r96, r32) * sin
        move = r64 * cos + jnp.where(first_half, r32, r96) * sin
        keep = lo if c == 0 else ~lo
        even, odd = (stay, move) if c == 0 else (move, stay)
        qz[:, 2 * LANES * j:2 * LANES * j + LANES] = jnp.where(keep, even, 0.0).astype(BF16)
        qz[:, 2 * LANES * j + LANES:2 * LANES * (j + 1)] = jnp.where(keep, odd, 0.0).astype(BF16)

    kcat[0:WINDOW, :] = kp_ref[0].astype(BF16)
    kcat[WINDOW:, :] = kc_ref[0].astype(BF16)
    vcat_t[:, 0:WINDOW] = vp_ref[0].T.astype(BF16)
    vcat_t[:, WINDOW:] = vc_ref[0].T.astype(BF16)

    ki = lax.broadcasted_iota(jnp.int32, (2 * WINDOW, WINDOW), 0)
    qi = lax.broadcasted_iota(jnp.int32, (2 * WINDOW, WINDOW), 1)
    band = (ki > qi) & (ki <= qi + WINDOW)

    mask_first = jnp.concatenate([band & ((ki >= WINDOW) | (t > 0))] * 2, axis=1)
    mask_rest = jnp.concatenate([band] * 2, axis=1)

    def scores(n, j):
        rows = slice(WINDOW * n, WINDOW * (n + 1))
        keys = kcat[WINDOW * n:WINDOW * (n + 2), :]
        q2 = jnp.concatenate([qz[rows, 2 * LANES * j:2 * LANES * j + LANES],
                              qz[rows, 2 * LANES * j + LANES:2 * LANES * (j + 1)]], axis=0)
        s = lax.dot_general(keys, q2, _NT, preferred_element_type=F32)
        s = jnp.where(mask_first if n == 0 else mask_rest, s, NEG)
        p0, den0 = _sink_softmax(s[:, :WINDOW], sink_ref[sink_row, 2 * j], axis=0)
        p1, den1 = _sink_softmax(s[:, WINDOW:], sink_ref[sink_row, 2 * j + 1], axis=0)
        return jnp.concatenate([p0, p1], axis=1).astype(BF16), den0, den1

    def outputs(n, j, p, den0, den1):
        rows = slice(WINDOW * n, WINDOW * (n + 1))
        vals_t = vcat_t[:, WINDOW * n:WINDOW * (n + 2)]
        o_t = jnp.dot(vals_t, p, preferred_element_type=F32)
        c = j // (GROUP // 2)
        own = slice(HEAD_DIM * c, HEAD_DIM * (c + 1))
        y_t = jnp.concatenate([o_t[own, :WINDOW] / den0, o_t[own, WINDOW:] / den1], axis=0)
        g_main = proj[rows, D_MAIN + LANES * j:D_MAIN + LANES * (j + 1)]
        z[rows, LANES * j:LANES * (j + 1)] = (y_t.T * _silu_of_half(g_main)).astype(BF16)

    units = [(functools.partial(scores, n, j), functools.partial(outputs, n, j))
             for n in range(blocks) for j in range(N_SLABS)]
    _run_skewed(units + _mem_units(proj, mk_ref, mv_ref, z, tile), ATTN_SKEW)

    xn = x + jnp.dot(z[...], wout_ref[...], preferred_element_type=F32)
    xo_ref[0] = xn
    if final:
        y_ref[0] = _rms(xn, fg_ref[...])


def _prompt_swa_layer(x, W, layer, mk, mv_t, k, v, cos, sin, final_g=None, tile=256):
    batch, seq, _ = x.shape
    final = final_g is not None
    blocks = tile // WINDOW
    tok = lambda b, t: (b, t, 0)
    mem = lambda b, t: (layer, b, 0, 0)
    prev_blk = lambda b, t: (b, jnp.maximum(t * blocks - 1, 0), 0)
    in_specs = [
        pl.BlockSpec((1, tile, D_MODEL), tok),
        _layer_spec((1, D_MODEL), layer), _layer_spec((D_MODEL, D_IN), layer),
        _layer_spec((D_CAT, D_MODEL), layer),
        pl.BlockSpec((None, 1, N_MEM, D_MEMW), mem), pl.BlockSpec((None, 1, D_MEMW, N_MEM), mem),
        pl.BlockSpec(memory_space=pltpu.SMEM),
        pl.BlockSpec((1, WINDOW, D_KV), prev_blk), pl.BlockSpec((1, tile, D_KV), tok),
        pl.BlockSpec((1, WINDOW, D_KV), prev_blk), pl.BlockSpec((1, tile, D_KV), tok),
        pl.BlockSpec((tile, LANES), lambda b, t: (t, 0)), pl.BlockSpec((tile, LANES), lambda b, t: (t, 0)),
    ]
    args = [x, W["norm_g"], W["w_in"], W["w_out"], mk, mv_t, W["sinks"], k, k, v, v, cos, sin]
    out_specs = [pl.BlockSpec((1, tile, D_MODEL), tok)]
    out_shape = [jax.ShapeDtypeStruct((batch, seq, D_MODEL), F32)]
    if final:
        in_specs.append(_const_spec((1, D_MODEL)))
        args.append(final_g)
        out_specs.append(pl.BlockSpec((1, tile, D_MODEL), tok))
        out_shape.append(jax.ShapeDtypeStruct((batch, seq, D_MODEL), F32))
    return pl.pallas_call(
        functools.partial(_prompt_swa_kernel, final, tile, layer - N_A),
        grid=(batch, seq // tile),
        in_specs=in_specs,
        out_specs=out_specs,
        out_shape=out_shape,
        scratch_shapes=[pltpu.VMEM((tile, D_CAT), BF16),
                        pltpu.VMEM((tile, 2 * D_MAIN), BF16),
                        pltpu.VMEM((WINDOW + tile, D_KV), BF16),
                        pltpu.VMEM((D_KV, WINDOW + tile), BF16)],
        compiler_params=pltpu.CompilerParams(
            dimension_semantics=("arbitrary", "arbitrary"), vmem_limit_bytes=VMEM_LIMIT_BYTES),
        name="prompt_swa_final" if final else "prompt_swa",
    )(*args)


def _sample_mem_unit(b, bb, qm_s, mk_ref, mv_ref, om_s):
    def scores():
        row = lax.broadcasted_iota(jnp.int32, (SUBLANES, MEM_HD), 0)
        q = qm_s[pl.ds(b, 1), :]
        qh = jnp.zeros((SUBLANES, MEM_HD), F32)
        for hh in range(MEM_HEADS):
            q_head = jnp.broadcast_to(q[:, MEM_HD * hh:MEM_HD * (hh + 1)], (SUBLANES, MEM_HD))
            qh = jnp.where(row == hh, q_head, qh)
        s = lax.dot_general(qh.astype(BF16), mk_ref[bb].astype(BF16), _NT,
                            preferred_element_type=F32) * (MEM_HD ** -0.5)
        srow = lax.broadcasted_iota(jnp.int32, s.shape, 0)
        scol = lax.broadcasted_iota(jnp.int32, s.shape, 1)
        s = jnp.where((scol % MEM_HEADS) == srow, s, NEG)
        m = jnp.max(s, axis=-1, keepdims=True)
        p = jnp.exp(s - m)
        return p.astype(BF16), jnp.sum(p, axis=-1, keepdims=True)

    def outputs(p, den):
        o = jnp.dot(p, mv_ref[bb].astype(BF16), preferred_element_type=F32) / den
        om_s[pl.ds(b, 1), :] = jnp.concatenate([o[hh:hh + 1, :] for hh in range(MEM_HEADS)], axis=1)

    return scores, outputs


def _sample_lru_kernel(seqs, x_ref, ng_ref, win_ref, wout_ref, cw_ref, cb_ref, wg_ref, ba_ref, bx_ref,
                       lam_ref, h0_ref, c0_ref, mk_ref, mv_ref, xo_ref, ho_ref, co_ref,
                       z_s, qm_s, gm_s, om_s):
    i = pl.program_id(0)

    @pl.when(i == 0)
    def _():
        hn = _rms(x_ref[...], ng_ref[...]).astype(BF16)
        proj = jnp.dot(hn, win_ref[...], preferred_element_type=F32)
        u = proj[:, :D_MAIN]
        taps = [c0_ref[k] for k in range(CONV_W - 1)] + [u]
        uc = cb_ref[...] + taps[0] * cw_ref[0:1, :]
        for k in range(1, CONV_W):
            uc = uc + taps[k] * cw_ref[k:k + 1, :]
        for k in range(CONV_W - 1):
            co_ref[k] = taps[k + 1]
        decay_c = _decay_exponent(lam_ref[...])
        ucb = uc.astype(BF16)
        for j in range(N_GATE_TILES):
            sl = slice(GATE_TILE * j, GATE_TILE * (j + 1))
            pre = jnp.dot(ucb[:, sl], wg_ref[j], preferred_element_type=F32)
            a, b = _rglru_coeffs(uc[:, sl], pre, ba_ref[:, sl], bx_ref[:, sl], decay_c[:, sl])
            h = a * h0_ref[:, sl] + b
            ho_ref[:, sl] = h
            z_s[:, sl] = (h * _silu_of_half(proj[:, D_MAIN + GATE_TILE * j:D_MAIN + GATE_TILE * (j + 1)])).astype(BF16)
        qm_s[...] = proj[:, 2 * D_MAIN:2 * D_MAIN + D_MEMW]
        gm_s[...] = _silu_of_half(proj[:, 2 * D_MAIN + D_MEMW:])

    _run_skewed([_sample_mem_unit(i * seqs + bb, bb, qm_s, mk_ref, mv_ref, om_s) for bb in range(seqs)],
                ATTN_SKEW)

    @pl.when(i == pl.num_programs(0) - 1)
    def _():
        z_s[:, D_MAIN:] = (om_s[...] * gm_s[...]).astype(BF16)
        xo_ref[...] = x_ref[...] + jnp.dot(z_s[...], wout_ref[...], preferred_element_type=F32)


def _cache_spec(layer, seqs):
    return pl.BlockSpec((None, seqs, N_MEM * MEM_HEADS, MEM_HD), lambda i: (layer, i, 0, 0))


def _sample_lru_layer(x, W, layer, state_h, state_conv_t, cache_k, cache_v, seqs=8):
    batch = x.shape[0]
    cache_spec = _cache_spec(layer, seqs)
    return pl.pallas_call(
        functools.partial(_sample_lru_kernel, seqs),
        grid=(batch // seqs,),
        in_specs=[_const_spec((batch, D_MODEL)), _layer_spec((1, D_MODEL), layer),
                  _layer_spec((D_MODEL, D_IN), layer), _layer_spec((D_CAT, D_MODEL), layer),
                  _layer_spec((CONV_W, D_MAIN), layer), _layer_spec((1, D_MAIN), layer),
                  _layer_spec((N_GATE_TILES, GATE_TILE, 2 * GATE_TILE), layer),
                  _layer_spec((1, D_MAIN), layer), _layer_spec((1, D_MAIN), layer),
                  _layer_spec((1, D_MAIN), layer),
                  _layer_spec((batch, D_MAIN), layer), _layer_spec((CONV_W - 1, batch, D_MAIN), layer),
                  cache_spec, cache_spec],
        out_specs=[pl.BlockSpec((batch, D_MODEL), lambda i: (0, 0)),
                   pl.BlockSpec((batch, D_MAIN), lambda i: (0, 0)),
                   pl.BlockSpec((CONV_W - 1, batch, D_MAIN), lambda i: (0, 0, 0))],
        out_shape=[jax.ShapeDtypeStruct((batch, D_MODEL), F32),
                   jax.ShapeDtypeStruct((batch, D_MAIN), F32),
                   jax.ShapeDtypeStruct((CONV_W - 1, batch, D_MAIN), F32)],
        scratch_shapes=[pltpu.VMEM((batch, D_CAT), BF16), pltpu.VMEM((batch, D_MEMW), F32),
                        pltpu.VMEM((batch, D_MEMW), F32), pltpu.VMEM((batch, D_MEMW), F32)],
        compiler_params=pltpu.CompilerParams(
            dimension_semantics=("arbitrary",), vmem_limit_bytes=VMEM_LIMIT_BYTES),
        name="sample_lru",
    )(x, W["norm_g"], W["w_in"], W["w_out"], W["conv_w"], W["conv_b"], W["w_gate"],
      W["ba"], W["bx"], W["lam"], state_h, state_conv_t, cache_k, cache_v)


def _sample_swa_kernel(first, final, seqs, *refs):
    (x_ref, ng_ref, win_ref, wout_ref, sink_ref, mk_ref, mv_ref, wk_ref, wv_ref) = refs[:9]
    pos = 9
    if first:
        kvg_ref, wkv_ref, inv_ref, sign_ref = refs[pos:pos + 4]
        pos += 4
    else:
        inv_ref, sign_ref = refs[pos:pos + 2]
        pos += 2
    if final:
        fg_ref = refs[pos]
        pos += 1
    xo_ref = refs[pos]
    pos += 1
    if first:
        wko_ref, wvo_ref = refs[pos:pos + 2]
        pos += 2
    if final:
        y_ref = refs[pos]
        pos += 1
    z_s, q_s, gmain_s, y_s, qm_s, gm_s, om_s, kn_s, vn_s = refs[pos:]
    i = pl.program_id(0)

    @pl.when(i == 0)
    def _():
        x = x_ref[...]
        hn = _rms(x, ng_ref[...]).astype(BF16)
        proj = jnp.dot(hn, win_ref[...], preferred_element_type=F32)
        ang = float(PAST_LEN) * inv_ref[...]
        cos = jnp.cos(ang)
        sin = jnp.sin(ang) * sign_ref[...]
        for j in range(N_SLABS):
            cols = slice(LANES * j, LANES * (j + 1))
            q_s[:, cols] = _rope(proj[:, cols], cos, sin)
        gmain_s[...] = _silu_of_half(proj[:, D_MAIN:2 * D_MAIN])
        qm_s[...] = proj[:, 2 * D_MAIN:2 * D_MAIN + D_MEMW]
        gm_s[...] = _silu_of_half(proj[:, 2 * D_MAIN + D_MEMW:])
        if first:
            kvn = _rms(x, kvg_ref[...]).astype(BF16)
            kv = jnp.dot(kvn, wkv_ref[...], preferred_element_type=F32)
            kn_s[...] = _rope(kv[:, :D_KV], cos, sin).T
            vn_s[...] = kv[:, D_KV:].T

    row = lax.broadcasted_iota(jnp.int32, (N_HEADS, LANES), 0)
    lane = lax.broadcasted_iota(jnp.int32, (N_HEADS, LANES), 1)
    kv_half = (lane // HEAD_DIM) == (row // GROUP)
    in_place = (row % 2) == (row // GROUP)
    lane1 = lax.broadcasted_iota(jnp.int32, (1, LANES), 1)
    seq_lane = lax.broadcasted_iota(jnp.int32, kn_s.shape, 1)
    slot = lax.broadcasted_iota(jnp.int32, (D_KV, WINDOW), 1)

    def slide(win_t, new_t_s, b):
        col = jnp.sum(jnp.where(seq_lane == b, new_t_s[...], 0.0), axis=1, keepdims=True)
        return jnp.where(slot == WINDOW - 1, col, pltpu.roll(win_t, WINDOW - 1, 1))

    def window_unit(b, bb):
        def scores():
            if first:
                kw_t = slide(wk_ref[bb], kn_s, b)
                wko_ref[bb] = kw_t
            else:
                kw_t = wk_ref[bb]
            q = jnp.broadcast_to(q_s[pl.ds(b, 1), :], (N_HEADS, D_MAIN))
            e = jnp.zeros((N_HEADS, LANES), F32)
            for j in range(N_SLABS):
                e = jnp.where((row // 2) == j, q[:, LANES * j:LANES * (j + 1)], e)
            qh = jnp.where(kv_half, jnp.where(in_place, e, pltpu.roll(e, HEAD_DIM, 1)), 0.0).astype(BF16)
            s = jnp.dot(qh, kw_t.astype(BF16), preferred_element_type=F32) * (HEAD_DIM ** -0.5)
            p, den = _sink_softmax(s, sink_ref[...])
            return p.astype(BF16), den

        def outputs(p, den):
            if first:
                vw_t = slide(wv_ref[bb], vn_s, b)
                wvo_ref[bb] = vw_t
            else:
                vw_t = wv_ref[bb]
            o = lax.dot_general(p, vw_t.astype(BF16), _NT, preferred_element_type=F32) / den
            f = jnp.where(in_place, o, pltpu.roll(o, HEAD_DIM, 1))
            y_s[pl.ds(b, 1), :] = jnp.concatenate(
                [jnp.where(lane1 < HEAD_DIM, f[2 * j:2 * j + 1, :], f[2 * j + 1:2 * j + 2, :])
                 for j in range(N_SLABS)], axis=1)

        return scores, outputs

    units = []
    for bb in range(seqs):
        units.append(window_unit(i * seqs + bb, bb))
        units.append(_sample_mem_unit(i * seqs + bb, bb, qm_s, mk_ref, mv_ref, om_s))
    _run_skewed(units, ATTN_SKEW)

    @pl.when(i == pl.num_programs(0) - 1)
    def _():
        z_s[:, :D_MAIN] = (y_s[...] * gmain_s[...]).astype(BF16)
        z_s[:, D_MAIN:] = (om_s[...] * gm_s[...]).astype(BF16)
        xn = x_ref[...] + jnp.dot(z_s[...], wout_ref[...], preferred_element_type=F32)
        xo_ref[...] = xn
        if final:
            y_ref[...] = _rms(xn, fg_ref[...])


def _sample_swa_layer(x, W, layer, cache_k, cache_v, win_k_t, win_v_t, first, final, seqs=8):
    batch = x.shape[0]
    cache_spec = _cache_spec(layer, seqs)
    win_spec = pl.BlockSpec((seqs, D_KV, WINDOW), lambda i: (i, 0, 0))
    whole = lambda shape: pl.BlockSpec(shape, lambda i: (0,) * len(shape))
    in_specs = [_const_spec((batch, D_MODEL)), _layer_spec((1, D_MODEL), layer),
                _layer_spec((D_MODEL, D_IN), layer), _layer_spec((D_CAT, D_MODEL), layer),
                _layer_spec((N_HEADS, 1), layer - N_A), cache_spec, cache_spec, win_spec, win_spec]
    args = [x, W["norm_g"], W["w_in"], W["w_out"], W["sinks_col"], cache_k, cache_v, win_k_t, win_v_t]
    if first:
        in_specs += [_const_spec((1, D_MODEL)), _const_spec((D_MODEL, 2 * D_KV))]
        args += [W["kv_norm_g"], W["w_kv"]]
    in_specs += [_const_spec((1, LANES)), _const_spec((1, LANES))]
    args += [W["inv128"], W["sign128"]]
    if final:
        in_specs.append(_const_spec((1, D_MODEL)))
        args.append(W["final_g"])
    out_specs = [whole((batch, D_MODEL))]
    out_shape = [jax.ShapeDtypeStruct((batch, D_MODEL), F32)]
    if first:
        out_specs += [win_spec, win_spec]
        out_shape += [jax.ShapeDtypeStruct((batch, D_KV, WINDOW), F32)] * 2
    if final:
        out_specs.append(whole((batch, D_MODEL)))
        out_shape.append(jax.ShapeDtypeStruct((batch, D_MODEL), F32))
    return pl.pallas_call(
        functools.partial(_sample_swa_kernel, first, final, seqs),
        grid=(batch // seqs,),
        in_specs=in_specs,
        out_specs=out_specs,
        out_shape=out_shape,
        scratch_shapes=[pltpu.VMEM((batch, D_CAT), BF16), pltpu.VMEM((batch, D_MAIN), F32),
                        pltpu.VMEM((batch, D_MAIN), F32), pltpu.VMEM((batch, D_MAIN), F32),
                        pltpu.VMEM((batch, D_MEMW), F32), pltpu.VMEM((batch, D_MEMW), F32),
                        pltpu.VMEM((batch, D_MEMW), F32),
                        pltpu.VMEM((D_KV, batch), F32), pltpu.VMEM((D_KV, batch), F32)],
        compiler_params=pltpu.CompilerParams(
            dimension_semantics=("arbitrary",), vmem_limit_bytes=VMEM_LIMIT_BYTES),
        name="sample_swa" + ("_first" if first else "") + ("_final" if final else ""),
    )(*args)


def _sample_trunk(x, W, cache_mem_k, cache_mem_v, state_h, state_conv, win_k, win_v, seqs=8):
    batch = x.shape[0]
    x = x.reshape(batch, D_MODEL)
    ck = cache_mem_k.reshape(DEPTH, batch, N_MEM * MEM_HEADS, MEM_HD)
    cv = cache_mem_v.reshape(DEPTH, batch, N_MEM * MEM_HEADS, MEM_HD)
    conv_t = jnp.transpose(state_conv, (0, 2, 1, 3))
    wk = jnp.transpose(win_k, (0, 2, 3, 1)).reshape(batch, D_KV, WINDOW)
    wv = jnp.transpose(win_v, (0, 2, 3, 1)).reshape(batch, D_KV, WINDOW)
    hs, convs = [], []
    for l in range(N_A):
        x, h, c = _sample_lru_layer(x, W, l, state_h, conv_t, ck, cv, seqs=seqs)
        hs.append(h)
        convs.append(c)
    y = None
    for l in range(N_A, DEPTH):
        outs = _sample_swa_layer(x, W, l, ck, cv, wk, wv, first=l == N_A, final=l == DEPTH - 1, seqs=seqs)
        x = outs[0]
        if l == N_A:
            wk, wv = outs[1], outs[2]
        if l == DEPTH - 1:
            y = outs[-1]
    unslide = lambda w_t: jnp.transpose(w_t.reshape(batch, N_KV, HEAD_DIM, WINDOW), (0, 3, 1, 2))
    return (y.reshape(batch, 1, D_MODEL), jnp.stack(hs), jnp.transpose(jnp.stack(convs), (0, 2, 1, 3)),
            unslide(wk), unslide(wv))


def _prompt_trunk(x, W, mkb, mvb_t, tile=512):
    batch, seq, _ = x.shape
    cos, sin = _rope_tables(seq, W["inv128"], W["sign128"])
    kv_w = dict(kv_norm_g=W["kv_norm_g"], w_kv=W["w_kv"], cos=cos, sin=sin)
    hs, convs = [], []
    k = v = None
    for l in range(N_A):
        outs = _prompt_lru_layer(x, W, l, mkb, mvb_t, kv_w=kv_w if l == N_A - 1 else None, tile=tile)
        x, h_last, conv_tail = outs[:3]
        if l == N_A - 1:
            k, v = outs[3:]
        hs.append(h_last.reshape(batch, D_MAIN))
        convs.append(conv_tail[:, SUBLANES - (CONV_W - 1):, :])
    y = None
    for l in range(N_A, DEPTH):
        last = l == DEPTH - 1
        outs = _prompt_swa_layer(x, W, l, mkb, mvb_t, k, v, cos, sin,
                                 final_g=W["final_g"] if last else None, tile=tile)
        x = outs[0]
        if last:
            y = outs[1]
    win_k = k[:, seq - WINDOW:, :].reshape(batch, WINDOW, N_KV, HEAD_DIM)
    win_v = v[:, seq - WINDOW:, :].reshape(batch, WINDOW, N_KV, HEAD_DIM)
    return y, jnp.stack(hs), jnp.stack(convs), win_k, win_v


def kernel(x_prompt, x_sample, cache_mem_k, cache_mem_v, state_lru_h, state_conv, cache_win_k, cache_win_v, mem_prompt, norm_g, w_in, w_out, mem_norm_g, w_mem_kv, conv_w, conv_b, lru_wa, lru_ba, lru_wx, lru_bx, lru_lambda, kv_norm_g, w_kv, sinks, final_g):
    W = _prepare_weights(norm_g, w_in, w_out, conv_w, conv_b, lru_wa, lru_ba, lru_wx, lru_bx, lru_lambda, kv_norm_g, w_kv, sinks, final_g)
    mk, mv, mkb, mvb = _mem_kv(mem_prompt, mem_norm_g, w_mem_kv.astype(BF16))
    y_p, h_p, conv_p, wk_p, wv_p = _prompt_trunk(x_prompt, W, mkb, mvb)
    y_s, h_s, conv_s, wk_s, wv_s = _sample_trunk(x_sample, W, cache_mem_k, cache_mem_v, state_lru_h,
                                                 state_conv, cache_win_k, cache_win_v)
    batch = x_prompt.shape[0]
    mk_p = mk.reshape(DEPTH, batch, N_MEM, MEM_HEADS, MEM_HD)
    mv_p = mv.reshape(DEPTH, batch, N_MEM, MEM_HEADS, MEM_HD)
    return (y_p, y_s, mk_p, mv_p, h_p, conv_p, wk_p, wv_p, h_s, conv_s, wk_s, wv_s)
```

```python
import functools

import jax
import jax.numpy as jnp
from jax import lax
from jax.experimental import pallas as pl
from jax.experimental.pallas import tpu as pltpu

F32 = jnp.float32
BF16 = jnp.bfloat16

D_MODEL = 1024
DEPTH = 4
N_A = DEPTH // 2
N_HEADS = 16
HEAD_DIM = 64
N_KV = 2
GROUP = N_HEADS // N_KV
D_MAIN = N_HEADS * HEAD_DIM
N_BLOCKS = 16
BLOCK_W = D_MAIN // N_BLOCKS
CONV_W = 4
LRU_C = 8.0
WINDOW = 128
ROPE_THETA = 10000.0
N_MEM = 256
MEM_HEADS = 4
MEM_HD = 128
D_MEMW = MEM_HEADS * MEM_HD
D_IN = 2 * D_MAIN + 2 * D_MEMW
D_CAT = D_MAIN + D_MEMW
D_KV = N_KV * HEAD_DIM
EPS = 1e-6
NEG = -1e30
LOG2_E = 1.4426950408889634
PAST_LEN = 8192

SUBLANES = 8
LANES = 128
N_SLABS = D_MAIN // LANES
GATE_TILE = 256
N_GATE_TILES = D_MAIN // GATE_TILE
VMEM_LIMIT_BYTES = 56 * 1024 * 1024
ATTN_SKEW = 4
MEM_CHUNK = 256
SCAN_CHUNKS = 16
PROJ_CHUNK = 512
OUT_CHUNK = 256

_NT = (((1,), (1,)), ((), ()))


def _const_spec(shape):
    zeros = (0,) * len(shape)
    return pl.BlockSpec(shape, lambda *_: zeros, pipeline_mode=pl.Buffered(1))


def _layer_spec(shape, layer):
    zeros = (0,) * len(shape)
    return pl.BlockSpec((None,) + tuple(shape), lambda *_: (layer,) + zeros, pipeline_mode=pl.Buffered(1))


def _rms(x, g):
    return x * lax.rsqrt(jnp.mean(x * x, axis=-1, keepdims=True) + EPS) * g


def _silu_of_half(hx):
    return hx * jnp.tanh(hx) + hx


def _softplus(x):
    return jnp.maximum(x, 0.0) + jnp.log1p(jnp.exp(-jnp.abs(x)))


def _decay_exponent(lam):
    return (-0.5 * LRU_C * LOG2_E) * _softplus(-lam)


def _rope(x, cos, sin_signed):
    lane = lax.broadcasted_iota(jnp.int32, x.shape, 1)
    first_half = (lane % HEAD_DIM) < (HEAD_DIM // 2)
    swapped = jnp.where(first_half,
                        pltpu.roll(x, LANES - HEAD_DIM // 2, 1),
                        pltpu.roll(x, HEAD_DIM // 2, 1))
    return x * cos + swapped * sin_signed


def _rglru_coeffs(uc, half_pre, half_ba, half_bx, c):
    half = half_pre.shape[-1] // 2
    t_r = jnp.tanh(half_pre[:, :half] + half_ba)
    t_i = jnp.tanh(half_pre[:, half:] + half_bx)
    a = jnp.exp2(c * t_r + c)
    w = 1.0 - a * a
    root = jnp.where(w > 0.0, w * lax.rsqrt(w), 0.0)
    return a, root * ((0.5 * t_i + 0.5) * uc)


def _mem_units(proj, mk_ref, mv_ref, z, row0, n_rows):
    chunk = min(n_rows, MEM_CHUNK)
    units = []
    for ch in range(n_rows // chunk):
        for hh in range(MEM_HEADS):
            rows = slice(row0 + chunk * ch, row0 + chunk * (ch + 1))
            cols = slice(MEM_HD * hh, MEM_HD * (hh + 1))
            q_cols = slice(2 * D_MAIN + MEM_HD * hh, 2 * D_MAIN + MEM_HD * (hh + 1))
            g_cols = slice(2 * D_MAIN + D_MEMW + MEM_HD * hh, 2 * D_MAIN + D_MEMW + MEM_HD * (hh + 1))
            z_cols = slice(D_MAIN + MEM_HD * hh, D_MAIN + MEM_HD * (hh + 1))

            def scores(rows=rows, cols=cols, q_cols=q_cols):
                q = proj[rows, q_cols].astype(BF16)
                s = lax.dot_general(mk_ref[0, :, cols], q, _NT,
                                    preferred_element_type=F32) * (MEM_HD ** -0.5 * LOG2_E)
                m = jnp.max(s, axis=0, keepdims=True)
                p = jnp.exp2(s - m)
                return p.astype(BF16), jnp.sum(p, axis=0, keepdims=True)

            def outputs(p, den, rows=rows, cols=cols, g_cols=g_cols, z_cols=z_cols):
                o_t = jnp.dot(mv_ref[0, cols, :], p, preferred_element_type=F32)
                z[rows, z_cols] = ((o_t / den).T * _silu_of_half(proj[rows, g_cols])).astype(BF16)

            units.append((scores, outputs))
    return units


def _skewed(units, skew):
    results = []
    first = lambda scores: lambda: results.append(scores())
    second = lambda outputs: lambda: outputs(*results.pop(0))
    thunks, pending = [], []
    for scores, outputs in units:
        thunks.append(first(scores))
        pending.append(second(outputs))
        if len(pending) > skew:
            thunks.append(pending.pop(0))
    return thunks + pending


def _interleave(*lists):
    lists = [l for l in lists if l]
    pos = [0] * len(lists)
    merged = []
    for _ in range(sum(len(l) for l in lists)):
        i = min((i for i in range(len(lists)) if pos[i] < len(lists[i])),
                key=lambda i: pos[i] / len(lists[i]))
        merged.append(lists[i][pos[i]])
        pos[i] += 1
    return merged


def _run(thunks):
    for thunk in thunks:
        thunk()


def _project_thunks(x_ref, ng_ref, win_ref, hn_s, proj_s, r0, n_rows):
    rows = slice(r0, r0 + n_rows)

    def norm():
        hn_s[rows, :] = _rms(x_ref[0, rows, :], ng_ref[...]).astype(BF16)

    def chunk(c0):
        def run():
            proj_s[rows, c0:c0 + PROJ_CHUNK] = jnp.dot(
                hn_s[rows, :], win_ref[:, c0:c0 + PROJ_CHUNK], preferred_element_type=F32)
        return run

    return [norm] + [chunk(c0) for c0 in range(0, D_IN, PROJ_CHUNK)]


def _out_project_thunks(x_ref, z, wout_ref, xo_ref, r0, n_rows):
    rows = slice(r0, r0 + n_rows)

    def chunk(c0):
        def run():
            cols = slice(c0, c0 + OUT_CHUNK)
            xo_ref[0, rows, cols] = x_ref[0, rows, cols] + jnp.dot(
                z[rows, :], wout_ref[:, cols], preferred_element_type=F32)
        return run

    return [chunk(c0) for c0 in range(0, D_MODEL, OUT_CHUNK)]


def _gate_tiles(wa, wx):
    per_tile = GATE_TILE // BLOCK_W
    eye = jnp.eye(per_tile, dtype=wa.dtype)

    def dense(w):
        w5 = w.reshape(N_A, N_GATE_TILES, per_tile, BLOCK_W, BLOCK_W)
        return jnp.einsum("ljicd,ik->ljickd", w5, eye).reshape(N_A, N_GATE_TILES, GATE_TILE, GATE_TILE)

    return jnp.concatenate([dense(wa), dense(wx)], axis=-1).astype(BF16)


def _prepare_weights(norm_g, w_in, w_out, conv_w, conv_b, lru_wa, lru_ba, lru_wx, lru_bx, lru_lambda,
                     kv_norm_g, w_kv, sinks, final_g):
    col = jnp.arange(D_IN)
    is_gate = ((col >= D_MAIN) & (col < 2 * D_MAIN)) | (col >= 2 * D_MAIN + D_MEMW)
    half = HEAD_DIM // 2
    inv = ROPE_THETA ** (-jnp.arange(half, dtype=F32) / half)
    sign = jnp.concatenate([-jnp.ones((half,), F32), jnp.ones((half,), F32)])
    return dict(
        norm_g=norm_g.reshape(DEPTH, 1, D_MODEL),
        w_in=(w_in * jnp.where(is_gate, 0.5, 1.0).astype(F32)).astype(BF16),
        w_out=w_out.astype(BF16),
        conv_w=conv_w, conv_b=conv_b.reshape(N_A, 1, D_MAIN),
        w_gate=_gate_tiles(0.5 * lru_wa, 0.5 * lru_wx),
        ba=0.5 * lru_ba.reshape(N_A, 1, D_MAIN), bx=0.5 * lru_bx.reshape(N_A, 1, D_MAIN),
        lam=lru_lambda.reshape(N_A, 1, D_MAIN),
        sinks=sinks, sinks_col=sinks.reshape(DEPTH - N_A, N_HEADS, 1),
        kv_norm_g=kv_norm_g.reshape(1, D_MODEL), w_kv=w_kv.astype(BF16),
        final_g=final_g.reshape(1, D_MODEL),
        inv128=jnp.tile(inv, LANES // half).reshape(1, LANES),
        sign128=jnp.tile(sign, LANES // HEAD_DIM).reshape(1, LANES))


def _rope_table_kernel(inv_ref, sign_ref, cos_ref, sin_ref):
    rows = cos_ref.shape[0]
    pos = (pl.program_id(0) * rows + lax.broadcasted_iota(jnp.int32, (rows, LANES), 0)).astype(F32)
    ang = pos * inv_ref[...]
    cos_ref[...] = jnp.cos(ang)
    sin_ref[...] = jnp.sin(ang) * sign_ref[...]


def _rope_tables(seq, inv128, sign128, rows=1024):
    rows = min(rows, seq)
    return pl.pallas_call(
        _rope_table_kernel,
        grid=(seq // rows,),
        in_specs=[_const_spec((1, LANES)), _const_spec((1, LANES))],
        out_specs=[pl.BlockSpec((rows, LANES), lambda i: (i, 0)),
                   pl.BlockSpec((rows, LANES), lambda i: (i, 0))],
        out_shape=[jax.ShapeDtypeStruct((seq, LANES), F32)] * 2,
        name="rope_tables",
    )(inv128, sign128)


def _mem_kv_kernel(m_ref, g_ref, w_ref, k_ref, v_ref, kb_ref, vb_ref):
    mn = _rms(m_ref[0], g_ref[0]).astype(BF16)
    kv = jnp.dot(mn, w_ref[0], preferred_element_type=F32)
    k, v = kv[:, :D_MEMW], kv[:, D_MEMW:]
    k_ref[0, 0] = k
    v_ref[0, 0] = v
    kb_ref[0, 0] = k.astype(BF16)
    vb_ref[0, 0] = v.T.astype(BF16)


def _mem_kv(mem_prompt, mem_norm_g, w_mem_kv_bf16):
    batch = mem_prompt.shape[0]
    out_spec = pl.BlockSpec((1, 1, N_MEM, D_MEMW), lambda l, b: (l, b, 0, 0))
    out_spec_t = pl.BlockSpec((1, 1, D_MEMW, N_MEM), lambda l, b: (l, b, 0, 0))
    return pl.pallas_call(
        _mem_kv_kernel,
        grid=(DEPTH, batch),
        in_specs=[pl.BlockSpec((1, N_MEM, D_MODEL), lambda l, b: (b, 0, 0)),
                  pl.BlockSpec((1, 1, D_MODEL), lambda l, b: (l, 0, 0)),
                  pl.BlockSpec((1, D_MODEL, 2 * D_MEMW), lambda l, b: (l, 0, 0))],
        out_specs=[out_spec, out_spec, out_spec, out_spec_t],
        out_shape=[jax.ShapeDtypeStruct((DEPTH, batch, N_MEM, D_MEMW), F32)] * 2
                  + [jax.ShapeDtypeStruct((DEPTH, batch, N_MEM, D_MEMW), BF16),
                     jax.ShapeDtypeStruct((DEPTH, batch, D_MEMW, N_MEM), BF16)],
        name="mem_kv",
    )(mem_prompt, mem_norm_g.reshape(DEPTH, 1, D_MODEL), w_mem_kv_bf16)


def _prompt_lru_kernel(emit_kv, tile, *refs):
    n_in = 16 if emit_kv else 12
    (x_ref, ng_ref, win_ref, wout_ref, cw_ref, cb_ref, wg_ref, ba_ref, bx_ref, lam_ref,
     mk_ref, mv_ref) = refs[:12]
    if emit_kv:
        kvg_ref, wkv_ref, cos_ref, sin_ref = refs[12:16]
        xo_ref, hl_ref, ct_ref, k_ref, v_ref = refs[n_in:n_in + 5]
        scratch = refs[n_in + 5:]
    else:
        xo_ref, hl_ref, ct_ref = refs[n_in:n_in + 3]
        scratch = refs[n_in + 3:]
    tails, a3, b3, h3, hc, z, hn_s, proj_s = scratch
    half = tile // 2
    t = pl.program_id(1)

    @pl.when(t == 0)
    def _():
        tails[...] = jnp.zeros((SUBLANES, D_MAIN), F32)
        hc[...] = jnp.zeros((SUBLANES, LANES), F32)

    decay_c = _decay_exponent(lam_ref[...])
    carry = [hc[...]]
    conv_out = {}


    project = lambda r0: _project_thunks(x_ref, ng_ref, win_ref, hn_s, proj_s, r0, half)

    def gates(r0):
        rows = slice(r0, r0 + half)
        g0, gn = r0 // SUBLANES, half // SUBLANES

        def conv():
            u = proj_s[rows, 0:D_MAIN]
            first_row = lax.broadcasted_iota(jnp.int32, (half, D_MAIN), 0) == 0
            acc = u * cw_ref[0:1, :]
            for k in range(1, CONV_W):
                delayed = jnp.where(first_row, tails[k - 1:k, :], pltpu.roll(acc, 1, 0))
                tails[k - 1:k, :] = acc[half - 1:half, :]
                acc = delayed + u * cw_ref[k:k + 1, :]
            conv_out[r0] = acc + cb_ref[...]

        def gate(j):
            def run():
                sl = slice(GATE_TILE * j, GATE_TILE * (j + 1))
                uc = conv_out[r0][:, sl]
                pre = jnp.dot(uc.astype(BF16), wg_ref[j], preferred_element_type=F32)
                a, b = _rglru_coeffs(uc, pre, ba_ref[:, sl], bx_ref[:, sl], decay_c[:, sl])
                for jj in range(GATE_TILE // LANES):
                    slab = j * (GATE_TILE // LANES) + jj
                    srows = slice(SUBLANES * slab, SUBLANES * (slab + 1))
                    cols = slice(LANES * jj, LANES * (jj + 1))
                    a3[g0:g0 + gn, srows, :] = a[:, cols].reshape(gn, SUBLANES, LANES)
                    b3[g0:g0 + gn, srows, :] = b[:, cols].reshape(gn, SUBLANES, LANES)
            return run

        return [conv] + [gate(j) for j in range(N_GATE_TILES)]

    def scan(r0):
        def piece(first, count):
            def run():
                h = carry[0]
                for g in range(first, first + count):
                    for r in range(SUBLANES):
                        step = pl.ds(r, N_SLABS, stride=SUBLANES)
                        h = a3[g, step, :] * h + b3[g, step, :]
                        h3[g, step, :] = h
                carry[0] = h
            return run

        g0, gn = r0 // SUBLANES, half // SUBLANES
        count = max(gn // SCAN_CHUNKS, 1)
        return [piece(g, count) for g in range(g0, g0 + gn, count)]

    def gating(r0):
        def slab(j):
            def run():
                g0, gn = r0 // SUBLANES, half // SUBLANES
                y = h3[g0:g0 + gn, SUBLANES * j:SUBLANES * (j + 1), :].reshape(half, LANES)
                g_main = proj_s[r0:r0 + half, D_MAIN + LANES * j:D_MAIN + LANES * (j + 1)]
                z[r0:r0 + half, LANES * j:LANES * (j + 1)] = (y * _silu_of_half(g_main)).astype(BF16)
            return run
        return [slab(j) for j in range(N_SLABS)]

    def out_project(r0):
        rows = slice(r0, r0 + half)

        def shared_kv():
            kvn = _rms(xo_ref[0, rows, :], kvg_ref[...]).astype(BF16)
            kv = jnp.dot(kvn, wkv_ref[...], preferred_element_type=F32)
            k_ref[0, rows, :] = _rope(kv[:, :D_KV], cos_ref[rows, :], sin_ref[rows, :])
            v_ref[0, rows, :] = kv[:, D_KV:]

        return _out_project_thunks(x_ref, z, wout_ref, xo_ref, r0, half) + ([shared_kv] if emit_kv else [])

    mem = lambda r0: _skewed(_mem_units(proj_s, mk_ref, mv_ref, z, r0, half), ATTN_SKEW)
    first, second = 0, half
    _run(project(first))
    _run(_interleave(project(second), gates(first)))
    _run(_interleave(mem(first), gates(second), scan(first)))
    _run(gating(first))
    _run(_interleave(out_project(first), mem(second), scan(second)))
    _run(gating(second))
    _run(out_project(second))

    h = carry[0]
    hc[...] = h

    @pl.when(t == pl.num_programs(1) - 1)
    def _():
        hl_ref[0] = h
        ct_ref[0] = proj_s[tile - SUBLANES:tile, 0:D_MAIN]


def _prompt_lru_layer(x, W, layer, mk, mv_t, kv_w=None, tile=256):
    batch, seq, _ = x.shape
    emit_kv = kv_w is not None
    tok = lambda b, t: (b, t, 0)
    per_b = lambda b, t: (b, 0, 0)
    mem = lambda b, t: (layer, b, 0, 0)
    in_specs = [
        pl.BlockSpec((1, tile, D_MODEL), tok),
        _layer_spec((1, D_MODEL), layer), _layer_spec((D_MODEL, D_IN), layer),
        _layer_spec((D_CAT, D_MODEL), layer),
        _layer_spec((CONV_W, D_MAIN), layer), _layer_spec((1, D_MAIN), layer),
        _layer_spec((N_GATE_TILES, GATE_TILE, 2 * GATE_TILE), layer),
        _layer_spec((1, D_MAIN), layer), _layer_spec((1, D_MAIN), layer), _layer_spec((1, D_MAIN), layer),
        pl.BlockSpec((None, 1, N_MEM, D_MEMW), mem), pl.BlockSpec((None, 1, D_MEMW, N_MEM), mem),
    ]
    args = [x, W["norm_g"], W["w_in"], W["w_out"], W["conv_w"], W["conv_b"], W["w_gate"],
            W["ba"], W["bx"], W["lam"], mk, mv_t]
    out_specs = [pl.BlockSpec((1, tile, D_MODEL), tok),
                 pl.BlockSpec((1, SUBLANES, LANES), per_b),
                 pl.BlockSpec((1, SUBLANES, D_MAIN), per_b)]
    out_shape = [jax.ShapeDtypeStruct((batch, seq, D_MODEL), F32),
                 jax.ShapeDtypeStruct((batch, SUBLANES, LANES), F32),
                 jax.ShapeDtypeStruct((batch, SUBLANES, D_MAIN), F32)]
    if emit_kv:
        in_specs += [_const_spec((1, D_MODEL)), _const_spec((D_MODEL, 2 * D_KV)),
                     pl.BlockSpec((tile, LANES), lambda b, t: (t, 0)),
                     pl.BlockSpec((tile, LANES), lambda b, t: (t, 0))]
        args += [kv_w["kv_norm_g"], kv_w["w_kv"], kv_w["cos"], kv_w["sin"]]
        out_specs += [pl.BlockSpec((1, tile, D_KV), tok)] * 2
        out_shape += [jax.ShapeDtypeStruct((batch, seq, D_KV), F32)] * 2
    groups = tile // SUBLANES
    return pl.pallas_call(
        functools.partial(_prompt_lru_kernel, emit_kv, tile),
        grid=(batch, seq // tile),
        in_specs=in_specs,
        out_specs=out_specs,
        out_shape=out_shape,
        scratch_shapes=[pltpu.VMEM((SUBLANES, D_MAIN), F32),
                        pltpu.VMEM((groups, SUBLANES * N_SLABS, LANES), F32),
                        pltpu.VMEM((groups, SUBLANES * N_SLABS, LANES), F32),
                        pltpu.VMEM((groups, SUBLANES * N_SLABS, LANES), F32),
                        pltpu.VMEM((SUBLANES, LANES), F32),
                        pltpu.VMEM((tile, D_CAT), BF16),
                        pltpu.VMEM((tile, D_MODEL), BF16),
                        pltpu.VMEM((tile, D_IN), F32)],
        compiler_params=pltpu.CompilerParams(
            dimension_semantics=("arbitrary", "arbitrary"), vmem_limit_bytes=VMEM_LIMIT_BYTES),
        name="prompt_lru_kv" if emit_kv else "prompt_lru",
    )(*args)


def _sink_softmax(s, sink, axis=-1):
    m = jnp.maximum(jnp.max(s, axis=axis, keepdims=True), sink)
    p = jnp.exp2(s - m)
    den = jnp.sum(p, axis=axis, keepdims=True) + jnp.exp2(sink - m)
    return p, den


def _prompt_swa_kernel(final, tile, sink_row, *refs):
    n_in = 14 if final else 13
    (x_ref, ng_ref, win_ref, wout_ref, mk_ref, mv_ref, sink_ref, kp_ref, kc_ref, vp_ref, vc_ref,
     cos_ref, sin_ref) = refs[:13]
    if final:
        fg_ref = refs[13]
        xo_ref, y_ref = refs[n_in:n_in + 2]
        scratch = refs[n_in + 2:]
    else:
        xo_ref, = refs[n_in:n_in + 1]
        scratch = refs[n_in + 1:]
    z, qz, kcat, vcat_t, hn_s, proj_s = scratch
    half = tile // 2
    t = pl.program_id(1)

    def rope_queries(r0):
        rows = slice(r0, r0 + half)

        def slab(j):
            def run():
                cos = cos_ref[rows, :]
                sin = sin_ref[rows, :]
                lane = lax.broadcasted_iota(jnp.int32, (half, LANES), 1)
                lo = lane < HEAD_DIM
                first_half = (lane % HEAD_DIM) < (HEAD_DIM // 2)
                c = j // (GROUP // 2)
                xq = proj_s[rows, LANES * j:LANES * (j + 1)] * (HEAD_DIM ** -0.5 * LOG2_E)
                r32 = pltpu.roll(xq, HEAD_DIM // 2, 1)
                r96 = pltpu.roll(xq, LANES - HEAD_DIM // 2, 1)
                r64 = pltpu.roll(xq, HEAD_DIM, 1)
                stay = xq * cos + jnp.where(first_half, r96, r32) * sin
                move = r64 * cos + jnp.where(first_half, r32, r96) * sin
                keep = lo if c == 0 else ~lo
                even, odd = (stay, move) if c == 0 else (move, stay)
                qz[rows, 2 * LANES * j:2 * LANES * j + LANES] = jnp.where(keep, even, 0.0).astype(BF16)
                qz[rows, 2 * LANES * j + LANES:2 * LANES * (j + 1)] = jnp.where(keep, odd, 0.0).astype(BF16)
            return run

        return [slab(j) for j in range(N_SLABS)]

    kcat[0:WINDOW, :] = kp_ref[0].astype(BF16)
    kcat[WINDOW:, :] = kc_ref[0].astype(BF16)
    vcat_t[:, 0:WINDOW] = vp_ref[0].T.astype(BF16)
    vcat_t[:, WINDOW:] = vc_ref[0].T.astype(BF16)

    key_r = lax.broadcasted_iota(jnp.int32, (WINDOW, 2 * WINDOW), 0)
    query = lax.broadcasted_iota(jnp.int32, (WINDOW, 2 * WINDOW), 1) % WINDOW
    from_prev = key_r > query

    def scores(n, j):
        rows = slice(WINDOW * n, WINDOW * (n + 1))
        keys = kcat[WINDOW * n:WINDOW * (n + 2), :]
        q2 = jnp.concatenate([qz[rows, 2 * LANES * j:2 * LANES * j + LANES],
                              qz[rows, 2 * LANES * j + LANES:2 * LANES * (j + 1)]], axis=0)
        s = lax.dot_general(keys, q2, _NT, preferred_element_type=F32)
        s_prev, s_own = s[:WINDOW, :], s[WINDOW:, :]
        if n == 0:
            s_prev = jnp.where(t > 0, s_prev, NEG)
        visible = jnp.where(from_prev, s_prev, s_own)
        p0, den0 = _sink_softmax(visible[:, :WINDOW], sink_ref[sink_row, 2 * j] * LOG2_E, axis=0)
        p1, den1 = _sink_softmax(visible[:, WINDOW:], sink_ref[sink_row, 2 * j + 1] * LOG2_E, axis=0)
        p = jnp.concatenate([p0, p1], axis=1).astype(BF16)
        zero = jnp.zeros_like(p)
        return jnp.concatenate([jnp.where(from_prev, p, zero), jnp.where(from_prev, zero, p)], axis=0), den0, den1

    def outputs(n, j, p, den0, den1):
        rows = slice(WINDOW * n, WINDOW * (n + 1))
        vals_t = vcat_t[:, WINDOW * n:WINDOW * (n + 2)]
        o_t = jnp.dot(vals_t, p, preferred_element_type=F32)
        c = j // (GROUP // 2)
        own = slice(HEAD_DIM * c, HEAD_DIM * (c + 1))
        y_t = jnp.concatenate([o_t[own, :WINDOW] / den0, o_t[own, WINDOW:] / den1], axis=0)
        g_main = proj_s[rows, D_MAIN + LANES * j:D_MAIN + LANES * (j + 1)]
        z[rows, LANES * j:LANES * (j + 1)] = (y_t.T * _silu_of_half(g_main)).astype(BF16)

    def attend(r0):
        units = [(functools.partial(scores, n, j), functools.partial(outputs, n, j))
                 for n in range(r0 // WINDOW, (r0 + half) // WINDOW) for j in range(N_SLABS)]
        return _skewed(units + _mem_units(proj_s, mk_ref, mv_ref, z, r0, half), ATTN_SKEW)

    def out_project(r0):
        rows = slice(r0, r0 + half)

        def final_norm():
            y_ref[0, rows, :] = _rms(xo_ref[0, rows, :], fg_ref[...])

        return _out_project_thunks(x_ref, z, wout_ref, xo_ref, r0, half) + ([final_norm] if final else [])

    project = lambda r0: _project_thunks(x_ref, ng_ref, win_ref, hn_s, proj_s, r0, half)
    first, second = 0, half
    _run(project(first))
    _run(_interleave(project(second), rope_queries(first)))
    _run(_interleave(attend(first), rope_queries(second)))
    _run(_interleave(attend(second), out_project(first)))
    _run(out_project(second))


def _prompt_swa_layer(x, W, layer, mk, mv_t, k, v, cos, sin, final_g=None, tile=256):
    batch, seq, _ = x.shape
    final = final_g is not None
    blocks = tile // WINDOW
    tok = lambda b, t: (b, t, 0)
    mem = lambda b, t: (layer, b, 0, 0)
    prev_blk = lambda b, t: (b, jnp.maximum(t * blocks - 1, 0), 0)
    in_specs = [
        pl.BlockSpec((1, tile, D_MODEL), tok),
        _layer_spec((1, D_MODEL), layer), _layer_spec((D_MODEL, D_IN), layer),
        _layer_spec((D_CAT, D_MODEL), layer),
        pl.BlockSpec((None, 1, N_MEM, D_MEMW), mem), pl.BlockSpec((None, 1, D_MEMW, N_MEM), mem),
        pl.BlockSpec(memory_space=pltpu.SMEM),
        pl.BlockSpec((1, WINDOW, D_KV), prev_blk), pl.BlockSpec((1, tile, D_KV), tok),
        pl.BlockSpec((1, WINDOW, D_KV), prev_blk), pl.BlockSpec((1, tile, D_KV), tok),
        pl.BlockSpec((tile, LANES), lambda b, t: (t, 0)), pl.BlockSpec((tile, LANES), lambda b, t: (t, 0)),
    ]
    args = [x, W["norm_g"], W["w_in"], W["w_out"], mk, mv_t, W["sinks"], k, k, v, v, cos, sin]
    out_specs = [pl.BlockSpec((1, tile, D_MODEL), tok)]
    out_shape = [jax.ShapeDtypeStruct((batch, seq, D_MODEL), F32)]
    if final:
        in_specs.append(_const_spec((1, D_MODEL)))
        args.append(final_g)
        out_specs.append(pl.BlockSpec((1, tile, D_MODEL), tok))
        out_shape.append(jax.ShapeDtypeStruct((batch, seq, D_MODEL), F32))
    return pl.pallas_call(
        functools.partial(_prompt_swa_kernel, final, tile, layer - N_A),
        grid=(batch, seq // tile),
        in_specs=in_specs,
        out_specs=out_specs,
        out_shape=out_shape,
        scratch_shapes=[pltpu.VMEM((tile, D_CAT), BF16),
                        pltpu.VMEM((tile, 2 * D_MAIN), BF16),
                        pltpu.VMEM((WINDOW + tile, D_KV), BF16),
                        pltpu.VMEM((D_KV, WINDOW + tile), BF16),
                        pltpu.VMEM((tile, D_MODEL), BF16),
                        pltpu.VMEM((tile, D_IN), F32)],
        compiler_params=pltpu.CompilerParams(
            dimension_semantics=("arbitrary", "arbitrary"), vmem_limit_bytes=VMEM_LIMIT_BYTES),
        name="prompt_swa_final" if final else "prompt_swa",
    )(*args)


def _sample_mem_unit(b, bb, qm_s, mk_ref, mv_ref, om_s):
    def scores():
        row = lax.broadcasted_iota(jnp.int32, (SUBLANES, MEM_HD), 0)
        q = qm_s[pl.ds(b, 1), :]
        qh = jnp.zeros((SUBLANES, MEM_HD), F32)
        for hh in range(MEM_HEADS):
            q_head = jnp.broadcast_to(q[:, MEM_HD * hh:MEM_HD * (hh + 1)], (SUBLANES, MEM_HD))
            qh = jnp.where(row == hh, q_head, qh)
        s = lax.dot_general(qh.astype(BF16), mk_ref[bb].astype(BF16), _NT,
                            preferred_element_type=F32) * (MEM_HD ** -0.5 * LOG2_E)
        srow = lax.broadcasted_iota(jnp.int32, s.shape, 0)
        scol = lax.broadcasted_iota(jnp.int32, s.shape, 1)
        s = jnp.where((scol % MEM_HEADS) == srow, s, NEG)
        m = jnp.max(s, axis=-1, keepdims=True)
        p = jnp.exp2(s - m)
        return p.astype(BF16), jnp.sum(p, axis=-1, keepdims=True)

    def outputs(p, den):
        o = jnp.dot(p, mv_ref[bb].astype(BF16), preferred_element_type=F32) / den
        om_s[pl.ds(b, 1), :] = jnp.concatenate([o[hh:hh + 1, :] for hh in range(MEM_HEADS)], axis=1)

    return scores, outputs


def _sample_lru_kernel(seqs, x_ref, ng_ref, win_ref, wout_ref, cw_ref, cb_ref, wg_ref, ba_ref, bx_ref,
                       lam_ref, h0_ref, c0_ref, mk_ref, mv_ref, xo_ref, ho_ref, co_ref,
                       z_s, qm_s, gm_s, om_s):
    i = pl.program_id(0)

    @pl.when(i == 0)
    def _():
        hn = _rms(x_ref[...], ng_ref[...]).astype(BF16)
        proj = jnp.dot(hn, win_ref[...], preferred_element_type=F32)
        u = proj[:, :D_MAIN]
        taps = [c0_ref[k] for k in range(CONV_W - 1)] + [u]
        uc = cb_ref[...] + taps[0] * cw_ref[0:1, :]
        for k in range(1, CONV_W):
            uc = uc + taps[k] * cw_ref[k:k + 1, :]
        for k in range(CONV_W - 1):
            co_ref[k] = taps[k + 1]
        decay_c = _decay_exponent(lam_ref[...])
        ucb = uc.astype(BF16)
        for j in range(N_GATE_TILES):
            sl = slice(GATE_TILE * j, GATE_TILE * (j + 1))
            pre = jnp.dot(ucb[:, sl], wg_ref[j], preferred_element_type=F32)
            a, b = _rglru_coeffs(uc[:, sl], pre, ba_ref[:, sl], bx_ref[:, sl], decay_c[:, sl])
            h = a * h0_ref[:, sl] + b
            ho_ref[:, sl] = h
            z_s[:, sl] = (h * _silu_of_half(proj[:, D_MAIN + GATE_TILE * j:D_MAIN + GATE_TILE * (j + 1)])).astype(BF16)
        qm_s[...] = proj[:, 2 * D_MAIN:2 * D_MAIN + D_MEMW]
        gm_s[...] = _silu_of_half(proj[:, 2 * D_MAIN + D_MEMW:])

    _run(_skewed([_sample_mem_unit(i * seqs + bb, bb, qm_s, mk_ref, mv_ref, om_s) for bb in range(seqs)],
                 ATTN_SKEW))

    @pl.when(i == pl.num_programs(0) - 1)
    def _():
        z_s[:, D_MAIN:] = (om_s[...] * gm_s[...]).astype(BF16)
        xo_ref[...] = x_ref[...] + jnp.dot(z_s[...], wout_ref[...], preferred_element_type=F32)


def _cache_spec(layer, seqs):
    return pl.BlockSpec((None, seqs, N_MEM * MEM_HEADS, MEM_HD), lambda i: (layer, i, 0, 0))


def _sample_lru_layer(x, W, layer, state_h, state_conv_t, cache_k, cache_v, seqs=8):
    batch = x.shape[0]
    cache_spec = _cache_spec(layer, seqs)
    return pl.pallas_call(
        functools.partial(_sample_lru_kernel, seqs),
        grid=(batch // seqs,),
        in_specs=[_const_spec((batch, D_MODEL)), _layer_spec((1, D_MODEL), layer),
                  _layer_spec((D_MODEL, D_IN), layer), _layer_spec((D_CAT, D_MODEL), layer),
                  _layer_spec((CONV_W, D_MAIN), layer), _layer_spec((1, D_MAIN), layer),
                  _layer_spec((N_GATE_TILES, GATE_TILE, 2 * GATE_TILE), layer),
                  _layer_spec((1, D_MAIN), layer), _layer_spec((1, D_MAIN), layer),
                  _layer_spec((1, D_MAIN), layer),
                  _layer_spec((batch, D_MAIN), layer), _layer_spec((CONV_W - 1, batch, D_MAIN), layer),
                  cache_spec, cache_spec],
        out_specs=[pl.BlockSpec((batch, D_MODEL), lambda i: (0, 0)),
                   pl.BlockSpec((batch, D_MAIN), lambda i: (0, 0)),
                   pl.BlockSpec((CONV_W - 1, batch, D_MAIN), lambda i: (0, 0, 0))],
        out_shape=[jax.ShapeDtypeStruct((batch, D_MODEL), F32),
                   jax.ShapeDtypeStruct((batch, D_MAIN), F32),
                   jax.ShapeDtypeStruct((CONV_W - 1, batch, D_MAIN), F32)],
        scratch_shapes=[pltpu.VMEM((batch, D_CAT), BF16), pltpu.VMEM((batch, D_MEMW), F32),
                        pltpu.VMEM((batch, D_MEMW), F32), pltpu.VMEM((batch, D_MEMW), F32)],
        compiler_params=pltpu.CompilerParams(
            dimension_semantics=("arbitrary",), vmem_limit_bytes=VMEM_LIMIT_BYTES),
        name="sample_lru",
    )(x, W["norm_g"], W["w_in"], W["w_out"], W["conv_w"], W["conv_b"], W["w_gate"],
      W["ba"], W["bx"], W["lam"], state_h, state_conv_t, cache_k, cache_v)


def _sample_swa_kernel(first, final, seqs, *refs):
    (x_ref, ng_ref, win_ref, wout_ref, sink_ref, mk_ref, mv_ref, wk_ref, wv_ref) = refs[:9]
    pos = 9
    if first:
        kvg_ref, wkv_ref, inv_ref, sign_ref = refs[pos:pos + 4]
        pos += 4
    else:
        inv_ref, sign_ref = refs[pos:pos + 2]
        pos += 2
    if final:
        fg_ref = refs[pos]
        pos += 1
    xo_ref = refs[pos]
    pos += 1
    if first:
        wko_ref, wvo_ref = refs[pos:pos + 2]
        pos += 2
    if final:
        y_ref = refs[pos]
        pos += 1
    z_s, q_s, gmain_s, y_s, qm_s, gm_s, om_s, kn_s, vn_s = refs[pos:]
    i = pl.program_id(0)

    @pl.when(i == 0)
    def _():
        x = x_ref[...]
        hn = _rms(x, ng_ref[...]).astype(BF16)
        proj = jnp.dot(hn, win_ref[...], preferred_element_type=F32)
        ang = float(PAST_LEN) * inv_ref[...]
        cos = jnp.cos(ang)
        sin = jnp.sin(ang) * sign_ref[...]
        for j in range(N_SLABS):
            cols = slice(LANES * j, LANES * (j + 1))
            q_s[:, cols] = _rope(proj[:, cols], cos, sin)
        gmain_s[...] = _silu_of_half(proj[:, D_MAIN:2 * D_MAIN])
        qm_s[...] = proj[:, 2 * D_MAIN:2 * D_MAIN + D_MEMW]
        gm_s[...] = _silu_of_half(proj[:, 2 * D_MAIN + D_MEMW:])
        if first:
            kvn = _rms(x, kvg_ref[...]).astype(BF16)
            kv = jnp.dot(kvn, wkv_ref[...], preferred_element_type=F32)
            kn_s[...] = _rope(kv[:, :D_KV], cos, sin).T
            vn_s[...] = kv[:, D_KV:].T

    row = lax.broadcasted_iota(jnp.int32, (N_HEADS, LANES), 0)
    lane = lax.broadcasted_iota(jnp.int32, (N_HEADS, LANES), 1)
    kv_half = (lane // HEAD_DIM) == (row // GROUP)
    in_place = (row % 2) == (row // GROUP)
    lane1 = lax.broadcasted_iota(jnp.int32, (1, LANES), 1)
    seq_lane = lax.broadcasted_iota(jnp.int32, kn_s.shape, 1)
    slot = lax.broadcasted_iota(jnp.int32, (D_KV, WINDOW), 1)

    def slide(win_t, new_t_s, b):
        if new_t_s.shape[1] == WINDOW:
            col = pltpu.roll(new_t_s[...], WINDOW - 1 - b, 1)
        else:
            col = jnp.sum(jnp.where(seq_lane == b, new_t_s[...], 0.0), axis=1, keepdims=True)
        return jnp.where(slot == WINDOW - 1, col, pltpu.roll(win_t, WINDOW - 1, 1))

    def window_unit(b, bb):
        def scores():
            if first:
                kw_t = slide(wk_ref[bb], kn_s, b)
                wko_ref[bb] = kw_t
            else:
                kw_t = wk_ref[bb]
            q = jnp.broadcast_to(q_s[pl.ds(b, 1), :], (N_HEADS, D_MAIN))
            e = jnp.zeros((N_HEADS, LANES), F32)
            for j in range(N_SLABS):
                e = jnp.where((row // 2) == j, q[:, LANES * j:LANES * (j + 1)], e)
            qh = jnp.where(kv_half, jnp.where(in_place, e, pltpu.roll(e, HEAD_DIM, 1)), 0.0).astype(BF16)
            s = jnp.dot(qh, kw_t.astype(BF16), preferred_element_type=F32) * (HEAD_DIM ** -0.5 * LOG2_E)
            p, den = _sink_softmax(s, sink_ref[...] * LOG2_E)
            return p.astype(BF16), den

        def outputs(p, den):
            if first:
                vw_t = slide(wv_ref[bb], vn_s, b)
                wvo_ref[bb] = vw_t
            else:
                vw_t = wv_ref[bb]
            o = lax.dot_general(p, vw_t.astype(BF16), _NT, preferred_element_type=F32) / den
            f = jnp.where(in_place, o, pltpu.roll(o, HEAD_DIM, 1))
            y_s[pl.ds(b, 1), :] = jnp.concatenate(
                [jnp.where(lane1 < HEAD_DIM, f[2 * j:2 * j + 1, :], f[2 * j + 1:2 * j + 2, :])
                 for j in range(N_SLABS)], axis=1)

        return scores, outputs

    units = []
    for bb in range(seqs):
        units.append(window_unit(i * seqs + bb, bb))
        units.append(_sample_mem_unit(i * seqs + bb, bb, qm_s, mk_ref, mv_ref, om_s))
    _run(_skewed(units, ATTN_SKEW))

    @pl.when(i == pl.num_programs(0) - 1)
    def _():
        z_s[:, :D_MAIN] = (y_s[...] * gmain_s[...]).astype(BF16)
        z_s[:, D_MAIN:] = (om_s[...] * gm_s[...]).astype(BF16)
        xn = x_ref[...] + jnp.dot(z_s[...], wout_ref[...], preferred_element_type=F32)
        xo_ref[...] = xn
        if final:
            y_ref[...] = _rms(xn, fg_ref[...])


def _sample_swa_layer(x, W, layer, cache_k, cache_v, win_k_t, win_v_t, first, final, seqs=8):
    batch = x.shape[0]
    cache_spec = _cache_spec(layer, seqs)
    win_spec = pl.BlockSpec((seqs, D_KV, WINDOW), lambda i: (i, 0, 0))
    whole = lambda shape: pl.BlockSpec(shape, lambda i: (0,) * len(shape))
    in_specs = [_const_spec((batch, D_MODEL)), _layer_spec((1, D_MODEL), layer),
                _layer_spec((D_MODEL, D_IN), layer), _layer_spec((D_CAT, D_MODEL), layer),
                _layer_spec((N_HEADS, 1), layer - N_A), cache_spec, cache_spec, win_spec, win_spec]
    args = [x, W["norm_g"], W["w_in"], W["w_out"], W["sinks_col"], cache_k, cache_v, win_k_t, win_v_t]
    if first:
        in_specs += [_const_spec((1, D_MODEL)), _const_spec((D_MODEL, 2 * D_KV))]
        args += [W["kv_norm_g"], W["w_kv"]]
    in_specs += [_const_spec((1, LANES)), _const_spec((1, LANES))]
    args += [W["inv128"], W["sign128"]]
    if final:
        in_specs.append(_const_spec((1, D_MODEL)))
        args.append(W["final_g"])
    out_specs = [whole((batch, D_MODEL))]
    out_shape = [jax.ShapeDtypeStruct((batch, D_MODEL), F32)]
    if first:
        out_specs += [win_spec, win_spec]
        out_shape += [jax.ShapeDtypeStruct((batch, D_KV, WINDOW), F32)] * 2
    if final:
        out_specs.append(whole((batch, D_MODEL)))
        out_shape.append(jax.ShapeDtypeStruct((batch, D_MODEL), F32))
    return pl.pallas_call(
        functools.partial(_sample_swa_kernel, first, final, seqs),
        grid=(batch // seqs,),
        in_specs=in_specs,
        out_specs=out_specs,
        out_shape=out_shape,
        scratch_shapes=[pltpu.VMEM((batch, D_CAT), BF16), pltpu.VMEM((batch, D_MAIN), F32),
                        pltpu.VMEM((batch, D_MAIN), F32), pltpu.VMEM((batch, D_MAIN), F32),
                        pltpu.VMEM((batch, D_MEMW), F32), pltpu.VMEM((batch, D_MEMW), F32),
                        pltpu.VMEM((batch, D_MEMW), F32),
                        pltpu.VMEM((D_KV, batch), F32), pltpu.VMEM((D_KV, batch), F32)],
        compiler_params=pltpu.CompilerParams(
            dimension_semantics=("arbitrary",), vmem_limit_bytes=VMEM_LIMIT_BYTES),
        name="sample_swa" + ("_first" if first else "") + ("_final" if final else ""),
    )(*args)


def _sample_trunk(x, W, cache_mem_k, cache_mem_v, state_h, state_conv, win_k, win_v, seqs=8):
    batch = x.shape[0]
    x = x.reshape(batch, D_MODEL)
    ck = cache_mem_k.reshape(DEPTH, batch, N_MEM * MEM_HEADS, MEM_HD)
    cv = cache_mem_v.reshape(DEPTH, batch, N_MEM * MEM_HEADS, MEM_HD)
    conv_t = jnp.transpose(state_conv, (0, 2, 1, 3))
    wk = jnp.transpose(win_k, (0, 2, 3, 1)).reshape(batch, D_KV, WINDOW)
    wv = jnp.transpose(win_v, (0, 2, 3, 1)).reshape(batch, D_KV, WINDOW)
    hs, convs = [], []
    for l in range(N_A):
        x, h, c = _sample_lru_layer(x, W, l, state_h, conv_t, ck, cv, seqs=seqs)
        hs.append(h)
        convs.append(c)
    y = None
    for l in range(N_A, DEPTH):
        outs = _sample_swa_layer(x, W, l, ck, cv, wk, wv, first=l == N_A, final=l == DEPTH - 1, seqs=seqs)
        x = outs[0]
        if l == N_A:
            wk, wv = outs[1], outs[2]
        if l == DEPTH - 1:
            y = outs[-1]
    unslide = lambda w_t: jnp.transpose(w_t.reshape(batch, N_KV, HEAD_DIM, WINDOW), (0, 3, 1, 2))
    return (y.reshape(batch, 1, D_MODEL), jnp.stack(hs), jnp.transpose(jnp.stack(convs), (0, 2, 1, 3)),
            unslide(wk), unslide(wv))


def _prompt_trunk(x, W, mkb, mvb_t, tile=512):
    batch, seq, _ = x.shape
    cos, sin = _rope_tables(seq, W["inv128"], W["sign128"])
    kv_w = dict(kv_norm_g=W["kv_norm_g"], w_kv=W["w_kv"], cos=cos, sin=sin)
    hs, convs = [], []
    k = v = None
    for l in range(N_A):
        outs = _prompt_lru_layer(x, W, l, mkb, mvb_t, kv_w=kv_w if l == N_A - 1 else None, tile=tile)
        x, h_last, conv_tail = outs[:3]
        if l == N_A - 1:
            k, v = outs[3:]
        hs.append(h_last.reshape(batch, D_MAIN))
        convs.append(conv_tail[:, SUBLANES - (CONV_W - 1):, :])
    y = None
    for l in range(N_A, DEPTH):
        last = l == DEPTH - 1
        outs = _prompt_swa_layer(x, W, l, mkb, mvb_t, k, v, cos, sin,
                                 final_g=W["final_g"] if last else None, tile=tile)
        x = outs[0]
        if last:
            y = outs[1]
    win_k = k[:, seq - WINDOW:, :].reshape(batch, WINDOW, N_KV, HEAD_DIM)
    win_v = v[:, seq - WINDOW:, :].reshape(batch, WINDOW, N_KV, HEAD_DIM)
    return y, jnp.stack(hs), jnp.stack(convs), win_k, win_v


def kernel(x_prompt, x_sample, cache_mem_k, cache_mem_v, state_lru_h, state_conv, cache_win_k, cache_win_v, mem_prompt, norm_g, w_in, w_out, mem_norm_g, w_mem_kv, conv_w, conv_b, lru_wa, lru_ba, lru_wx, lru_bx, lru_lambda, kv_norm_g, w_kv, sinks, final_g):
    W = _prepare_weights(norm_g, w_in, w_out, conv_w, conv_b, lru_wa, lru_ba, lru_wx, lru_bx, lru_lambda, kv_norm_g, w_kv, sinks, final_g)
    mk, mv, mkb, mvb = _mem_kv(mem_prompt, mem_norm_g, w_mem_kv.astype(BF16))
    y_p, h_p, conv_p, wk_p, wv_p = _prompt_trunk(x_prompt, W, mkb, mvb)
    y_s, h_s, conv_s, wk_s, wv_s = _sample_trunk(x_sample, W, cache_mem_k, cache_mem_v, state_lru_h,
                                                 state_conv, cache_win_k, cache_win_v)
    batch = x_prompt.shape[0]
    mk_p = mk.reshape(DEPTH, batch, N_MEM, MEM_HEADS, MEM_HD)
    mv_p = mv.reshape(DEPTH, batch, N_MEM, MEM_HEADS, MEM_HD)
    return (y_p, y_s, mk_p, mv_p, h_p, conv_p, wk_p, wv_p, h_s, conv_s, wk_s, wv_s)
```

```python
import functools

import jax
import jax.numpy as jnp
from jax import lax
from jax.experimental import pallas as pl
from jax.experimental.pallas import tpu as pltpu

F32 = jnp.float32
BF16 = jnp.bfloat16

D_MODEL = 1024
DEPTH = 4
N_A = DEPTH // 2
N_HEADS = 16
HEAD_DIM = 64
N_KV = 2
GROUP = N_HEADS // N_KV
D_MAIN = N_HEADS * HEAD_DIM
N_BLOCKS = 16
BLOCK_W = D_MAIN // N_BLOCKS
CONV_W = 4
LRU_C = 8.0
WINDOW = 128
ROPE_THETA = 10000.0
N_MEM = 256
MEM_HEADS = 4
MEM_HD = 128
D_MEMW = MEM_HEADS * MEM_HD
D_IN = 2 * D_MAIN + 2 * D_MEMW
D_CAT = D_MAIN + D_MEMW
D_KV = N_KV * HEAD_DIM
EPS = 1e-6
NEG = -1e30
LOG2_E = 1.4426950408889634
PAST_LEN = 8192

SUBLANES = 8
LANES = 128
N_SLABS = D_MAIN // LANES
GATE_TILE = 256
N_GATE_TILES = D_MAIN // GATE_TILE
VMEM_LIMIT_BYTES = 56 * 1024 * 1024
ATTN_SKEW = 4
MEM_CHUNK = 256
SCAN_CHUNKS = 16
PROJ_CHUNK = 512
OUT_CHUNK = 256

_NT = (((1,), (1,)), ((), ()))


def _const_spec(shape):
    zeros = (0,) * len(shape)
    return pl.BlockSpec(shape, lambda *_: zeros, pipeline_mode=pl.Buffered(1))


def _layer_spec(shape, layer):
    zeros = (0,) * len(shape)
    return pl.BlockSpec((None,) + tuple(shape), lambda *_: (layer,) + zeros, pipeline_mode=pl.Buffered(1))


def _rms(x, g):
    return x * lax.rsqrt(jnp.mean(x * x, axis=-1, keepdims=True) + EPS) * g


def _silu_of_half(hx):
    return hx * jnp.tanh(hx) + hx


def _softplus(x):
    return jnp.maximum(x, 0.0) + jnp.log1p(jnp.exp(-jnp.abs(x)))


def _decay_exponent(lam):
    return (-0.5 * LRU_C * LOG2_E) * _softplus(-lam)


def _rope(x, cos, sin_signed):
    lane = lax.broadcasted_iota(jnp.int32, x.shape, 1)
    first_half = (lane % HEAD_DIM) < (HEAD_DIM // 2)
    swapped = jnp.where(first_half,
                        pltpu.roll(x, LANES - HEAD_DIM // 2, 1),
                        pltpu.roll(x, HEAD_DIM // 2, 1))
    return x * cos + swapped * sin_signed


def _rglru_coeffs(uc, half_pre, half_ba, half_bx, c):
    half = half_pre.shape[-1] // 2
    t_r = jnp.tanh(half_pre[:, :half] + half_ba)
    t_i = jnp.tanh(half_pre[:, half:] + half_bx)
    a = jnp.exp2(c * t_r + c)
    w = 1.0 - a * a
    root = jnp.where(w > 0.0, w * lax.rsqrt(w), 0.0)
    return a, root * ((0.5 * t_i + 0.5) * uc)


def _mem_units(proj, mk_ref, mv_ref, z, row0, n_rows):
    chunk = min(n_rows, MEM_CHUNK)
    units = []
    for ch in range(n_rows // chunk):
        for hh in range(MEM_HEADS):
            rows = slice(row0 + chunk * ch, row0 + chunk * (ch + 1))
            cols = slice(MEM_HD * hh, MEM_HD * (hh + 1))
            q_cols = slice(2 * D_MAIN + MEM_HD * hh, 2 * D_MAIN + MEM_HD * (hh + 1))
            g_cols = slice(2 * D_MAIN + D_MEMW + MEM_HD * hh, 2 * D_MAIN + D_MEMW + MEM_HD * (hh + 1))
            z_cols = slice(D_MAIN + MEM_HD * hh, D_MAIN + MEM_HD * (hh + 1))

            def scores(rows=rows, cols=cols, q_cols=q_cols):
                q = proj[rows, q_cols].astype(BF16)
                s = lax.dot_general(mk_ref[0, :, cols], q, _NT,
                                    preferred_element_type=F32) * (MEM_HD ** -0.5 * LOG2_E)
                m = jnp.max(s, axis=0, keepdims=True)
                p = jnp.exp2(s - m)
                return p.astype(BF16), jnp.sum(p, axis=0, keepdims=True)

            def outputs(p, den, rows=rows, cols=cols, g_cols=g_cols, z_cols=z_cols):
                o_t = jnp.dot(mv_ref[0, cols, :], p, preferred_element_type=F32)
                z[rows, z_cols] = ((o_t / den).T * _silu_of_half(proj[rows, g_cols])).astype(BF16)

            units.append((scores, outputs))
    return units


def _skewed(units, skew):
    results = []
    first = lambda scores: lambda: results.append(scores())
    second = lambda outputs: lambda: outputs(*results.pop(0))
    thunks, pending = [], []
    for scores, outputs in units:
        thunks.append(first(scores))
        pending.append(second(outputs))
        if len(pending) > skew:
            thunks.append(pending.pop(0))
    return thunks + pending


def _interleave(*lists):
    lists = [l for l in lists if l]
    pos = [0] * len(lists)
    merged = []
    for _ in range(sum(len(l) for l in lists)):
        i = min((i for i in range(len(lists)) if pos[i] < len(lists[i])),
                key=lambda i: pos[i] / len(lists[i]))
        merged.append(lists[i][pos[i]])
        pos[i] += 1
    return merged


def _run(thunks):
    for thunk in thunks:
        thunk()


def _project_thunks(x_ref, ng_ref, win_ref, hn_s, proj_s, r0, n_rows):
    rows = slice(r0, r0 + n_rows)

    def norm():
        hn_s[rows, :] = _rms(x_ref[0, rows, :], ng_ref[...]).astype(BF16)

    def chunk(c0):
        def run():
            proj_s[rows, c0:c0 + PROJ_CHUNK] = jnp.dot(
                hn_s[rows, :], win_ref[:, c0:c0 + PROJ_CHUNK], preferred_element_type=F32)
        return run

    return [norm] + [chunk(c0) for c0 in range(0, D_IN, PROJ_CHUNK)]


def _out_project_thunks(x_ref, z, wout_ref, xo_ref, r0, n_rows):
    rows = slice(r0, r0 + n_rows)

    def chunk(c0):
        def run():
            cols = slice(c0, c0 + OUT_CHUNK)
            xo_ref[0, rows, cols] = x_ref[0, rows, cols] + jnp.dot(
                z[rows, :], wout_ref[:, cols], preferred_element_type=F32)
        return run

    return [chunk(c0) for c0 in range(0, D_MODEL, OUT_CHUNK)]


def _gate_tiles(wa, wx):
    per_tile = GATE_TILE // BLOCK_W
    eye = jnp.eye(per_tile, dtype=wa.dtype)

    def dense(w):
        w5 = w.reshape(N_A, N_GATE_TILES, per_tile, BLOCK_W, BLOCK_W)
        return jnp.einsum("ljicd,ik->ljickd", w5, eye).reshape(N_A, N_GATE_TILES, GATE_TILE, GATE_TILE)

    return jnp.concatenate([dense(wa), dense(wx)], axis=-1).astype(BF16)


def _prepare_weights(norm_g, w_in, w_out, conv_w, conv_b, lru_wa, lru_ba, lru_wx, lru_bx, lru_lambda,
                     kv_norm_g, w_kv, sinks, final_g):
    col = jnp.arange(D_IN)
    is_gate = ((col >= D_MAIN) & (col < 2 * D_MAIN)) | (col >= 2 * D_MAIN + D_MEMW)
    half = HEAD_DIM // 2
    inv = ROPE_THETA ** (-jnp.arange(half, dtype=F32) / half)
    sign = jnp.concatenate([-jnp.ones((half,), F32), jnp.ones((half,), F32)])
    return dict(
        norm_g=norm_g.reshape(DEPTH, 1, D_MODEL),
        w_in=(w_in * jnp.where(is_gate, 0.5, 1.0).astype(F32)).astype(BF16),
        w_out=w_out.astype(BF16),
        conv_w=conv_w, conv_b=conv_b.reshape(N_A, 1, D_MAIN),
        w_gate=_gate_tiles(0.5 * lru_wa, 0.5 * lru_wx),
        ba=0.5 * lru_ba.reshape(N_A, 1, D_MAIN), bx=0.5 * lru_bx.reshape(N_A, 1, D_MAIN),
        lam=lru_lambda.reshape(N_A, 1, D_MAIN),
        sinks=sinks, sinks_col=sinks.reshape(DEPTH - N_A, N_HEADS, 1),
        kv_norm_g=kv_norm_g.reshape(1, D_MODEL), w_kv=w_kv.astype(BF16),
        final_g=final_g.reshape(1, D_MODEL),
        inv128=jnp.tile(inv, LANES // half).reshape(1, LANES),
        sign128=jnp.tile(sign, LANES // HEAD_DIM).reshape(1, LANES))


def _rope_table_kernel(inv_ref, sign_ref, cos_ref, sin_ref):
    rows = cos_ref.shape[0]
    bases = rows // WINDOW
    inv = inv_ref[...]
    offset = lax.broadcasted_iota(jnp.int32, (WINDOW, LANES), 0).astype(F32) * inv
    cos_o, sin_o = jnp.cos(offset), jnp.sin(offset)
    base_pos = pl.program_id(0) * rows + WINDOW * lax.broadcasted_iota(jnp.int32, (bases, LANES), 0)
    base = base_pos.astype(F32) * inv
    cos_b, sin_b = jnp.cos(base), jnp.sin(base)
    for s in range(bases):
        blk = slice(WINDOW * s, WINDOW * (s + 1))
        cb, sb = cos_b[s:s + 1, :], sin_b[s:s + 1, :]
        cos_ref[blk, :] = cb * cos_o - sb * sin_o
        sin_ref[blk, :] = (sb * cos_o + cb * sin_o) * sign_ref[...]


def _rope_tables(seq, inv128, sign128, rows=1024):
    rows = min(rows, seq)
    return pl.pallas_call(
        _rope_table_kernel,
        grid=(seq // rows,),
        in_specs=[_const_spec((1, LANES)), _const_spec((1, LANES))],
        out_specs=[pl.BlockSpec((rows, LANES), lambda i: (i, 0)),
                   pl.BlockSpec((rows, LANES), lambda i: (i, 0))],
        out_shape=[jax.ShapeDtypeStruct((seq, LANES), F32)] * 2,
        name="rope_tables",
    )(inv128, sign128)


def _mem_kv_kernel(m_ref, g_ref, w_ref, k_ref, v_ref, kb_ref, vb_ref):
    mn = _rms(m_ref[0], g_ref[0]).astype(BF16)
    kv = jnp.dot(mn, w_ref[0], preferred_element_type=F32)
    k, v = kv[:, :D_MEMW], kv[:, D_MEMW:]
    k_ref[0, 0] = k
    v_ref[0, 0] = v
    kb_ref[0, 0] = k.astype(BF16)
    vb_ref[0, 0] = v.T.astype(BF16)


def _mem_kv(mem_prompt, mem_norm_g, w_mem_kv_bf16):
    batch = mem_prompt.shape[0]
    out_spec = pl.BlockSpec((1, 1, N_MEM, D_MEMW), lambda l, b: (l, b, 0, 0))
    out_spec_t = pl.BlockSpec((1, 1, D_MEMW, N_MEM), lambda l, b: (l, b, 0, 0))
    return pl.pallas_call(
        _mem_kv_kernel,
        grid=(DEPTH, batch),
        in_specs=[pl.BlockSpec((1, N_MEM, D_MODEL), lambda l, b: (b, 0, 0)),
                  pl.BlockSpec((1, 1, D_MODEL), lambda l, b: (l, 0, 0)),
                  pl.BlockSpec((1, D_MODEL, 2 * D_MEMW), lambda l, b: (l, 0, 0))],
        out_specs=[out_spec, out_spec, out_spec, out_spec_t],
        out_shape=[jax.ShapeDtypeStruct((DEPTH, batch, N_MEM, D_MEMW), F32)] * 2
                  + [jax.ShapeDtypeStruct((DEPTH, batch, N_MEM, D_MEMW), BF16),
                     jax.ShapeDtypeStruct((DEPTH, batch, D_MEMW, N_MEM), BF16)],
        name="mem_kv",
    )(mem_prompt, mem_norm_g.reshape(DEPTH, 1, D_MODEL), w_mem_kv_bf16)


def _prompt_lru_kernel(emit_kv, tile, make_side, *refs):
    n_in = 16 if emit_kv else 12
    (x_ref, ng_ref, win_ref, wout_ref, cw_ref, cb_ref, wg_ref, ba_ref, bx_ref, lam_ref,
     mk_ref, mv_ref) = refs[:12]
    if emit_kv:
        kvg_ref, wkv_ref, cos_ref, sin_ref = refs[12:16]
        xo_ref, hl_ref, ct_ref, k_ref, v_ref = refs[n_in:n_in + 5]
        scratch = refs[n_in + 5:]
    else:
        xo_ref, hl_ref, ct_ref = refs[n_in:n_in + 3]
        scratch = refs[n_in + 3:]
    tails, a3, b3, h3, hc, z, hn_s, proj_s = scratch
    half = tile // 2
    t = pl.program_id(1)
    side_begin, side_thunks, side_end = make_side(dict(
        ng=ng_ref, win=win_ref, wout=wout_ref, cw=cw_ref, cb=cb_ref, wg=wg_ref, ba=ba_ref, bx=bx_ref, lam=lam_ref))

    @pl.when(t == 0)
    def _():
        tails[...] = jnp.zeros((SUBLANES, D_MAIN), F32)
        hc[...] = jnp.zeros((SUBLANES, LANES), F32)

    side_begin()

    decay_c = _decay_exponent(lam_ref[...])
    carry = [hc[...]]
    conv_out = {}


    project = lambda r0: _project_thunks(x_ref, ng_ref, win_ref, hn_s, proj_s, r0, half)

    def gates(r0):
        rows = slice(r0, r0 + half)
        g0, gn = r0 // SUBLANES, half // SUBLANES

        def conv():
            u = proj_s[rows, 0:D_MAIN]
            first_row = lax.broadcasted_iota(jnp.int32, (half, D_MAIN), 0) == 0
            acc = u * cw_ref[0:1, :]
            for k in range(1, CONV_W):
                delayed = jnp.where(first_row, tails[k - 1:k, :], pltpu.roll(acc, 1, 0))
                tails[k - 1:k, :] = acc[half - 1:half, :]
                acc = delayed + u * cw_ref[k:k + 1, :]
            conv_out[r0] = acc + cb_ref[...]

        def gate(j):
            def run():
                sl = slice(GATE_TILE * j, GATE_TILE * (j + 1))
                uc = conv_out[r0][:, sl]
                pre = jnp.dot(uc.astype(BF16), wg_ref[j], preferred_element_type=F32)
                a, b = _rglru_coeffs(uc, pre, ba_ref[:, sl], bx_ref[:, sl], decay_c[:, sl])
                for jj in range(GATE_TILE // LANES):
                    slab = j * (GATE_TILE // LANES) + jj
                    srows = slice(SUBLANES * slab, SUBLANES * (slab + 1))
                    cols = slice(LANES * jj, LANES * (jj + 1))
                    a3[g0:g0 + gn, srows, :] = a[:, cols].reshape(gn, SUBLANES, LANES)
                    b3[g0:g0 + gn, srows, :] = b[:, cols].reshape(gn, SUBLANES, LANES)
            return run

        return [conv] + [gate(j) for j in range(N_GATE_TILES)]

    def scan(r0):
        def piece(first, count):
            def run():
                h = carry[0]
                for g in range(first, first + count):
                    for r in range(SUBLANES):
                        step = pl.ds(r, N_SLABS, stride=SUBLANES)
                        h = a3[g, step, :] * h + b3[g, step, :]
                        h3[g, step, :] = h
                carry[0] = h
            return run

        g0, gn = r0 // SUBLANES, half // SUBLANES
        count = max(gn // SCAN_CHUNKS, 1)
        return [piece(g, count) for g in range(g0, g0 + gn, count)]

    def gating(r0):
        def slab(j):
            def run():
                g0, gn = r0 // SUBLANES, half // SUBLANES
                y = h3[g0:g0 + gn, SUBLANES * j:SUBLANES * (j + 1), :].reshape(half, LANES)
                g_main = proj_s[r0:r0 + half, D_MAIN + LANES * j:D_MAIN + LANES * (j + 1)]
                z[r0:r0 + half, LANES * j:LANES * (j + 1)] = (y * _silu_of_half(g_main)).astype(BF16)
            return run
        return [slab(j) for j in range(N_SLABS)]

    def out_project(r0):
        rows = slice(r0, r0 + half)

        def shared_kv():
            kvn = _rms(xo_ref[0, rows, :], kvg_ref[...]).astype(BF16)
            kv = jnp.dot(kvn, wkv_ref[...], preferred_element_type=F32)
            k_ref[0, rows, :] = _rope(kv[:, :D_KV], cos_ref[rows, :], sin_ref[rows, :])
            v_ref[0, rows, :] = kv[:, D_KV:]

        return _out_project_thunks(x_ref, z, wout_ref, xo_ref, r0, half) + ([shared_kv] if emit_kv else [])

    mem = lambda r0: _skewed(_mem_units(proj_s, mk_ref, mv_ref, z, r0, half), ATTN_SKEW)
    first, second = 0, half
    n_conv_in = 1 + D_MAIN // PROJ_CHUNK
    p_first, p_second = project(first), project(second)
    _run(p_first[:n_conv_in])
    _run(_interleave(p_first[n_conv_in:], gates(first)))
    _run(p_second[:n_conv_in])
    _run(_interleave(p_second[n_conv_in:], gates(second), mem(first), scan(first)))
    _run(gating(first))
    _run(_interleave(out_project(first), mem(second), scan(second), side_thunks))
    _run(gating(second))
    _run(out_project(second))

    h = carry[0]
    hc[...] = h

    @pl.when(t == pl.num_programs(1) - 1)
    def _():
        hl_ref[0] = h
        ct_ref[0] = proj_s[tile - SUBLANES:tile, 0:D_MAIN]

    side_end()


def _prompt_lru_part(x, W, layer, mk, mv_t, kv_w=None, tile=256):
    batch, seq, _ = x.shape
    emit_kv = kv_w is not None
    tok = lambda b, t: (b, t, 0)
    per_b = lambda b, t: (b, 0, 0)
    mem = lambda b, t: (layer, b, 0, 0)
    in_specs = [
        pl.BlockSpec((1, tile, D_MODEL), tok),
        _layer_spec((1, D_MODEL), layer), _layer_spec((D_MODEL, D_IN), layer),
        _layer_spec((D_CAT, D_MODEL), layer),
        _layer_spec((CONV_W, D_MAIN), layer), _layer_spec((1, D_MAIN), layer),
        _layer_spec((N_GATE_TILES, GATE_TILE, 2 * GATE_TILE), layer),
        _layer_spec((1, D_MAIN), layer), _layer_spec((1, D_MAIN), layer), _layer_spec((1, D_MAIN), layer),
        pl.BlockSpec((None, 1, N_MEM, D_MEMW), mem), pl.BlockSpec((None, 1, D_MEMW, N_MEM), mem),
    ]
    args = [x, W["norm_g"], W["w_in"], W["w_out"], W["conv_w"], W["conv_b"], W["w_gate"],
            W["ba"], W["bx"], W["lam"], mk, mv_t]
    out_specs = [pl.BlockSpec((1, tile, D_MODEL), tok),
                 pl.BlockSpec((1, SUBLANES, LANES), per_b),
                 pl.BlockSpec((1, SUBLANES, D_MAIN), per_b)]
    out_shape = [jax.ShapeDtypeStruct((batch, seq, D_MODEL), F32),
                 jax.ShapeDtypeStruct((batch, SUBLANES, LANES), F32),
                 jax.ShapeDtypeStruct((batch, SUBLANES, D_MAIN), F32)]
    if emit_kv:
        in_specs += [_const_spec((1, D_MODEL)), _const_spec((D_MODEL, 2 * D_KV)),
                     pl.BlockSpec((tile, LANES), lambda b, t: (t, 0)),
                     pl.BlockSpec((tile, LANES), lambda b, t: (t, 0))]
        args += [kv_w["kv_norm_g"], kv_w["w_kv"], kv_w["cos"], kv_w["sin"]]
        out_specs += [pl.BlockSpec((1, tile, D_KV), tok)] * 2
        out_shape += [jax.ShapeDtypeStruct((batch, seq, D_KV), F32)] * 2
    groups = tile // SUBLANES
    return dict(
        body=functools.partial(_prompt_lru_kernel, emit_kv, tile), grid=(batch, seq // tile),
        in_specs=in_specs, args=args, out_specs=out_specs, out_shape=out_shape,
        scratch=[pltpu.VMEM((SUBLANES, D_MAIN), F32),
                 pltpu.VMEM((groups, SUBLANES * N_SLABS, LANES), F32),
                 pltpu.VMEM((groups, SUBLANES * N_SLABS, LANES), F32),
                 pltpu.VMEM((groups, SUBLANES * N_SLABS, LANES), F32),
                 pltpu.VMEM((SUBLANES, LANES), F32),
                 pltpu.VMEM((tile, D_CAT), BF16),
                 pltpu.VMEM((tile, D_MODEL), BF16),
                 pltpu.VMEM((tile, D_IN), F32)],
        name="lru_kv" if emit_kv else "lru")


def _sink_softmax(s, sink, axis=-1):
    m = jnp.maximum(jnp.max(s, axis=axis, keepdims=True), sink)
    p = jnp.exp2(s - m)
    den = jnp.sum(p, axis=axis, keepdims=True) + jnp.exp2(sink - m)
    return p, den


def _prompt_swa_kernel(final, tile, sink_row, make_side, *refs):
    n_in = 14 if final else 13
    (x_ref, ng_ref, win_ref, wout_ref, mk_ref, mv_ref, sink_ref, kp_ref, kc_ref, vp_ref, vc_ref,
     cos_ref, sin_ref) = refs[:13]
    if final:
        fg_ref = refs[13]
        xo_ref, y_ref = refs[n_in:n_in + 2]
        scratch = refs[n_in + 2:]
    else:
        fg_ref = None
        xo_ref, = refs[n_in:n_in + 1]
        scratch = refs[n_in + 1:]
    z, qz, kcat, vcat_t, hn_s, proj_s = scratch
    half = tile // 2
    t = pl.program_id(1)
    side_begin, side_thunks, side_end = make_side(dict(ng=ng_ref, win=win_ref, wout=wout_ref, fg=fg_ref))
    side_begin()

    def rope_queries(r0):
        rows = slice(r0, r0 + half)

        def slab(j):
            def run():
                cos = cos_ref[rows, :]
                sin = sin_ref[rows, :]
                lane = lax.broadcasted_iota(jnp.int32, (half, LANES), 1)
                lo = lane < HEAD_DIM
                first_half = (lane % HEAD_DIM) < (HEAD_DIM // 2)
                c = j // (GROUP // 2)
                xq = proj_s[rows, LANES * j:LANES * (j + 1)] * (HEAD_DIM ** -0.5 * LOG2_E)
                r32 = pltpu.roll(xq, HEAD_DIM // 2, 1)
                r96 = pltpu.roll(xq, LANES - HEAD_DIM // 2, 1)
                r64 = pltpu.roll(xq, HEAD_DIM, 1)
                stay = xq * cos + jnp.where(first_half, r96, r32) * sin
                move = r64 * cos + jnp.where(first_half, r32, r96) * sin
                keep = lo if c == 0 else ~lo
                even, odd = (stay, move) if c == 0 else (move, stay)
                qz[rows, 2 * LANES * j:2 * LANES * j + LANES] = jnp.where(keep, even, 0.0).astype(BF16)
                qz[rows, 2 * LANES * j + LANES:2 * LANES * (j + 1)] = jnp.where(keep, odd, 0.0).astype(BF16)
            return run

        return [slab(j) for j in range(N_SLABS)]

    kcat[0:WINDOW, :] = kp_ref[0].astype(BF16)
    kcat[WINDOW:, :] = kc_ref[0].astype(BF16)
    vcat_t[:, 0:WINDOW] = vp_ref[0].T.astype(BF16)
    vcat_t[:, WINDOW:] = vc_ref[0].T.astype(BF16)

    key_r = lax.broadcasted_iota(jnp.int32, (WINDOW, 2 * WINDOW), 0)
    query = lax.broadcasted_iota(jnp.int32, (WINDOW, 2 * WINDOW), 1) % WINDOW
    from_prev = key_r > query

    def scores(n, j):
        rows = slice(WINDOW * n, WINDOW * (n + 1))
        keys = kcat[WINDOW * n:WINDOW * (n + 2), :]
        q2 = jnp.concatenate([qz[rows, 2 * LANES * j:2 * LANES * j + LANES],
                              qz[rows, 2 * LANES * j + LANES:2 * LANES * (j + 1)]], axis=0)
        s = lax.dot_general(keys, q2, _NT, preferred_element_type=F32)
        s_prev, s_own = s[:WINDOW, :], s[WINDOW:, :]
        if n == 0:
            s_prev = jnp.where(t > 0, s_prev, NEG)
        visible = jnp.where(from_prev, s_prev, s_own)
        p0, den0 = _sink_softmax(visible[:, :WINDOW], sink_ref[sink_row, 2 * j] * LOG2_E, axis=0)
        p1, den1 = _sink_softmax(visible[:, WINDOW:], sink_ref[sink_row, 2 * j + 1] * LOG2_E, axis=0)
        p = jnp.concatenate([p0, p1], axis=1).astype(BF16)
        zero = jnp.zeros_like(p)
        return jnp.concatenate([jnp.where(from_prev, p, zero), jnp.where(from_prev, zero, p)], axis=0), den0, den1

    def outputs(n, j, p, den0, den1):
        rows = slice(WINDOW * n, WINDOW * (n + 1))
        c = j // (GROUP // 2)
        vals_t = vcat_t[HEAD_DIM * c:HEAD_DIM * (c + 1), WINDOW * n:WINDOW * (n + 2)]
        o_t = jnp.dot(vals_t, p, preferred_element_type=F32)
        y_t = jnp.concatenate([o_t[:, :WINDOW] / den0, o_t[:, WINDOW:] / den1], axis=0)
        g_main = proj_s[rows, D_MAIN + LANES * j:D_MAIN + LANES * (j + 1)]
        z[rows, LANES * j:LANES * (j + 1)] = (y_t.T * _silu_of_half(g_main)).astype(BF16)

    def attend(r0):
        units = [(functools.partial(scores, n, j), functools.partial(outputs, n, j))
                 for n in range(r0 // WINDOW, (r0 + half) // WINDOW) for j in range(N_SLABS)]
        return _skewed(units + _mem_units(proj_s, mk_ref, mv_ref, z, r0, half), ATTN_SKEW)

    def out_project(r0):
        rows = slice(r0, r0 + half)

        def final_norm():
            y_ref[0, rows, :] = _rms(xo_ref[0, rows, :], fg_ref[...])

        return _out_project_thunks(x_ref, z, wout_ref, xo_ref, r0, half) + ([final_norm] if final else [])

    project = lambda r0: _project_thunks(x_ref, ng_ref, win_ref, hn_s, proj_s, r0, half)
    first, second = 0, half
    _run(project(first))
    _run(_interleave(project(second), rope_queries(first)))
    _run(_interleave(attend(first), rope_queries(second)))
    _run(_interleave(attend(second), out_project(first), side_thunks))
    _run(out_project(second))
    side_end()


def _prompt_swa_part(x, W, layer, mk, mv_t, k, v, cos, sin, final_g=None, tile=256):
    batch, seq, _ = x.shape
    final = final_g is not None
    blocks = tile // WINDOW
    tok = lambda b, t: (b, t, 0)
    mem = lambda b, t: (layer, b, 0, 0)
    prev_blk = lambda b, t: (b, jnp.maximum(t * blocks - 1, 0), 0)
    in_specs = [
        pl.BlockSpec((1, tile, D_MODEL), tok),
        _layer_spec((1, D_MODEL), layer), _layer_spec((D_MODEL, D_IN), layer),
        _layer_spec((D_CAT, D_MODEL), layer),
        pl.BlockSpec((None, 1, N_MEM, D_MEMW), mem), pl.BlockSpec((None, 1, D_MEMW, N_MEM), mem),
        pl.BlockSpec(memory_space=pltpu.SMEM),
        pl.BlockSpec((1, WINDOW, D_KV), prev_blk), pl.BlockSpec((1, tile, D_KV), tok),
        pl.BlockSpec((1, WINDOW, D_KV), prev_blk), pl.BlockSpec((1, tile, D_KV), tok),
        pl.BlockSpec((tile, LANES), lambda b, t: (t, 0)), pl.BlockSpec((tile, LANES), lambda b, t: (t, 0)),
    ]
    args = [x, W["norm_g"], W["w_in"], W["w_out"], mk, mv_t, W["sinks"], k, k, v, v, cos, sin]
    out_specs = [pl.BlockSpec((1, tile, D_MODEL), tok)]
    out_shape = [jax.ShapeDtypeStruct((batch, seq, D_MODEL), F32)]
    if final:
        in_specs.append(_const_spec((1, D_MODEL)))
        args.append(final_g)
        out_specs.append(pl.BlockSpec((1, tile, D_MODEL), tok))
        out_shape.append(jax.ShapeDtypeStruct((batch, seq, D_MODEL), F32))
    return dict(
        body=functools.partial(_prompt_swa_kernel, final, tile, layer - N_A), grid=(batch, seq // tile),
        in_specs=in_specs, args=args, out_specs=out_specs, out_shape=out_shape,
        scratch=[pltpu.VMEM((tile, D_CAT), BF16),
                 pltpu.VMEM((tile, 2 * D_MAIN), BF16),
                 pltpu.VMEM((WINDOW + tile, D_KV), BF16),
                 pltpu.VMEM((D_KV, WINDOW + tile), BF16),
                 pltpu.VMEM((tile, D_MODEL), BF16),
                 pltpu.VMEM((tile, D_IN), F32)],
        name="swa_final" if final else "swa")


def _sample_mem_unit(b, bb, qm_s, mk_ref, mv_ref, om_s):
    def scores():
        row = lax.broadcasted_iota(jnp.int32, (SUBLANES, MEM_HD), 0)
        q = qm_s[pl.ds(b, 1), :]
        qh = jnp.zeros((SUBLANES, MEM_HD), F32)
        for hh in range(MEM_HEADS):
            q_head = jnp.broadcast_to(q[:, MEM_HD * hh:MEM_HD * (hh + 1)], (SUBLANES, MEM_HD))
            qh = jnp.where(row == hh, q_head, qh)
        s = lax.dot_general(qh.astype(BF16), mk_ref[bb].astype(BF16), _NT,
                            preferred_element_type=F32) * (MEM_HD ** -0.5 * LOG2_E)
        srow = lax.broadcasted_iota(jnp.int32, s.shape, 0)
        scol = lax.broadcasted_iota(jnp.int32, s.shape, 1)
        s = jnp.where((scol % MEM_HEADS) == srow, s, NEG)
        m = jnp.max(s, axis=-1, keepdims=True)
        p = jnp.exp2(s - m)
        return p.astype(BF16), jnp.sum(p, axis=-1, keepdims=True)

    def outputs(p, den):
        o = jnp.dot(p, mv_ref[bb].astype(BF16), preferred_element_type=F32) / den
        om_s[pl.ds(b, 1), :] = jnp.concatenate([o[hh:hh + 1, :] for hh in range(MEM_HEADS)], axis=1)

    return scores, outputs


def _sample_lru_side(seqs, step, n_steps, w, x_ref, h0_ref, c0_ref, mk_ref, mv_ref, xo_ref, ho_ref, co_ref,
                     z_s, qm_s, gm_s, om_s):
    def begin():
        @pl.when(step == 0)
        def _():
            hn = _rms(x_ref[...], w["ng"][...]).astype(BF16)
            proj = jnp.dot(hn, w["win"][...], preferred_element_type=F32)
            u = proj[:, :D_MAIN]
            taps = [c0_ref[k] for k in range(CONV_W - 1)] + [u]
            uc = w["cb"][...] + taps[0] * w["cw"][0:1, :]
            for k in range(1, CONV_W):
                uc = uc + taps[k] * w["cw"][k:k + 1, :]
            for k in range(CONV_W - 1):
                co_ref[k] = taps[k + 1]
            decay_c = _decay_exponent(w["lam"][...])
            ucb = uc.astype(BF16)
            for j in range(N_GATE_TILES):
                sl = slice(GATE_TILE * j, GATE_TILE * (j + 1))
                pre = jnp.dot(ucb[:, sl], w["wg"][j], preferred_element_type=F32)
                a, b = _rglru_coeffs(uc[:, sl], pre, w["ba"][:, sl], w["bx"][:, sl], decay_c[:, sl])
                h = a * h0_ref[:, sl] + b
                ho_ref[:, sl] = h
                z_s[:, sl] = (h * _silu_of_half(
                    proj[:, D_MAIN + GATE_TILE * j:D_MAIN + GATE_TILE * (j + 1)])).astype(BF16)
            qm_s[...] = proj[:, 2 * D_MAIN:2 * D_MAIN + D_MEMW]
            gm_s[...] = _silu_of_half(proj[:, 2 * D_MAIN + D_MEMW:])

    thunks = _skewed([_sample_mem_unit(step * seqs + bb, bb, qm_s, mk_ref, mv_ref, om_s) for bb in range(seqs)],
                     ATTN_SKEW)

    def end():
        @pl.when(step == n_steps - 1)
        def _():
            z_s[:, D_MAIN:] = (om_s[...] * gm_s[...]).astype(BF16)
            xo_ref[...] = x_ref[...] + jnp.dot(z_s[...], w["wout"][...], preferred_element_type=F32)

    return begin, thunks, end


def _cache_spec(layer, seqs, steps_per_row):
    return pl.BlockSpec((None, seqs, N_MEM * MEM_HEADS, MEM_HD),
                        lambda b, t: (layer, b * steps_per_row + t, 0, 0))


def _whole_spec(shape):
    zeros = (0,) * len(shape)
    return pl.BlockSpec(shape, lambda *_: zeros)


def _sample_lru_part(x, layer, state_h, state_conv_t, cache_k, cache_v, grid):
    batch = x.shape[0]
    seqs = batch // (grid[0] * grid[1])
    cache_spec = _cache_spec(layer, seqs, grid[1])
    return dict(
        body=functools.partial(_sample_lru_side, seqs),
        in_specs=[_const_spec((batch, D_MODEL)),
                  _layer_spec((batch, D_MAIN), layer), _layer_spec((CONV_W - 1, batch, D_MAIN), layer),
                  cache_spec, cache_spec],
        args=[x, state_h, state_conv_t, cache_k, cache_v],
        out_specs=[_whole_spec((batch, D_MODEL)), _whole_spec((batch, D_MAIN)),
                   _whole_spec((CONV_W - 1, batch, D_MAIN))],
        out_shape=[jax.ShapeDtypeStruct((batch, D_MODEL), F32),
                   jax.ShapeDtypeStruct((batch, D_MAIN), F32),
                   jax.ShapeDtypeStruct((CONV_W - 1, batch, D_MAIN), F32)],
        scratch=[pltpu.VMEM((batch, D_CAT), BF16), pltpu.VMEM((batch, D_MEMW), F32),
                 pltpu.VMEM((batch, D_MEMW), F32), pltpu.VMEM((batch, D_MEMW), F32)])


def _sample_swa_side(first, final, seqs, step, n_steps, w, *refs):
    (x_ref, sink_ref, mk_ref, mv_ref, wk_ref, wv_ref) = refs[:6]
    pos = 6
    if first:
        kvg_ref, wkv_ref = refs[pos:pos + 2]
        pos += 2
    inv_ref, sign_ref = refs[pos:pos + 2]
    pos += 2
    xo_ref = refs[pos]
    pos += 1
    if first:
        wko_ref, wvo_ref = refs[pos:pos + 2]
        pos += 2
    if final:
        y_ref = refs[pos]
        pos += 1
    z_s, q_s, gmain_s, y_s, qm_s, gm_s, om_s, kn_s, vn_s = refs[pos:]

    def begin():
        @pl.when(step == 0)
        def _():
            x = x_ref[...]
            hn = _rms(x, w["ng"][...]).astype(BF16)
            proj = jnp.dot(hn, w["win"][...], preferred_element_type=F32)
            ang = float(PAST_LEN) * inv_ref[...]
            cos = jnp.cos(ang)
            sin = jnp.sin(ang) * sign_ref[...]
            for j in range(N_SLABS):
                cols = slice(LANES * j, LANES * (j + 1))
                q_s[:, cols] = _rope(proj[:, cols], cos, sin)
            gmain_s[...] = _silu_of_half(proj[:, D_MAIN:2 * D_MAIN])
            qm_s[...] = proj[:, 2 * D_MAIN:2 * D_MAIN + D_MEMW]
            gm_s[...] = _silu_of_half(proj[:, 2 * D_MAIN + D_MEMW:])
            if first:
                kvn = _rms(x, kvg_ref[...]).astype(BF16)
                kv = jnp.dot(kvn, wkv_ref[...], preferred_element_type=F32)
                kn_s[...] = _rope(kv[:, :D_KV], cos, sin).T
                vn_s[...] = kv[:, D_KV:].T

    row = lax.broadcasted_iota(jnp.int32, (N_HEADS, LANES), 0)
    lane = lax.broadcasted_iota(jnp.int32, (N_HEADS, LANES), 1)
    kv_half = (lane // HEAD_DIM) == (row // GROUP)
    in_place = (row % 2) == (row // GROUP)
    lane1 = lax.broadcasted_iota(jnp.int32, (1, LANES), 1)
    seq_lane = lax.broadcasted_iota(jnp.int32, kn_s.shape, 1)
    slot = lax.broadcasted_iota(jnp.int32, (D_KV, WINDOW), 1)

    def slide(win_t, new_t_s, b):
        if new_t_s.shape[1] == WINDOW:
            col = pltpu.roll(new_t_s[...], WINDOW - 1 - b, 1)
        else:
            col = jnp.sum(jnp.where(seq_lane == b, new_t_s[...], 0.0), axis=1, keepdims=True)
        return jnp.where(slot == WINDOW - 1, col, pltpu.roll(win_t, WINDOW - 1, 1))

    def window_unit(b, bb):
        def scores():
            if first:
                kw_t = slide(wk_ref[bb], kn_s, b)
                wko_ref[bb] = kw_t
            else:
                kw_t = wk_ref[bb]
            q = jnp.broadcast_to(q_s[pl.ds(b, 1), :], (N_HEADS, D_MAIN))
            e = jnp.zeros((N_HEADS, LANES), F32)
            for j in range(N_SLABS):
                e = jnp.where((row // 2) == j, q[:, LANES * j:LANES * (j + 1)], e)
            qh = jnp.where(kv_half, jnp.where(in_place, e, pltpu.roll(e, HEAD_DIM, 1)), 0.0).astype(BF16)
            s = jnp.dot(qh, kw_t.astype(BF16), preferred_element_type=F32) * (HEAD_DIM ** -0.5 * LOG2_E)
            p, den = _sink_softmax(s, sink_ref[...] * LOG2_E)
            return p.astype(BF16), den

        def outputs(p, den):
            if first:
                vw_t = slide(wv_ref[bb], vn_s, b)
                wvo_ref[bb] = vw_t
            else:
                vw_t = wv_ref[bb]
            o = lax.dot_general(p, vw_t.astype(BF16), _NT, preferred_element_type=F32) / den
            f = jnp.where(in_place, o, pltpu.roll(o, HEAD_DIM, 1))
            y_s[pl.ds(b, 1), :] = jnp.concatenate(
                [jnp.where(lane1 < HEAD_DIM, f[2 * j:2 * j + 1, :], f[2 * j + 1:2 * j + 2, :])
                 for j in range(N_SLABS)], axis=1)

        return scores, outputs

    units = []
    for bb in range(seqs):
        units.append(window_unit(step * seqs + bb, bb))
        units.append(_sample_mem_unit(step * seqs + bb, bb, qm_s, mk_ref, mv_ref, om_s))

    def end():
        @pl.when(step == n_steps - 1)
        def _():
            z_s[:, :D_MAIN] = (y_s[...] * gmain_s[...]).astype(BF16)
            z_s[:, D_MAIN:] = (om_s[...] * gm_s[...]).astype(BF16)
            xn = x_ref[...] + jnp.dot(z_s[...], w["wout"][...], preferred_element_type=F32)
            xo_ref[...] = xn
            if final:
                y_ref[...] = _rms(xn, w["fg"][...])

    return begin, _skewed(units, ATTN_SKEW), end


def _sample_swa_part(x, W, layer, cache_k, cache_v, win_k_t, win_v_t, first, final, grid):
    batch = x.shape[0]
    seqs = batch // (grid[0] * grid[1])
    cache_spec = _cache_spec(layer, seqs, grid[1])
    win_spec = pl.BlockSpec((seqs, D_KV, WINDOW), lambda b, t: (b * grid[1] + t, 0, 0))
    in_specs = [_const_spec((batch, D_MODEL)), _layer_spec((N_HEADS, 1), layer - N_A),
                cache_spec, cache_spec, win_spec, win_spec]
    args = [x, W["sinks_col"], cache_k, cache_v, win_k_t, win_v_t]
    if first:
        in_specs += [_const_spec((1, D_MODEL)), _const_spec((D_MODEL, 2 * D_KV))]
        args += [W["kv_norm_g"], W["w_kv"]]
    in_specs += [_const_spec((1, LANES)), _const_spec((1, LANES))]
    args += [W["inv128"], W["sign128"]]
    out_specs = [_whole_spec((batch, D_MODEL))]
    out_shape = [jax.ShapeDtypeStruct((batch, D_MODEL), F32)]
    if first:
        out_specs += [win_spec, win_spec]
        out_shape += [jax.ShapeDtypeStruct((batch, D_KV, WINDOW), F32)] * 2
    if final:
        out_specs.append(_whole_spec((batch, D_MODEL)))
        out_shape.append(jax.ShapeDtypeStruct((batch, D_MODEL), F32))
    return dict(
        body=functools.partial(_sample_swa_side, first, final, seqs),
        in_specs=in_specs, args=args, out_specs=out_specs, out_shape=out_shape,
        scratch=[pltpu.VMEM((batch, D_CAT), BF16), pltpu.VMEM((batch, D_MAIN), F32),
                 pltpu.VMEM((batch, D_MAIN), F32), pltpu.VMEM((batch, D_MAIN), F32),
                 pltpu.VMEM((batch, D_MEMW), F32), pltpu.VMEM((batch, D_MEMW), F32),
                 pltpu.VMEM((batch, D_MEMW), F32),
                 pltpu.VMEM((D_KV, batch), F32), pltpu.VMEM((D_KV, batch), F32)])


def _layer_kernel(prompt_body, sample_body, n_prompt, n_sample, *refs):
    it = iter(refs)
    take = lambda n: [next(it) for _ in range(n)]
    p_in, s_in = take(n_prompt[0]), take(n_sample[0])
    p_out, s_out = take(n_prompt[1]), take(n_sample[1])
    p_scr, s_scr = take(n_prompt[2]), take(n_sample[2])
    step = pl.program_id(0) * pl.num_programs(1) + pl.program_id(1)
    n_steps = pl.num_programs(0) * pl.num_programs(1)
    make_side = lambda weights: sample_body(step, n_steps, weights, *s_in, *s_out, *s_scr)
    prompt_body(make_side, *p_in, *p_out, *p_scr)


def _layer_call(prompt, sample):
    counts = lambda part: (len(part["in_specs"]), len(part["out_specs"]), len(part["scratch"]))
    outs = pl.pallas_call(
        functools.partial(_layer_kernel, prompt["body"], sample["body"], counts(prompt), counts(sample)),
        grid=prompt["grid"],
        in_specs=prompt["in_specs"] + sample["in_specs"],
        out_specs=prompt["out_specs"] + sample["out_specs"],
        out_shape=prompt["out_shape"] + sample["out_shape"],
        scratch_shapes=prompt["scratch"] + sample["scratch"],
        compiler_params=pltpu.CompilerParams(
            dimension_semantics=("arbitrary", "arbitrary"), vmem_limit_bytes=VMEM_LIMIT_BYTES),
        name="layer_" + prompt["name"],
    )(*prompt["args"], *sample["args"])
    n = len(prompt["out_specs"])
    return outs[:n], outs[n:]


def _trunks(xp, xs, W, mkb, mvb_t, cache_mem_k, cache_mem_v, state_h, state_conv, win_k, win_v, tile=512):
    pb, seq, _ = xp.shape
    sb = xs.shape[0]
    grid = (pb, seq // tile)
    cos, sin = _rope_tables(seq, W["inv128"], W["sign128"])
    kv_w = dict(kv_norm_g=W["kv_norm_g"], w_kv=W["w_kv"], cos=cos, sin=sin)
    xs = xs.reshape(sb, D_MODEL)
    ck = cache_mem_k.reshape(DEPTH, sb, N_MEM * MEM_HEADS, MEM_HD)
    cv = cache_mem_v.reshape(DEPTH, sb, N_MEM * MEM_HEADS, MEM_HD)
    conv_t = jnp.transpose(state_conv, (0, 2, 1, 3))
    wk = jnp.transpose(win_k, (0, 2, 3, 1)).reshape(sb, D_KV, WINDOW)
    wv = jnp.transpose(win_v, (0, 2, 3, 1)).reshape(sb, D_KV, WINDOW)
    hp, cp, hs, cs = [], [], [], []
    k = v = None
    for l in range(N_A):
        p_out, s_out = _layer_call(
            _prompt_lru_part(xp, W, l, mkb, mvb_t, kv_w=kv_w if l == N_A - 1 else None, tile=tile),
            _sample_lru_part(xs, l, state_h, conv_t, ck, cv, grid))
        xp, h_last, conv_tail = p_out[:3]
        if l == N_A - 1:
            k, v = p_out[3:]
        hp.append(h_last.reshape(pb, D_MAIN))
        cp.append(conv_tail[:, SUBLANES - (CONV_W - 1):, :])
        xs, h, c = s_out
        hs.append(h)
        cs.append(c)
    yp = ys = None
    for l in range(N_A, DEPTH):
        first, last = l == N_A, l == DEPTH - 1
        p_out, s_out = _layer_call(
            _prompt_swa_part(xp, W, l, mkb, mvb_t, k, v, cos, sin, final_g=W["final_g"] if last else None, tile=tile),
            _sample_swa_part(xs, W, l, ck, cv, wk, wv, first, last, grid))
        xp, xs = p_out[0], s_out[0]
        if first:
            wk, wv = s_out[1], s_out[2]
        if last:
            yp, ys = p_out[1], s_out[-1]
    win_kp = k[:, seq - WINDOW:, :].reshape(pb, WINDOW, N_KV, HEAD_DIM)
    win_vp = v[:, seq - WINDOW:, :].reshape(pb, WINDOW, N_KV, HEAD_DIM)
    unslide = lambda w_t: jnp.transpose(w_t.reshape(sb, N_KV, HEAD_DIM, WINDOW), (0, 3, 1, 2))
    prompt_out = (yp, jnp.stack(hp), jnp.stack(cp), win_kp, win_vp)
    sample_out = (ys.reshape(sb, 1, D_MODEL), jnp.stack(hs), jnp.transpose(jnp.stack(cs), (0, 2, 1, 3)),
                  unslide(wk), unslide(wv))
    return prompt_out, sample_out


def kernel(x_prompt, x_sample, cache_mem_k, cache_mem_v, state_lru_h, state_conv, cache_win_k, cache_win_v, mem_prompt, norm_g, w_in, w_out, mem_norm_g, w_mem_kv, conv_w, conv_b, lru_wa, lru_ba, lru_wx, lru_bx, lru_lambda, kv_norm_g, w_kv, sinks, final_g):
    W = _prepare_weights(norm_g, w_in, w_out, conv_w, conv_b, lru_wa, lru_ba, lru_wx, lru_bx, lru_lambda, kv_norm_g, w_kv, sinks, final_g)
    mk, mv, mkb, mvb = _mem_kv(mem_prompt, mem_norm_g, w_mem_kv.astype(BF16))
    (y_p, h_p, conv_p, wk_p, wv_p), (y_s, h_s, conv_s, wk_s, wv_s) = _trunks(
        x_prompt, x_sample, W, mkb, mvb, cache_mem_k, cache_mem_v, state_lru_h, state_conv,
        cache_win_k, cache_win_v)
    batch = x_prompt.shape[0]
    mk_p = mk.reshape(DEPTH, batch, N_MEM, MEM_HEADS, MEM_HD)
    mv_p = mv.reshape(DEPTH, batch, N_MEM, MEM_HEADS, MEM_HD)
    return (y_p, y_s, mk_p, mv_p, h_p, conv_p, wk_p, wv_p, h_s, conv_s, wk_s, wv_s)
```

```python
import functools

import jax
import jax.numpy as jnp
from jax import lax
from jax.experimental import pallas as pl
from jax.experimental.pallas import tpu as pltpu

F32 = jnp.float32
BF16 = jnp.bfloat16

D_MODEL = 1024
DEPTH = 4
N_A = DEPTH // 2
N_HEADS = 16
HEAD_DIM = 64
N_KV = 2
GROUP = N_HEADS // N_KV
D_MAIN = N_HEADS * HEAD_DIM
N_BLOCKS = 16
BLOCK_W = D_MAIN // N_BLOCKS
CONV_W = 4
LRU_C = 8.0
WINDOW = 128
ROPE_THETA = 10000.0
N_MEM = 256
MEM_HEADS = 4
MEM_HD = 128
D_MEMW = MEM_HEADS * MEM_HD
D_IN = 2 * D_MAIN + 2 * D_MEMW
D_CAT = D_MAIN + D_MEMW
D_KV = N_KV * HEAD_DIM
EPS = 1e-6
NEG = -1e30
LOG2_E = 1.4426950408889634
PAST_LEN = 8192

SUBLANES = 8
LANES = 128
N_SLABS = D_MAIN // LANES
GATE_TILE = 256
N_GATE_TILES = D_MAIN // GATE_TILE
VMEM_LIMIT_BYTES = 56 * 1024 * 1024
ATTN_SKEW = 4
MEM_CHUNK = 256
SCAN_CHUNKS = 16
PROJ_CHUNK = 512
OUT_CHUNK = 256

_NT = (((1,), (1,)), ((), ()))


def _const_spec(shape):
    zeros = (0,) * len(shape)
    return pl.BlockSpec(shape, lambda *_: zeros, pipeline_mode=pl.Buffered(1))


def _layer_spec(shape, layer):
    zeros = (0,) * len(shape)
    return pl.BlockSpec((None,) + tuple(shape), lambda *_: (layer,) + zeros, pipeline_mode=pl.Buffered(1))


def _rms(x, g):
    return x * lax.rsqrt(jnp.mean(x * x, axis=-1, keepdims=True) + EPS) * g


def _silu_of_half(hx):
    return hx * jnp.tanh(hx) + hx


def _softplus(x):
    return jnp.maximum(x, 0.0) + jnp.log1p(jnp.exp(-jnp.abs(x)))


def _decay_exponent(lam):
    return (-0.5 * LRU_C * LOG2_E) * _softplus(-lam)


def _rope(x, cos, sin_signed):
    lane = lax.broadcasted_iota(jnp.int32, x.shape, 1)
    first_half = (lane % HEAD_DIM) < (HEAD_DIM // 2)
    swapped = jnp.where(first_half,
                        pltpu.roll(x, LANES - HEAD_DIM // 2, 1),
                        pltpu.roll(x, HEAD_DIM // 2, 1))
    return x * cos + swapped * sin_signed


def _rglru_coeffs(uc, half_pre, half_ba, half_bx, c):
    half = half_pre.shape[-1] // 2
    t_r = jnp.tanh(half_pre[:, :half] + half_ba)
    t_i = jnp.tanh(half_pre[:, half:] + half_bx)
    a = jnp.exp2(c * t_r + c)
    w = 1.0 - a * a
    root = jnp.where(w > 0.0, w * lax.rsqrt(w), 0.0)
    return a, root * ((0.5 * t_i + 0.5) * uc)


def _mem_units(proj, mk_ref, mv_ref, z, row0, n_rows):
    chunk = min(n_rows, MEM_CHUNK)
    units = []
    for ch in range(n_rows // chunk):
        for hh in range(MEM_HEADS):
            rows = slice(row0 + chunk * ch, row0 + chunk * (ch + 1))
            cols = slice(MEM_HD * hh, MEM_HD * (hh + 1))
            q_cols = slice(2 * D_MAIN + MEM_HD * hh, 2 * D_MAIN + MEM_HD * (hh + 1))
            g_cols = slice(2 * D_MAIN + D_MEMW + MEM_HD * hh, 2 * D_MAIN + D_MEMW + MEM_HD * (hh + 1))
            z_cols = slice(D_MAIN + MEM_HD * hh, D_MAIN + MEM_HD * (hh + 1))

            def scores(rows=rows, cols=cols, q_cols=q_cols):
                q = proj[rows, q_cols].astype(BF16)
                s = lax.dot_general(mk_ref[0, :, cols], q, _NT,
                                    preferred_element_type=F32) * (MEM_HD ** -0.5 * LOG2_E)
                m = jnp.max(s, axis=0, keepdims=True)
                p = jnp.exp2(s - m)
                return p.astype(BF16), jnp.sum(p, axis=0, keepdims=True)

            def outputs(p, den, rows=rows, cols=cols, g_cols=g_cols, z_cols=z_cols):
                o_t = jnp.dot(mv_ref[0, cols, :], p, preferred_element_type=F32)
                z[rows, z_cols] = ((o_t / den).T * _silu_of_half(proj[rows, g_cols])).astype(BF16)

            units.append((scores, outputs))
    return units


def _skewed(units, skew):
    results = []
    first = lambda scores: lambda: results.append(scores())
    second = lambda outputs: lambda: outputs(*results.pop(0))
    thunks, pending = [], []
    for scores, outputs in units:
        thunks.append(first(scores))
        pending.append(second(outputs))
        if len(pending) > skew:
            thunks.append(pending.pop(0))
    return thunks + pending


def _interleave(*lists):
    lists = [l for l in lists if l]
    pos = [0] * len(lists)
    merged = []
    for _ in range(sum(len(l) for l in lists)):
        i = min((i for i in range(len(lists)) if pos[i] < len(lists[i])),
                key=lambda i: pos[i] / len(lists[i]))
        merged.append(lists[i][pos[i]])
        pos[i] += 1
    return merged


def _run(thunks):
    for thunk in thunks:
        thunk()


def _project_thunks(x_ref, ng_ref, win_ref, hn_s, proj_s, r0, n_rows):
    rows = slice(r0, r0 + n_rows)

    def norm():
        hn_s[rows, :] = _rms(x_ref[0, rows, :], ng_ref[...]).astype(BF16)

    def chunk(c0):
        def run():
            proj_s[rows, c0:c0 + PROJ_CHUNK] = jnp.dot(
                hn_s[rows, :], win_ref[:, c0:c0 + PROJ_CHUNK], preferred_element_type=F32)
        return run

    return [norm] + [chunk(c0) for c0 in range(0, D_IN, PROJ_CHUNK)]


def _out_project_thunks(x_ref, z, wout_ref, xo_ref, r0, n_rows):
    rows = slice(r0, r0 + n_rows)

    def chunk(c0):
        def run():
            cols = slice(c0, c0 + OUT_CHUNK)
            xo_ref[0, rows, cols] = x_ref[0, rows, cols] + jnp.dot(
                z[rows, :], wout_ref[:, cols], preferred_element_type=F32)
        return run

    return [chunk(c0) for c0 in range(0, D_MODEL, OUT_CHUNK)]


def _gate_tiles(wa, wx):
    per_tile = GATE_TILE // BLOCK_W
    eye = jnp.eye(per_tile, dtype=wa.dtype)

    def dense(w):
        w5 = w.reshape(N_A, N_GATE_TILES, per_tile, BLOCK_W, BLOCK_W)
        return jnp.einsum("ljicd,ik->ljickd", w5, eye).reshape(N_A, N_GATE_TILES, GATE_TILE, GATE_TILE)

    return jnp.concatenate([dense(wa), dense(wx)], axis=-1).astype(BF16)


def _prepare_weights(norm_g, w_in, w_out, conv_w, conv_b, lru_wa, lru_ba, lru_wx, lru_bx, lru_lambda,
                     kv_norm_g, w_kv, sinks, final_g):
    col = jnp.arange(D_IN)
    is_gate = ((col >= D_MAIN) & (col < 2 * D_MAIN)) | (col >= 2 * D_MAIN + D_MEMW)
    half = HEAD_DIM // 2
    inv = ROPE_THETA ** (-jnp.arange(half, dtype=F32) / half)
    sign = jnp.concatenate([-jnp.ones((half,), F32), jnp.ones((half,), F32)])
    return dict(
        norm_g=norm_g.reshape(DEPTH, 1, D_MODEL),
        w_in=(w_in * jnp.where(is_gate, 0.5, 1.0).astype(F32)).astype(BF16),
        w_out=w_out.astype(BF16),
        conv_w=conv_w, conv_b=conv_b.reshape(N_A, 1, D_MAIN),
        w_gate=_gate_tiles(0.5 * lru_wa, 0.5 * lru_wx),
        ba=0.5 * lru_ba.reshape(N_A, 1, D_MAIN), bx=0.5 * lru_bx.reshape(N_A, 1, D_MAIN),
        lam=lru_lambda.reshape(N_A, 1, D_MAIN),
        sinks=sinks, sinks_col=sinks.reshape(DEPTH - N_A, N_HEADS, 1),
        kv_norm_g=kv_norm_g.reshape(1, D_MODEL), w_kv=w_kv.astype(BF16),
        final_g=final_g.reshape(1, D_MODEL),
        inv128=jnp.tile(inv, LANES // half).reshape(1, LANES),
        sign128=jnp.tile(sign, LANES // HEAD_DIM).reshape(1, LANES))


def _rope_table_kernel(inv_ref, sign_ref, cos_ref, sin_ref):
    rows = cos_ref.shape[0]
    bases = rows // WINDOW
    inv = inv_ref[...]
    offset = lax.broadcasted_iota(jnp.int32, (WINDOW, LANES), 0).astype(F32) * inv
    cos_o, sin_o = jnp.cos(offset), jnp.sin(offset)
    base_pos = pl.program_id(0) * rows + WINDOW * lax.broadcasted_iota(jnp.int32, (bases, LANES), 0)
    base = base_pos.astype(F32) * inv
    cos_b, sin_b = jnp.cos(base), jnp.sin(base)
    for s in range(bases):
        blk = slice(WINDOW * s, WINDOW * (s + 1))
        cb, sb = cos_b[s:s + 1, :], sin_b[s:s + 1, :]
        cos_ref[blk, :] = cb * cos_o - sb * sin_o
        sin_ref[blk, :] = (sb * cos_o + cb * sin_o) * sign_ref[...]


def _rope_tables(seq, inv128, sign128, rows=1024):
    rows = min(rows, seq)
    return pl.pallas_call(
        _rope_table_kernel,
        grid=(seq // rows,),
        in_specs=[_const_spec((1, LANES)), _const_spec((1, LANES))],
        out_specs=[pl.BlockSpec((rows, LANES), lambda i: (i, 0)),
                   pl.BlockSpec((rows, LANES), lambda i: (i, 0))],
        out_shape=[jax.ShapeDtypeStruct((seq, LANES), F32)] * 2,
        name="rope_tables",
    )(inv128, sign128)


def _mem_kv_kernel(m_ref, g_ref, w_ref, k_ref, v_ref, kb_ref, vb_ref):
    batch = m_ref.shape[0]
    mn = _rms(m_ref[...].reshape(batch * N_MEM, D_MODEL), g_ref[0]).astype(BF16)
    kv = jnp.dot(mn, w_ref[0].astype(BF16), preferred_element_type=F32)
    for b in range(batch):
        rows = slice(N_MEM * b, N_MEM * (b + 1))
        k, v = kv[rows, :D_MEMW], kv[rows, D_MEMW:]
        for hh in range(MEM_HEADS):
            cols = slice(MEM_HD * hh, MEM_HD * (hh + 1))
            k_ref[0, b, pl.ds(hh, N_MEM, stride=MEM_HEADS), :] = k[:, cols]
            v_ref[0, b, pl.ds(hh, N_MEM, stride=MEM_HEADS), :] = v[:, cols]
        kb_ref[0, b] = k.astype(BF16)
        vb_ref[0, b] = v.T.astype(BF16)


def _mem_kv(mem_prompt, mem_norm_g, w_mem_kv):
    batch = mem_prompt.shape[0]
    layer_blk = lambda *shape: pl.BlockSpec((1,) + shape, lambda l: (l,) + (0,) * len(shape))
    return pl.pallas_call(
        _mem_kv_kernel,
        grid=(DEPTH,),
        in_specs=[_const_spec((batch, N_MEM, D_MODEL)), layer_blk(1, D_MODEL),
                  layer_blk(D_MODEL, 2 * D_MEMW)],
        out_specs=[layer_blk(batch, N_MEM * MEM_HEADS, MEM_HD), layer_blk(batch, N_MEM * MEM_HEADS, MEM_HD),
                   layer_blk(batch, N_MEM, D_MEMW), layer_blk(batch, D_MEMW, N_MEM)],
        out_shape=[jax.ShapeDtypeStruct((DEPTH, batch, N_MEM * MEM_HEADS, MEM_HD), F32)] * 2
                  + [jax.ShapeDtypeStruct((DEPTH, batch, N_MEM, D_MEMW), BF16),
                     jax.ShapeDtypeStruct((DEPTH, batch, D_MEMW, N_MEM), BF16)],
        compiler_params=pltpu.CompilerParams(vmem_limit_bytes=VMEM_LIMIT_BYTES),
        name="mem_kv",
    )(mem_prompt, mem_norm_g.reshape(DEPTH, 1, D_MODEL), w_mem_kv)


def _prompt_lru_kernel(tile, make_side, x_ref, ng_ref, win_ref, wout_ref, cw_ref, cb_ref, wg_ref, ba_ref,
                       bx_ref, lam_ref, mk_ref, mv_ref, xo_ref, hl_ref, ct_ref,
                       tails, a3, b3, h3, hc, z, hn_s, proj_s):
    half = tile // 2
    t = pl.program_id(1)
    side_begin, side_thunks, side_end = make_side(dict(
        ng=ng_ref, win=win_ref, wout=wout_ref, cw=cw_ref, cb=cb_ref, wg=wg_ref, ba=ba_ref, bx=bx_ref, lam=lam_ref))

    @pl.when(t == 0)
    def _():
        tails[...] = jnp.zeros((SUBLANES, D_MAIN), F32)
        hc[...] = jnp.zeros((SUBLANES, LANES), F32)

    side_begin()

    decay_c = _decay_exponent(lam_ref[...])
    carry = [hc[...]]
    conv_out = {}


    project = lambda r0: _project_thunks(x_ref, ng_ref, win_ref, hn_s, proj_s, r0, half)

    def gates(r0):
        rows = slice(r0, r0 + half)
        g0, gn = r0 // SUBLANES, half // SUBLANES

        def conv():
            u = proj_s[rows, 0:D_MAIN]
            first_row = lax.broadcasted_iota(jnp.int32, (half, D_MAIN), 0) == 0
            acc = u * cw_ref[0:1, :]
            for k in range(1, CONV_W):
                delayed = jnp.where(first_row, tails[k - 1:k, :], pltpu.roll(acc, 1, 0))
                tails[k - 1:k, :] = acc[half - 1:half, :]
                acc = delayed + u * cw_ref[k:k + 1, :]
            conv_out[r0] = acc + cb_ref[...]

        def gate(j):
            def run():
                sl = slice(GATE_TILE * j, GATE_TILE * (j + 1))
                uc = conv_out[r0][:, sl]
                pre = jnp.dot(uc.astype(BF16), wg_ref[j], preferred_element_type=F32)
                a, b = _rglru_coeffs(uc, pre, ba_ref[:, sl], bx_ref[:, sl], decay_c[:, sl])
                for jj in range(GATE_TILE // LANES):
                    slab = j * (GATE_TILE // LANES) + jj
                    srows = slice(SUBLANES * slab, SUBLANES * (slab + 1))
                    cols = slice(LANES * jj, LANES * (jj + 1))
                    a3[g0:g0 + gn, srows, :] = a[:, cols].reshape(gn, SUBLANES, LANES)
                    b3[g0:g0 + gn, srows, :] = b[:, cols].reshape(gn, SUBLANES, LANES)
            return run

        return [conv] + [gate(j) for j in range(N_GATE_TILES)]

    def scan(r0):
        def piece(first, count):
            def run():
                h = carry[0]
                for g in range(first, first + count):
                    for r in range(SUBLANES):
                        step = pl.ds(r, N_SLABS, stride=SUBLANES)
                        h = a3[g, step, :] * h + b3[g, step, :]
                        h3[g, step, :] = h
                carry[0] = h
            return run

        g0, gn = r0 // SUBLANES, half // SUBLANES
        count = max(gn // SCAN_CHUNKS, 1)
        return [piece(g, count) for g in range(g0, g0 + gn, count)]

    def gating(r0):
        def slab(j):
            def run():
                g0, gn = r0 // SUBLANES, half // SUBLANES
                y = h3[g0:g0 + gn, SUBLANES * j:SUBLANES * (j + 1), :].reshape(half, LANES)
                g_main = proj_s[r0:r0 + half, D_MAIN + LANES * j:D_MAIN + LANES * (j + 1)]
                z[r0:r0 + half, LANES * j:LANES * (j + 1)] = (y * _silu_of_half(g_main)).astype(BF16)
            return run
        return [slab(j) for j in range(N_SLABS)]

    out_project = lambda r0: _out_project_thunks(x_ref, z, wout_ref, xo_ref, r0, half)
    mem =lambda r0: _skewed(_mem_units(proj_s, mk_ref, mv_ref, z, r0, half), ATTN_SKEW)
    first, second = 0, half
    n_conv_in = 1 + D_MAIN // PROJ_CHUNK
    p_first, p_second = project(first), project(second)
    _run(p_first[:n_conv_in])
    _run(_interleave(p_first[n_conv_in:], gates(first)))
    _run(p_second[:n_conv_in])
    _run(_interleave(p_second[n_conv_in:], gates(second), mem(first), scan(first)))
    _run(gating(first))
    _run(_interleave(out_project(first), mem(second), scan(second), side_thunks))
    _run(gating(second))
    _run(out_project(second))

    h = carry[0]
    hc[...] = h

    @pl.when(t == pl.num_programs(1) - 1)
    def _():
        hl_ref[0] = h
        ct_ref[0] = proj_s[tile - SUBLANES:tile, 0:D_MAIN]

    side_end()


def _prompt_lru_part(x, W, layer, mk, mv_t, tile=256):
    batch, seq, _ = x.shape
    tok = lambda b, t: (b, t, 0)
    per_b = lambda b, t: (b, 0, 0)
    mem = lambda b, t: (layer, b, 0, 0)
    in_specs = [
        pl.BlockSpec((1, tile, D_MODEL), tok),
        _layer_spec((1, D_MODEL), layer), _layer_spec((D_MODEL, D_IN), layer),
        _layer_spec((D_CAT, D_MODEL), layer),
        _layer_spec((CONV_W, D_MAIN), layer), _layer_spec((1, D_MAIN), layer),
        _layer_spec((N_GATE_TILES, GATE_TILE, 2 * GATE_TILE), layer),
        _layer_spec((1, D_MAIN), layer), _layer_spec((1, D_MAIN), layer), _layer_spec((1, D_MAIN), layer),
        pl.BlockSpec((None, 1, N_MEM, D_MEMW), mem), pl.BlockSpec((None, 1, D_MEMW, N_MEM), mem),
    ]
    args = [x, W["norm_g"], W["w_in"], W["w_out"], W["conv_w"], W["conv_b"], W["w_gate"],
            W["ba"], W["bx"], W["lam"], mk, mv_t]
    out_specs = [pl.BlockSpec((1, tile, D_MODEL), tok),
                 pl.BlockSpec((1, SUBLANES, LANES), per_b),
                 pl.BlockSpec((1, SUBLANES, D_MAIN), per_b)]
    out_shape = [jax.ShapeDtypeStruct((batch, seq, D_MODEL), F32),
                 jax.ShapeDtypeStruct((batch, SUBLANES, LANES), F32),
                 jax.ShapeDtypeStruct((batch, SUBLANES, D_MAIN), F32)]
    groups = tile // SUBLANES
    return dict(
        body=functools.partial(_prompt_lru_kernel, tile), grid=(batch, seq // tile),
        in_specs=in_specs, args=args, out_specs=out_specs, out_shape=out_shape,
        scratch=[pltpu.VMEM((SUBLANES, D_MAIN), F32),
                 pltpu.VMEM((groups, SUBLANES * N_SLABS, LANES), F32),
                 pltpu.VMEM((groups, SUBLANES * N_SLABS, LANES), F32),
                 pltpu.VMEM((groups, SUBLANES * N_SLABS, LANES), F32),
                 pltpu.VMEM((SUBLANES, LANES), F32),
                 pltpu.VMEM((tile, D_CAT), BF16),
                 pltpu.VMEM((tile, D_MODEL), BF16),
                 pltpu.VMEM((tile, D_IN), F32)],
        name="lru")


def _sink_softmax(s, sink, axis=-1):
    m = jnp.maximum(jnp.max(s, axis=axis, keepdims=True), sink)
    p = jnp.exp2(s - m)
    den = jnp.sum(p, axis=axis, keepdims=True) + jnp.exp2(sink - m)
    return p, den


def _prompt_swa_kernel(make_kv, final, tile, sink_row, make_side, *refs):
    it = iter(refs)
    take = lambda n: [next(it) for _ in range(n)]
    x_ref, ng_ref, win_ref, wout_ref, mk_ref, mv_ref, sink_ref = take(7)
    if make_kv:
        kvg_ref, wkv_ref = take(2)
    else:
        kp_ref, kc_ref, vp_ref, vc_ref = take(4)
    cos_ref, sin_ref = take(2)
    fg_ref = take(1)[0] if final else None
    xo_ref, = take(1)
    if make_kv:
        ko_ref, vo_ref = take(2)
    if final:
        y_ref, = take(1)
    z, qz, kcat, vcat_t, hn_s, proj_s = take(6)
    half = tile // 2
    t = pl.program_id(1)
    side_begin, side_thunks, side_end = make_side(dict(ng=ng_ref, win=win_ref, wout=wout_ref, fg=fg_ref))
    side_begin()

    def rope_queries(r0):
        rows = slice(r0, r0 + half)

        def slab(j):
            def run():
                cos = cos_ref[rows, :]
                sin = sin_ref[rows, :]
                lane = lax.broadcasted_iota(jnp.int32, (half, LANES), 1)
                lo = lane < HEAD_DIM
                first_half = (lane % HEAD_DIM) < (HEAD_DIM // 2)
                c = j // (GROUP // 2)
                xq = proj_s[rows, LANES * j:LANES * (j + 1)] * (HEAD_DIM ** -0.5 * LOG2_E)
                r32 = pltpu.roll(xq, HEAD_DIM // 2, 1)
                r96 = pltpu.roll(xq, LANES - HEAD_DIM // 2, 1)
                r64 = pltpu.roll(xq, HEAD_DIM, 1)
                stay = xq * cos + jnp.where(first_half, r96, r32) * sin
                move = r64 * cos + jnp.where(first_half, r32, r96) * sin
                keep = lo if c == 0 else ~lo
                even, odd = (stay, move) if c == 0 else (move, stay)
                qz[rows, 2 * LANES * j:2 * LANES * j + LANES] = jnp.where(keep, even, 0.0).astype(BF16)
                qz[rows, 2 * LANES * j + LANES:2 * LANES * (j + 1)] = jnp.where(keep, odd, 0.0).astype(BF16)
            return run

        return [slab(j) for j in range(N_SLABS)]

    if make_kv:
        @pl.when(t == 0)
        def _():
            kcat[0:WINDOW, :] = jnp.zeros((WINDOW, D_KV), BF16)
            vcat_t[:, 0:WINDOW] = jnp.zeros((D_KV, WINDOW), BF16)

        @pl.when(t > 0)
        def _():
            kcat[0:WINDOW, :] = kcat[tile:tile + WINDOW, :]
            vcat_t[:, 0:WINDOW] = vcat_t[:, tile:tile + WINDOW]

        def shared_kv(r0):
            def run():
                rows = slice(r0, r0 + half)
                kvn = _rms(x_ref[0, rows, :], kvg_ref[...]).astype(BF16)
                kv = jnp.dot(kvn, wkv_ref[...], preferred_element_type=F32)
                k = _rope(kv[:, :D_KV], cos_ref[rows, :], sin_ref[rows, :])
                v = kv[:, D_KV:]
                ko_ref[0, rows, :] = k
                vo_ref[0, rows, :] = v
                kcat[WINDOW + r0:WINDOW + r0 + half, :] = k.astype(BF16)
                vcat_t[:, WINDOW + r0:WINDOW + r0 + half] = v.T.astype(BF16)
            return [run]
    else:
        kcat[0:WINDOW, :] = kp_ref[0].astype(BF16)
        kcat[WINDOW:, :] = kc_ref[0].astype(BF16)
        vcat_t[:, 0:WINDOW] = vp_ref[0].T.astype(BF16)
        vcat_t[:, WINDOW:] = vc_ref[0].T.astype(BF16)
        shared_kv = lambda r0: []

    key_r = lax.broadcasted_iota(jnp.int32, (WINDOW, 2 * WINDOW), 0)
    query = lax.broadcasted_iota(jnp.int32, (WINDOW, 2 * WINDOW), 1) % WINDOW
    from_prev = key_r > query

    def scores(n, j):
        rows = slice(WINDOW * n, WINDOW * (n + 1))
        keys = kcat[WINDOW * n:WINDOW * (n + 2), :]
        q2 = jnp.concatenate([qz[rows, 2 * LANES * j:2 * LANES * j + LANES],
                              qz[rows, 2 * LANES * j + LANES:2 * LANES * (j + 1)]], axis=0)
        s = lax.dot_general(keys, q2, _NT, preferred_element_type=F32)
        s_prev, s_own = s[:WINDOW, :], s[WINDOW:, :]
        if n == 0:
            s_prev = jnp.where(t > 0, s_prev, NEG)
        visible = jnp.where(from_prev, s_prev, s_own)
        p0, den0 = _sink_softmax(visible[:, :WINDOW], sink_ref[sink_row, 2 * j] * LOG2_E, axis=0)
        p1, den1 = _sink_softmax(visible[:, WINDOW:], sink_ref[sink_row, 2 * j + 1] * LOG2_E, axis=0)
        p = jnp.concatenate([p0, p1], axis=1).astype(BF16)
        zero = jnp.zeros_like(p)
        return jnp.concatenate([jnp.where(from_prev, p, zero), jnp.where(from_prev, zero, p)], axis=0), den0, den1

    def outputs(n, j, p, den0, den1):
        rows = slice(WINDOW * n, WINDOW * (n + 1))
        c = j // (GROUP // 2)
        vals_t = vcat_t[HEAD_DIM * c:HEAD_DIM * (c + 1), WINDOW * n:WINDOW * (n + 2)]
        o_t = jnp.dot(vals_t, p, preferred_element_type=F32)
        y_t = jnp.concatenate([o_t[:, :WINDOW] / den0, o_t[:, WINDOW:] / den1], axis=0)
        g_main = proj_s[rows, D_MAIN + LANES * j:D_MAIN + LANES * (j + 1)]
        z[rows, LANES * j:LANES * (j + 1)] = (y_t.T * _silu_of_half(g_main)).astype(BF16)

    def attend(r0):
        units = [(functools.partial(scores, n, j), functools.partial(outputs, n, j))
                 for n in range(r0 // WINDOW, (r0 + half) // WINDOW) for j in range(N_SLABS)]
        return _skewed(units + _mem_units(proj_s, mk_ref, mv_ref, z, r0, half), ATTN_SKEW)

    def out_project(r0):
        rows = slice(r0, r0 + half)

        def final_norm():
            y_ref[0, rows, :] = _rms(xo_ref[0, rows, :], fg_ref[...])

        return _out_project_thunks(x_ref, z, wout_ref, xo_ref, r0, half) + ([final_norm] if final else [])

    project = lambda r0: _project_thunks(x_ref, ng_ref, win_ref, hn_s, proj_s, r0, half)
    first, second = 0, half
    _run(project(first) + shared_kv(first))
    _run(_interleave(project(second) + shared_kv(second), rope_queries(first)))
    _run(_interleave(attend(first), rope_queries(second)))
    _run(_interleave(attend(second), out_project(first), side_thunks))
    _run(out_project(second))
    side_end()


def _prompt_swa_part(x, W, layer, mk, mv_t, kv, cos, sin, final_g=None, tile=256):
    batch, seq, _ = x.shape
    final = final_g is not None
    make_kv = kv is None
    blocks = tile // WINDOW
    tok = lambda b, t: (b, t, 0)
    mem = lambda b, t: (layer, b, 0, 0)
    prev_blk = lambda b, t: (b, jnp.maximum(t * blocks - 1, 0), 0)
    in_specs = [
        pl.BlockSpec((1, tile, D_MODEL), tok),
        _layer_spec((1, D_MODEL), layer), _layer_spec((D_MODEL, D_IN), layer),
        _layer_spec((D_CAT, D_MODEL), layer),
        pl.BlockSpec((None, 1, N_MEM, D_MEMW), mem), pl.BlockSpec((None, 1, D_MEMW, N_MEM), mem),
        pl.BlockSpec(memory_space=pltpu.SMEM)]
    args = [x, W["norm_g"], W["w_in"], W["w_out"], mk, mv_t, W["sinks"]]
    if make_kv:
        in_specs += [_const_spec((1, D_MODEL)), _const_spec((D_MODEL, 2 * D_KV))]
        args += [W["kv_norm_g"], W["w_kv"]]
    else:
        k, v = kv
        in_specs += [pl.BlockSpec((1, WINDOW, D_KV), prev_blk), pl.BlockSpec((1, tile, D_KV), tok),
                     pl.BlockSpec((1, WINDOW, D_KV), prev_blk), pl.BlockSpec((1, tile, D_KV), tok)]
        args += [k, k, v, v]
    in_specs += [pl.BlockSpec((tile, LANES), lambda b, t: (t, 0)), pl.BlockSpec((tile, LANES), lambda b, t: (t, 0))]
    args += [cos, sin]
    out_specs = [pl.BlockSpec((1, tile, D_MODEL), tok)]
    out_shape = [jax.ShapeDtypeStruct((batch, seq, D_MODEL), F32)]
    if make_kv:
        out_specs += [pl.BlockSpec((1, tile, D_KV), tok)] * 2
        out_shape += [jax.ShapeDtypeStruct((batch, seq, D_KV), F32)] * 2
    if final:
        in_specs.append(_const_spec((1, D_MODEL)))
        args.append(final_g)
        out_specs.append(pl.BlockSpec((1, tile, D_MODEL), tok))
        out_shape.append(jax.ShapeDtypeStruct((batch, seq, D_MODEL), F32))
    return dict(
        body=functools.partial(_prompt_swa_kernel, make_kv, final, tile, layer - N_A), grid=(batch, seq // tile),
        in_specs=in_specs, args=args, out_specs=out_specs, out_shape=out_shape,
        scratch=[pltpu.VMEM((tile, D_CAT), BF16),
                 pltpu.VMEM((tile, 2 * D_MAIN), BF16),
                 pltpu.VMEM((WINDOW + tile, D_KV), BF16),
                 pltpu.VMEM((D_KV, WINDOW + tile), BF16),
                 pltpu.VMEM((tile, D_MODEL), BF16),
                 pltpu.VMEM((tile, D_IN), F32)],
        name="swa" + ("_kv" if make_kv else "") + ("_final" if final else ""))


def _sample_mem_unit(b, bb, qm_s, mk_ref, mv_ref, om_s):
    def scores():
        row = lax.broadcasted_iota(jnp.int32, (SUBLANES, MEM_HD), 0)
        q = qm_s[pl.ds(b, 1), :]
        qh = jnp.zeros((SUBLANES, MEM_HD), F32)
        for hh in range(MEM_HEADS):
            q_head = jnp.broadcast_to(q[:, MEM_HD * hh:MEM_HD * (hh + 1)], (SUBLANES, MEM_HD))
            qh = jnp.where(row == hh, q_head, qh)
        s = lax.dot_general(qh.astype(BF16), mk_ref[bb].astype(BF16), _NT,
                            preferred_element_type=F32) * (MEM_HD ** -0.5 * LOG2_E)
        srow = lax.broadcasted_iota(jnp.int32, s.shape, 0)
        scol = lax.broadcasted_iota(jnp.int32, s.shape, 1)
        s = jnp.where((scol % MEM_HEADS) == srow, s, NEG)
        m = jnp.max(s, axis=-1, keepdims=True)
        p = jnp.exp2(s - m)
        return p.astype(BF16), jnp.sum(p, axis=-1, keepdims=True)

    def outputs(p, den):
        o = jnp.dot(p, mv_ref[bb].astype(BF16), preferred_element_type=F32) / den
        om_s[pl.ds(b, 1), :] = jnp.concatenate([o[hh:hh + 1, :] for hh in range(MEM_HEADS)], axis=1)

    return scores, outputs


def _sample_lru_side(seqs, step, n_steps, w, x_ref, h0_ref, c0_ref, mk_ref, mv_ref, xo_ref, ho_ref, co_ref,
                     z_s, qm_s, gm_s, om_s):
    def begin():
        @pl.when(step == 0)
        def _():
            hn = _rms(x_ref[...], w["ng"][...]).astype(BF16)
            proj = jnp.dot(hn, w["win"][...], preferred_element_type=F32)
            u = proj[:, :D_MAIN]
            taps = [c0_ref[k] for k in range(CONV_W - 1)] + [u]
            uc = w["cb"][...] + taps[0] * w["cw"][0:1, :]
            for k in range(1, CONV_W):
                uc = uc + taps[k] * w["cw"][k:k + 1, :]
            for k in range(CONV_W - 1):
                co_ref[k] = taps[k + 1]
            decay_c = _decay_exponent(w["lam"][...])
            ucb = uc.astype(BF16)
            for j in range(N_GATE_TILES):
                sl = slice(GATE_TILE * j, GATE_TILE * (j + 1))
                pre = jnp.dot(ucb[:, sl], w["wg"][j], preferred_element_type=F32)
                a, b = _rglru_coeffs(uc[:, sl], pre, w["ba"][:, sl], w["bx"][:, sl], decay_c[:, sl])
                h = a * h0_ref[:, sl] + b
                ho_ref[:, sl] = h
                z_s[:, sl] = (h * _silu_of_half(
                    proj[:, D_MAIN + GATE_TILE * j:D_MAIN + GATE_TILE * (j + 1)])).astype(BF16)
            qm_s[...] = proj[:, 2 * D_MAIN:2 * D_MAIN + D_MEMW]
            gm_s[...] = _silu_of_half(proj[:, 2 * D_MAIN + D_MEMW:])

    thunks = _skewed([_sample_mem_unit(step * seqs + bb, bb, qm_s, mk_ref, mv_ref, om_s) for bb in range(seqs)],
                     ATTN_SKEW)

    def end():
        @pl.when(step == n_steps - 1)
        def _():
            z_s[:, D_MAIN:] = (om_s[...] * gm_s[...]).astype(BF16)
            xo_ref[...] = x_ref[...] + jnp.dot(z_s[...], w["wout"][...], preferred_element_type=F32)

    return begin, thunks, end


def _cache_spec(layer, seqs, steps_per_row):
    return pl.BlockSpec((None, seqs, N_MEM * MEM_HEADS, MEM_HD),
                        lambda b, t: (layer, b * steps_per_row + t, 0, 0))


def _whole_spec(shape):
    zeros = (0,) * len(shape)
    return pl.BlockSpec(shape, lambda *_: zeros)


def _sample_lru_part(x, layer, state_h, state_conv_t, cache_k, cache_v, grid):
    batch = x.shape[0]
    seqs = batch // (grid[0] * grid[1])
    cache_spec = _cache_spec(layer, seqs, grid[1])
    return dict(
        body=functools.partial(_sample_lru_side, seqs),
        in_specs=[_const_spec((batch, D_MODEL)),
                  _layer_spec((batch, D_MAIN), layer), _layer_spec((CONV_W - 1, batch, D_MAIN), layer),
                  cache_spec, cache_spec],
        args=[x, state_h, state_conv_t, cache_k, cache_v],
        out_specs=[_whole_spec((batch, D_MODEL)), _whole_spec((batch, D_MAIN)),
                   _whole_spec((CONV_W - 1, batch, D_MAIN))],
        out_shape=[jax.ShapeDtypeStruct((batch, D_MODEL), F32),
                   jax.ShapeDtypeStruct((batch, D_MAIN), F32),
                   jax.ShapeDtypeStruct((CONV_W - 1, batch, D_MAIN), F32)],
        scratch=[pltpu.VMEM((batch, D_CAT), BF16), pltpu.VMEM((batch, D_MEMW), F32),
                 pltpu.VMEM((batch, D_MEMW), F32), pltpu.VMEM((batch, D_MEMW), F32)])


def _sample_swa_side(first, final, seqs, step, n_steps, w, *refs):
    (x_ref, sink_ref, mk_ref, mv_ref, wk_ref, wv_ref) = refs[:6]
    pos = 6
    if first:
        kvg_ref, wkv_ref = refs[pos:pos + 2]
        pos += 2
    inv_ref, sign_ref = refs[pos:pos + 2]
    pos += 2
    xo_ref = refs[pos]
    pos += 1
    if first:
        wko_ref, wvo_ref = refs[pos:pos + 2]
        pos += 2
    if final:
        y_ref = refs[pos]
        pos += 1
    z_s, q_s, gmain_s, y_s, qm_s, gm_s, om_s, kn_s, vn_s = refs[pos:]

    def begin():
        @pl.when(step == 0)
        def _():
            x = x_ref[...]
            hn = _rms(x, w["ng"][...]).astype(BF16)
            proj = jnp.dot(hn, w["win"][...], preferred_element_type=F32)
            ang = float(PAST_LEN) * inv_ref[...]
            cos = jnp.cos(ang)
            sin = jnp.sin(ang) * sign_ref[...]
            for j in range(N_SLABS):
                cols = slice(LANES * j, LANES * (j + 1))
                q_s[:, cols] = _rope(proj[:, cols], cos, sin)
            gmain_s[...] = _silu_of_half(proj[:, D_MAIN:2 * D_MAIN])
            qm_s[...] = proj[:, 2 * D_MAIN:2 * D_MAIN + D_MEMW]
            gm_s[...] = _silu_of_half(proj[:, 2 * D_MAIN + D_MEMW:])
            if first:
                kvn = _rms(x, kvg_ref[...]).astype(BF16)
                kv = jnp.dot(kvn, wkv_ref[...], preferred_element_type=F32)
                kn_s[...] = _rope(kv[:, :D_KV], cos, sin).T
                vn_s[...] = kv[:, D_KV:].T

    row = lax.broadcasted_iota(jnp.int32, (N_HEADS, LANES), 0)
    lane = lax.broadcasted_iota(jnp.int32, (N_HEADS, LANES), 1)
    kv_half = (lane // HEAD_DIM) == (row // GROUP)
    in_place = (row % 2) == (row // GROUP)
    lane1 = lax.broadcasted_iota(jnp.int32, (1, LANES), 1)
    seq_lane = lax.broadcasted_iota(jnp.int32, kn_s.shape, 1)
    slot = lax.broadcasted_iota(jnp.int32, (D_KV, WINDOW), 1)

    def slide(win_t, new_t_s, b):
        if new_t_s.shape[1] == WINDOW:
            col = pltpu.roll(new_t_s[...], WINDOW - 1 - b, 1)
        else:
            col = jnp.sum(jnp.where(seq_lane == b, new_t_s[...], 0.0), axis=1, keepdims=True)
        return jnp.where(slot == WINDOW - 1, col, pltpu.roll(win_t, WINDOW - 1, 1))

    def window_unit(b, bb):
        def scores():
            if first:
                kw_t = slide(wk_ref[bb], kn_s, b)
                wko_ref[bb] = kw_t
            else:
                kw_t = wk_ref[bb]
            q = jnp.broadcast_to(q_s[pl.ds(b, 1), :], (N_HEADS, D_MAIN))
            e = jnp.zeros((N_HEADS, LANES), F32)
            for j in range(N_SLABS):
                e = jnp.where((row // 2) == j, q[:, LANES * j:LANES * (j + 1)], e)
            qh = jnp.where(kv_half, jnp.where(in_place, e, pltpu.roll(e, HEAD_DIM, 1)), 0.0).astype(BF16)
            s = jnp.dot(qh, kw_t.astype(BF16), preferred_element_type=F32) * (HEAD_DIM ** -0.5 * LOG2_E)
            p, den = _sink_softmax(s, sink_ref[...] * LOG2_E)
            return p.astype(BF16), den

        def outputs(p, den):
            if first:
                vw_t = slide(wv_ref[bb], vn_s, b)
                wvo_ref[bb] = vw_t
            else:
                vw_t = wv_ref[bb]
            o = lax.dot_general(p, vw_t.astype(BF16), _NT, preferred_element_type=F32) / den
            f = jnp.where(in_place, o, pltpu.roll(o, HEAD_DIM, 1))
            y_s[pl.ds(b, 1), :] = jnp.concatenate(
                [jnp.where(lane1 < HEAD_DIM, f[2 * j:2 * j + 1, :], f[2 * j + 1:2 * j + 2, :])
                 for j in range(N_SLABS)], axis=1)

        return scores, outputs

    units = []
    for bb in range(seqs):
        units.append(window_unit(step * seqs + bb, bb))
        units.append(_sample_mem_unit(step * seqs + bb, bb, qm_s, mk_ref, mv_ref, om_s))

    def end():
        @pl.when(step == n_steps - 1)
        def _():
            z_s[:, :D_MAIN] = (y_s[...] * gmain_s[...]).astype(BF16)
            z_s[:, D_MAIN:] = (om_s[...] * gm_s[...]).astype(BF16)
            xn = x_ref[...] + jnp.dot(z_s[...], w["wout"][...], preferred_element_type=F32)
            xo_ref[...] = xn
            if final:
                y_ref[...] = _rms(xn, w["fg"][...])

    return begin, _skewed(units, ATTN_SKEW), end


def _sample_swa_part(x, W, layer, cache_k, cache_v, win_k_t, win_v_t, first, final, grid):
    batch = x.shape[0]
    seqs = batch // (grid[0] * grid[1])
    cache_spec = _cache_spec(layer, seqs, grid[1])
    win_spec = pl.BlockSpec((seqs, D_KV, WINDOW), lambda b, t: (b * grid[1] + t, 0, 0))
    in_specs = [_const_spec((batch, D_MODEL)), _layer_spec((N_HEADS, 1), layer - N_A),
                cache_spec, cache_spec, win_spec, win_spec]
    args = [x, W["sinks_col"], cache_k, cache_v, win_k_t, win_v_t]
    if first:
        in_specs += [_const_spec((1, D_MODEL)), _const_spec((D_MODEL, 2 * D_KV))]
        args += [W["kv_norm_g"], W["w_kv"]]
    in_specs += [_const_spec((1, LANES)), _const_spec((1, LANES))]
    args += [W["inv128"], W["sign128"]]
    out_specs = [_whole_spec((batch, D_MODEL))]
    out_shape = [jax.ShapeDtypeStruct((batch, D_MODEL), F32)]
    if first:
        out_specs += [win_spec, win_spec]
        out_shape += [jax.ShapeDtypeStruct((batch, D_KV, WINDOW), F32)] * 2
    if final:
        out_specs.append(_whole_spec((batch, D_MODEL)))
        out_shape.append(jax.ShapeDtypeStruct((batch, D_MODEL), F32))
    return dict(
        body=functools.partial(_sample_swa_side, first, final, seqs),
        in_specs=in_specs, args=args, out_specs=out_specs, out_shape=out_shape,
        scratch=[pltpu.VMEM((batch, D_CAT), BF16), pltpu.VMEM((batch, D_MAIN), F32),
                 pltpu.VMEM((batch, D_MAIN), F32), pltpu.VMEM((batch, D_MAIN), F32),
                 pltpu.VMEM((batch, D_MEMW), F32), pltpu.VMEM((batch, D_MEMW), F32),
                 pltpu.VMEM((batch, D_MEMW), F32),
                 pltpu.VMEM((D_KV, batch), F32), pltpu.VMEM((D_KV, batch), F32)])


def _layer_kernel(prompt_body, sample_body, n_prompt, n_sample, *refs):
    it = iter(refs)
    take = lambda n: [next(it) for _ in range(n)]
    p_in, s_in = take(n_prompt[0]), take(n_sample[0])
    p_out, s_out = take(n_prompt[1]), take(n_sample[1])
    p_scr, s_scr = take(n_prompt[2]), take(n_sample[2])
    step = pl.program_id(0) * pl.num_programs(1) + pl.program_id(1)
    n_steps = pl.num_programs(0) * pl.num_programs(1)
    make_side = lambda weights: sample_body(step, n_steps, weights, *s_in, *s_out, *s_scr)
    prompt_body(make_side, *p_in, *p_out, *p_scr)


def _layer_call(prompt, sample):
    counts = lambda part: (len(part["in_specs"]), len(part["out_specs"]), len(part["scratch"]))
    outs = pl.pallas_call(
        functools.partial(_layer_kernel, prompt["body"], sample["body"], counts(prompt), counts(sample)),
        grid=prompt["grid"],
        in_specs=prompt["in_specs"] + sample["in_specs"],
        out_specs=prompt["out_specs"] + sample["out_specs"],
        out_shape=prompt["out_shape"] + sample["out_shape"],
        scratch_shapes=prompt["scratch"] + sample["scratch"],
        compiler_params=pltpu.CompilerParams(
            dimension_semantics=("arbitrary", "arbitrary"), vmem_limit_bytes=VMEM_LIMIT_BYTES),
        name="layer_" + prompt["name"],
    )(*prompt["args"], *sample["args"])
    n = len(prompt["out_specs"])
    return outs[:n], outs[n:]


def _trunks(xp, xs, W, mkb, mvb_t, cache_mem_k, cache_mem_v, state_h, state_conv, win_k, win_v, tile=512):
    pb, seq, _ = xp.shape
    sb = xs.shape[0]
    grid = (pb, seq // tile)
    cos, sin = _rope_tables(seq, W["inv128"], W["sign128"])
    xs = xs.reshape(sb, D_MODEL)
    ck = cache_mem_k.reshape(DEPTH, sb, N_MEM * MEM_HEADS, MEM_HD)
    cv = cache_mem_v.reshape(DEPTH, sb, N_MEM * MEM_HEADS, MEM_HD)
    conv_t = jnp.transpose(state_conv, (0, 2, 1, 3))
    wk = jnp.transpose(win_k, (0, 2, 3, 1)).reshape(sb, D_KV, WINDOW)
    wv = jnp.transpose(win_v, (0, 2, 3, 1)).reshape(sb, D_KV, WINDOW)
    hp, cp, hs, cs = [], [], [], []
    for l in range(N_A):
        p_out, s_out = _layer_call(
            _prompt_lru_part(xp, W, l, mkb, mvb_t, tile=tile),
            _sample_lru_part(xs, l, state_h, conv_t, ck, cv, grid))
        xp, h_last, conv_tail = p_out
        hp.append(h_last.reshape(pb, D_MAIN))
        cp.append(conv_tail[:, SUBLANES - (CONV_W - 1):, :])
        xs, h, c = s_out
        hs.append(h)
        cs.append(c)
    yp = ys = kv = None
    for l in range(N_A, DEPTH):
        first, last = l == N_A, l == DEPTH - 1
        p_out, s_out = _layer_call(
            _prompt_swa_part(xp, W, l, mkb, mvb_t, kv, cos, sin, final_g=W["final_g"] if last else None, tile=tile),
            _sample_swa_part(xs, W, l, ck, cv, wk, wv, first, last, grid))
        xp, xs = p_out[0], s_out[0]
        if first:
            kv = (p_out[1], p_out[2])
            wk, wv = s_out[1], s_out[2]
        if last:
            yp, ys = p_out[-1], s_out[-1]
    k, v = kv
    win_kp = k[:, seq - WINDOW:, :].reshape(pb, WINDOW, N_KV, HEAD_DIM)
    win_vp = v[:, seq - WINDOW:, :].reshape(pb, WINDOW, N_KV, HEAD_DIM)
    unslide = lambda w_t: jnp.transpose(w_t.reshape(sb, N_KV, HEAD_DIM, WINDOW), (0, 3, 1, 2))
    prompt_out = (yp, jnp.stack(hp), jnp.stack(cp), win_kp, win_vp)
    sample_out = (ys.reshape(sb, 1, D_MODEL), jnp.stack(hs), jnp.transpose(jnp.stack(cs), (0, 2, 1, 3)),
                  unslide(wk), unslide(wv))
    return prompt_out, sample_out


def kernel(x_prompt, x_sample, cache_mem_k, cache_mem_v, state_lru_h, state_conv, cache_win_k, cache_win_v, mem_prompt, norm_g, w_in, w_out, mem_norm_g, w_mem_kv, conv_w, conv_b, lru_wa, lru_ba, lru_wx, lru_bx, lru_lambda, kv_norm_g, w_kv, sinks, final_g):
    W = _prepare_weights(norm_g, w_in, w_out, conv_w, conv_b, lru_wa, lru_ba, lru_wx, lru_bx, lru_lambda, kv_norm_g, w_kv, sinks, final_g)
    mk, mv, mkb, mvb = _mem_kv(mem_prompt, mem_norm_g, w_mem_kv)
    (y_p, h_p, conv_p, wk_p, wv_p), (y_s, h_s, conv_s, wk_s, wv_s) = _trunks(
        x_prompt, x_sample, W, mkb, mvb, cache_mem_k, cache_mem_v, state_lru_h, state_conv,
        cache_win_k, cache_win_v)
    batch = x_prompt.shape[0]
    mk_p = mk.reshape(DEPTH, batch, N_MEM, MEM_HEADS, MEM_HD)
    mv_p = mv.reshape(DEPTH, batch, N_MEM, MEM_HEADS, MEM_HD)
    return (y_p, y_s, mk_p, mv_p, h_p, conv_p, wk_p, wv_p, h_s, conv_s, wk_s, wv_s)
```

```python
import functools

import jax
import jax.numpy as jnp
from jax import lax
from jax.experimental import pallas as pl
from jax.experimental.pallas import tpu as pltpu

F32 = jnp.float32
BF16 = jnp.bfloat16

D_MODEL = 1024
DEPTH = 4
N_A = DEPTH // 2
N_HEADS = 16
HEAD_DIM = 64
N_KV = 2
GROUP = N_HEADS // N_KV
D_MAIN = N_HEADS * HEAD_DIM
N_BLOCKS = 16
BLOCK_W = D_MAIN // N_BLOCKS
CONV_W = 4
LRU_C = 8.0
WINDOW = 128
ROPE_THETA = 10000.0
N_MEM = 256
MEM_HEADS = 4
MEM_HD = 128
D_MEMW = MEM_HEADS * MEM_HD
D_IN = 2 * D_MAIN + 2 * D_MEMW
D_CAT = D_MAIN + D_MEMW
D_KV = N_KV * HEAD_DIM
EPS = 1e-6
NEG = -1e30
LOG2_E = 1.4426950408889634
PAST_LEN = 8192

SUBLANES = 8
LANES = 128
N_SLABS = D_MAIN // LANES
GATE_TILE = 256
N_GATE_TILES = D_MAIN // GATE_TILE
VMEM_LIMIT_BYTES = 56 * 1024 * 1024
ATTN_SKEW = 8
MEM_CHUNK = 256
SCAN_CHUNKS = 16
PROJ_CHUNK = 512
OUT_CHUNK = 256

_NT = (((1,), (1,)), ((), ()))


def _const_spec(shape):
    zeros = (0,) * len(shape)
    return pl.BlockSpec(shape, lambda *_: zeros, pipeline_mode=pl.Buffered(1))


def _layer_spec(shape, layer):
    zeros = (0,) * len(shape)
    return pl.BlockSpec((None,) + tuple(shape), lambda *_: (layer,) + zeros, pipeline_mode=pl.Buffered(1))


def _rms(x, g):
    return x * lax.rsqrt(jnp.mean(x * x, axis=-1, keepdims=True) + EPS) * g


def _silu_of_half(hx):
    return hx * jnp.tanh(hx) + hx


def _softplus(x):
    return jnp.maximum(x, 0.0) + jnp.log1p(jnp.exp(-jnp.abs(x)))


def _decay_exponent(lam):
    return (-0.5 * LRU_C * LOG2_E) * _softplus(-lam)


def _rope(x, cos, sin_signed):
    lane = lax.broadcasted_iota(jnp.int32, x.shape, 1)
    first_half = (lane % HEAD_DIM) < (HEAD_DIM // 2)
    swapped = jnp.where(first_half,
                        pltpu.roll(x, LANES - HEAD_DIM // 2, 1),
                        pltpu.roll(x, HEAD_DIM // 2, 1))
    return x * cos + swapped * sin_signed


def _rglru_coeffs(uc, half_pre, half_ba, half_bx, c):
    half = half_pre.shape[-1] // 2
    t_r = jnp.tanh(half_pre[:, :half] + half_ba)
    t_i = jnp.tanh(half_pre[:, half:] + half_bx)
    a = jnp.exp2(c * t_r + c)
    w = 1.0 - a * a
    root = jnp.where(w > 0.0, w * lax.rsqrt(w), 0.0)
    return a, root * ((0.5 * t_i + 0.5) * uc)


def _mem_units(proj, mk_ref, mv_ref, z, row0, n_rows):
    chunk = min(n_rows, MEM_CHUNK)
    units = []
    for ch in range(n_rows // chunk):
        for hh in range(MEM_HEADS):
            rows = slice(row0 + chunk * ch, row0 + chunk * (ch + 1))
            cols = slice(MEM_HD * hh, MEM_HD * (hh + 1))
            q_cols = slice(2 * D_MAIN + MEM_HD * hh, 2 * D_MAIN + MEM_HD * (hh + 1))
            g_cols = slice(2 * D_MAIN + D_MEMW + MEM_HD * hh, 2 * D_MAIN + D_MEMW + MEM_HD * (hh + 1))
            z_cols = slice(D_MAIN + MEM_HD * hh, D_MAIN + MEM_HD * (hh + 1))

            def scores(rows=rows, cols=cols, q_cols=q_cols):
                q = proj[rows, q_cols].astype(BF16)
                s = lax.dot_general(mk_ref[0, :, cols], q, _NT,
                                    preferred_element_type=F32) * (MEM_HD ** -0.5 * LOG2_E)
                m = jnp.max(s, axis=0, keepdims=True)
                p = jnp.exp2(s - m)
                return p.astype(BF16), jnp.sum(p, axis=0, keepdims=True)

            def outputs(p, den, rows=rows, cols=cols, g_cols=g_cols, z_cols=z_cols):
                o_t = jnp.dot(mv_ref[0, cols, :], p, preferred_element_type=F32)
                z[rows, z_cols] = ((o_t / den).T * _silu_of_half(proj[rows, g_cols])).astype(BF16)

            units.append((scores, outputs))
    return units


def _skewed(units, skew):
    results = []
    first = lambda scores: lambda: results.append(scores())
    second = lambda outputs: lambda: outputs(*results.pop(0))
    thunks, pending = [], []
    for scores, outputs in units:
        thunks.append(first(scores))
        pending.append(second(outputs))
        if len(pending) > skew:
            thunks.append(pending.pop(0))
    return thunks + pending


def _interleave(*lists):
    lists = [l for l in lists if l]
    pos = [0] * len(lists)
    merged = []
    for _ in range(sum(len(l) for l in lists)):
        i = min((i for i in range(len(lists)) if pos[i] < len(lists[i])),
                key=lambda i: pos[i] / len(lists[i]))
        merged.append(lists[i][pos[i]])
        pos[i] += 1
    return merged


def _run(thunks):
    for thunk in thunks:
        thunk()


def _project_thunks(x_ref, ng_ref, win_ref, hn_s, proj_s, r0, n_rows):
    rows = slice(r0, r0 + n_rows)

    def norm():
        hn_s[rows, :] = _rms(x_ref[0, rows, :], ng_ref[...]).astype(BF16)

    def chunk(c0):
        def run():
            proj_s[rows, c0:c0 + PROJ_CHUNK] = jnp.dot(
                hn_s[rows, :], win_ref[:, c0:c0 + PROJ_CHUNK], preferred_element_type=F32)
        return run

    return [norm] + [chunk(c0) for c0 in range(0, D_IN, PROJ_CHUNK)]


def _out_project_thunks(x_ref, z, wout_ref, xo_ref, r0, n_rows):
    rows = slice(r0, r0 + n_rows)

    def chunk(c0):
        def run():
            cols = slice(c0, c0 + OUT_CHUNK)
            xo_ref[0, rows, cols] = x_ref[0, rows, cols] + jnp.dot(
                z[rows, :], wout_ref[:, cols], preferred_element_type=F32)
        return run

    return [chunk(c0) for c0 in range(0, D_MODEL, OUT_CHUNK)]


def _gate_tiles(wa, wx):
    per_tile = GATE_TILE // BLOCK_W
    eye = jnp.eye(per_tile, dtype=wa.dtype)

    def dense(w):
        w5 = w.reshape(N_A, N_GATE_TILES, per_tile, BLOCK_W, BLOCK_W)
        return jnp.einsum("ljicd,ik->ljickd", w5, eye).reshape(N_A, N_GATE_TILES, GATE_TILE, GATE_TILE)

    return jnp.concatenate([dense(wa), dense(wx)], axis=-1).astype(BF16)


def _prepare_weights(norm_g, w_in, w_out, conv_w, conv_b, lru_wa, lru_ba, lru_wx, lru_bx, lru_lambda,
                     kv_norm_g, w_kv, sinks, final_g):
    col = jnp.arange(D_IN)
    is_gate = ((col >= D_MAIN) & (col < 2 * D_MAIN)) | (col >= 2 * D_MAIN + D_MEMW)
    half = HEAD_DIM // 2
    inv = ROPE_THETA ** (-jnp.arange(half, dtype=F32) / half)
    sign = jnp.concatenate([-jnp.ones((half,), F32), jnp.ones((half,), F32)])
    return dict(
        norm_g=norm_g.reshape(DEPTH, 1, D_MODEL),
        w_in=(w_in * jnp.where(is_gate, 0.5, 1.0).astype(F32)).astype(BF16),
        w_out=w_out.astype(BF16),
        conv_w=conv_w, conv_b=conv_b.reshape(N_A, 1, D_MAIN),
        w_gate=_gate_tiles(0.5 * lru_wa, 0.5 * lru_wx),
        ba=0.5 * lru_ba.reshape(N_A, 1, D_MAIN), bx=0.5 * lru_bx.reshape(N_A, 1, D_MAIN),
        lam=lru_lambda.reshape(N_A, 1, D_MAIN),
        sinks=sinks, sinks_col=sinks.reshape(DEPTH - N_A, N_HEADS, 1),
        kv_norm_g=kv_norm_g.reshape(1, D_MODEL), w_kv=w_kv.astype(BF16),
        final_g=final_g.reshape(1, D_MODEL),
        inv128=jnp.tile(inv, LANES // half).reshape(1, LANES),
        sign128=jnp.tile(sign, LANES // HEAD_DIM).reshape(1, LANES))


def _rope_table_kernel(inv_ref, sign_ref, cos_ref, sin_ref):
    rows = cos_ref.shape[0]
    bases = rows // WINDOW
    inv = inv_ref[...]
    offset = lax.broadcasted_iota(jnp.int32, (WINDOW, LANES), 0).astype(F32) * inv
    cos_o, sin_o = jnp.cos(offset), jnp.sin(offset)
    base_pos = pl.program_id(0) * rows + WINDOW * lax.broadcasted_iota(jnp.int32, (bases, LANES), 0)
    base = base_pos.astype(F32) * inv
    cos_b, sin_b = jnp.cos(base), jnp.sin(base)
    for s in range(bases):
        blk = slice(WINDOW * s, WINDOW * (s + 1))
        cb, sb = cos_b[s:s + 1, :], sin_b[s:s + 1, :]
        cos_ref[blk, :] = cb * cos_o - sb * sin_o
        sin_ref[blk, :] = (sb * cos_o + cb * sin_o) * sign_ref[...]


def _rope_tables(seq, inv128, sign128, rows=1024):
    rows = min(rows, seq)
    return pl.pallas_call(
        _rope_table_kernel,
        grid=(seq // rows,),
        in_specs=[_const_spec((1, LANES)), _const_spec((1, LANES))],
        out_specs=[pl.BlockSpec((rows, LANES), lambda i: (i, 0)),
                   pl.BlockSpec((rows, LANES), lambda i: (i, 0))],
        out_shape=[jax.ShapeDtypeStruct((seq, LANES), F32)] * 2,
        name="rope_tables",
    )(inv128, sign128)


def _mem_kv_kernel(m_ref, g_ref, w_ref, k_ref, v_ref, kb_ref, vb_ref):
    batch = m_ref.shape[0]
    mn = _rms(m_ref[...].reshape(batch * N_MEM, D_MODEL), g_ref[0]).astype(BF16)
    kv = jnp.dot(mn, w_ref[0].astype(BF16), preferred_element_type=F32)
    for b in range(batch):
        rows = slice(N_MEM * b, N_MEM * (b + 1))
        k, v = kv[rows, :D_MEMW], kv[rows, D_MEMW:]
        for hh in range(MEM_HEADS):
            cols = slice(MEM_HD * hh, MEM_HD * (hh + 1))
            k_ref[0, b, pl.ds(hh, N_MEM, stride=MEM_HEADS), :] = k[:, cols]
            v_ref[0, b, pl.ds(hh, N_MEM, stride=MEM_HEADS), :] = v[:, cols]
        kb_ref[0, b] = k.astype(BF16)
        vb_ref[0, b] = v.T.astype(BF16)


def _mem_kv(mem_prompt, mem_norm_g, w_mem_kv):
    batch = mem_prompt.shape[0]
    layer_blk = lambda *shape: pl.BlockSpec((1,) + shape, lambda l: (l,) + (0,) * len(shape))
    return pl.pallas_call(
        _mem_kv_kernel,
        grid=(DEPTH,),
        in_specs=[_const_spec((batch, N_MEM, D_MODEL)), layer_blk(1, D_MODEL),
                  layer_blk(D_MODEL, 2 * D_MEMW)],
        out_specs=[layer_blk(batch, N_MEM * MEM_HEADS, MEM_HD), layer_blk(batch, N_MEM * MEM_HEADS, MEM_HD),
                   layer_blk(batch, N_MEM, D_MEMW), layer_blk(batch, D_MEMW, N_MEM)],
        out_shape=[jax.ShapeDtypeStruct((DEPTH, batch, N_MEM * MEM_HEADS, MEM_HD), F32)] * 2
                  + [jax.ShapeDtypeStruct((DEPTH, batch, N_MEM, D_MEMW), BF16),
                     jax.ShapeDtypeStruct((DEPTH, batch, D_MEMW, N_MEM), BF16)],
        compiler_params=pltpu.CompilerParams(vmem_limit_bytes=VMEM_LIMIT_BYTES),
        name="mem_kv",
    )(mem_prompt, mem_norm_g.reshape(DEPTH, 1, D_MODEL), w_mem_kv)


def _prompt_lru_kernel(tile, make_side, x_ref, ng_ref, win_ref, wout_ref, cw_ref, cb_ref, wg_ref, ba_ref,
                       bx_ref, lam_ref, mk_ref, mv_ref, xo_ref, hl_ref, ct_ref,
                       tails, a3, b3, h3, hc, z, hn_s, proj_s, uc_s):
    half = tile // 2
    t = pl.program_id(1)
    side_begin, side_thunks, side_end = make_side(dict(
        ng=ng_ref, win=win_ref, wout=wout_ref, cw=cw_ref, cb=cb_ref, wg=wg_ref, ba=ba_ref, bx=bx_ref, lam=lam_ref))

    @pl.when(t == 0)
    def _():
        tails[...] = jnp.zeros((SUBLANES, D_MAIN), F32)
        hc[...] = jnp.zeros((SUBLANES, LANES), F32)

    side_begin()

    decay_c = _decay_exponent(lam_ref[...])
    carry = [hc[...]]


    project = lambda r0: _project_thunks(x_ref, ng_ref, win_ref, hn_s, proj_s, r0, half)

    def gates(r0):
        rows = slice(r0, r0 + half)
        g0, gn = r0 // SUBLANES, half // SUBLANES

        def conv():
            u = proj_s[rows, 0:D_MAIN]
            first_row = lax.broadcasted_iota(jnp.int32, (half, D_MAIN), 0) == 0
            acc = u * cw_ref[0:1, :]
            for k in range(1, CONV_W):
                delayed = jnp.where(first_row, tails[k - 1:k, :], pltpu.roll(acc, 1, 0))
                tails[k - 1:k, :] = acc[half - 1:half, :]
                acc = delayed + u * cw_ref[k:k + 1, :]
            uc_s[rows, :] = acc + cb_ref[...]

        def gate(j):
            def run():
                sl = slice(GATE_TILE * j, GATE_TILE * (j + 1))
                uc = uc_s[rows, sl]
                pre = jnp.dot(uc.astype(BF16), wg_ref[j], preferred_element_type=F32)
                a, b = _rglru_coeffs(uc, pre, ba_ref[:, sl], bx_ref[:, sl], decay_c[:, sl])
                for jj in range(GATE_TILE // LANES):
                    slab = j * (GATE_TILE // LANES) + jj
                    srows = slice(SUBLANES * slab, SUBLANES * (slab + 1))
                    cols = slice(LANES * jj, LANES * (jj + 1))
                    a3[g0:g0 + gn, srows, :] = a[:, cols].reshape(gn, SUBLANES, LANES)
                    b3[g0:g0 + gn, srows, :] = b[:, cols].reshape(gn, SUBLANES, LANES)
            return run

        return [conv] + [gate(j) for j in range(N_GATE_TILES)]

    def scan(r0):
        def piece(first, count):
            def run():
                h = carry[0]
                for g in range(first, first + count):
                    for r in range(SUBLANES):
                        step = pl.ds(r, N_SLABS, stride=SUBLANES)
                        h = a3[g, step, :] * h + b3[g, step, :]
                        h3[g, step, :] = h
                carry[0] = h
            return run

        g0, gn = r0 // SUBLANES, half // SUBLANES
        count = max(gn // SCAN_CHUNKS, 1)
        return [piece(g, count) for g in range(g0, g0 + gn, count)]

    def gating(r0):
        def slab(j):
            def run():
                g0, gn = r0 // SUBLANES, half // SUBLANES
                y = h3[g0:g0 + gn, SUBLANES * j:SUBLANES * (j + 1), :].reshape(half, LANES)
                g_main = proj_s[r0:r0 + half, D_MAIN + LANES * j:D_MAIN + LANES * (j + 1)]
                z[r0:r0 + half, LANES * j:LANES * (j + 1)] = (y * _silu_of_half(g_main)).astype(BF16)
            return run
        return [slab(j) for j in range(N_SLABS)]

    out_project = lambda r0: _out_project_thunks(x_ref, z, wout_ref, xo_ref, r0, half)
    mem =lambda r0: _skewed(_mem_units(proj_s, mk_ref, mv_ref, z, r0, half), ATTN_SKEW)
    first, second = 0, half
    n_conv_in = 1 + D_MAIN // PROJ_CHUNK
    p_first, p_second = project(first), project(second)
    _run(p_first[:n_conv_in])
    _run(_interleave(p_first[n_conv_in:], gates(first)))
    _run(p_second[:n_conv_in])
    _run(_interleave(p_second[n_conv_in:], gates(second), mem(first), scan(first)))
    _run(_interleave(gating(first) + out_project(first), mem(second), scan(second), side_thunks))
    _run(gating(second))
    _run(out_project(second))

    h = carry[0]
    hc[...] = h

    @pl.when(t == pl.num_programs(1) - 1)
    def _():
        hl_ref[0] = h
        ct_ref[0] = proj_s[tile - SUBLANES:tile, 0:D_MAIN]

    side_end()


def _prompt_lru_part(x, W, layer, mk, mv_t, tile=256):
    batch, seq, _ = x.shape
    tok = lambda b, t: (b, t, 0)
    per_b = lambda b, t: (b, 0, 0)
    mem = lambda b, t: (layer, b, 0, 0)
    in_specs = [
        pl.BlockSpec((1, tile, D_MODEL), tok),
        _layer_spec((1, D_MODEL), layer), _layer_spec((D_MODEL, D_IN), layer),
        _layer_spec((D_CAT, D_MODEL), layer),
        _layer_spec((CONV_W, D_MAIN), layer), _layer_spec((1, D_MAIN), layer),
        _layer_spec((N_GATE_TILES, GATE_TILE, 2 * GATE_TILE), layer),
        _layer_spec((1, D_MAIN), layer), _layer_spec((1, D_MAIN), layer), _layer_spec((1, D_MAIN), layer),
        pl.BlockSpec((None, 1, N_MEM, D_MEMW), mem), pl.BlockSpec((None, 1, D_MEMW, N_MEM), mem),
    ]
    args = [x, W["norm_g"], W["w_in"], W["w_out"], W["conv_w"], W["conv_b"], W["w_gate"],
            W["ba"], W["bx"], W["lam"], mk, mv_t]
    out_specs = [pl.BlockSpec((1, tile, D_MODEL), tok),
                 pl.BlockSpec((1, SUBLANES, LANES), per_b),
                 pl.BlockSpec((1, SUBLANES, D_MAIN), per_b)]
    out_shape = [jax.ShapeDtypeStruct((batch, seq, D_MODEL), F32),
                 jax.ShapeDtypeStruct((batch, SUBLANES, LANES), F32),
                 jax.ShapeDtypeStruct((batch, SUBLANES, D_MAIN), F32)]
    groups = tile // SUBLANES
    return dict(
        body=functools.partial(_prompt_lru_kernel, tile), grid=(batch, seq // tile),
        in_specs=in_specs, args=args, out_specs=out_specs, out_shape=out_shape,
        scratch=[pltpu.VMEM((SUBLANES, D_MAIN), F32),
                 pltpu.VMEM((groups, SUBLANES * N_SLABS, LANES), F32),
                 pltpu.VMEM((groups, SUBLANES * N_SLABS, LANES), F32),
                 pltpu.VMEM((groups, SUBLANES * N_SLABS, LANES), F32),
                 pltpu.VMEM((SUBLANES, LANES), F32),
                 pltpu.VMEM((tile, D_CAT), BF16),
                 pltpu.VMEM((tile, D_MODEL), BF16),
                 pltpu.VMEM((tile, D_IN), F32),
                 pltpu.VMEM((tile, D_MAIN), F32)],
        name="lru")


def _sink_softmax(s, sink, axis=-1):
    m = jnp.maximum(jnp.max(s, axis=axis, keepdims=True), sink)
    p = jnp.exp2(s - m)
    den = jnp.sum(p, axis=axis, keepdims=True) + jnp.exp2(sink - m)
    return p, den


def _prompt_swa_kernel(make_kv, final, tile, sink_row, make_side, *refs):
    it = iter(refs)
    take = lambda n: [next(it) for _ in range(n)]
    x_ref, ng_ref, win_ref, wout_ref, mk_ref, mv_ref, sink_ref = take(7)
    if make_kv:
        kvg_ref, wkv_ref = take(2)
    else:
        kp_ref, kc_ref, vp_ref, vc_ref = take(4)
    cos_ref, sin_ref = take(2)
    fg_ref = take(1)[0] if final else None
    xo_ref, = take(1)
    if make_kv:
        ko_ref, vo_ref = take(2)
    if final:
        y_ref, = take(1)
    z, qz, kcat, vcat_t, hn_s, proj_s = take(6)
    half = tile // 2
    t = pl.program_id(1)
    side_begin, side_thunks, side_end = make_side(dict(ng=ng_ref, win=win_ref, wout=wout_ref, fg=fg_ref))
    side_begin()

    def rope_queries(r0):
        rows = slice(r0, r0 + half)

        def slab(j):
            def run():
                cos = cos_ref[rows, :]
                sin = sin_ref[rows, :]
                lane = lax.broadcasted_iota(jnp.int32, (half, LANES), 1)
                lo = lane < HEAD_DIM
                first_half = (lane % HEAD_DIM) < (HEAD_DIM // 2)
                c = j // (GROUP // 2)
                xq = proj_s[rows, LANES * j:LANES * (j + 1)] * (HEAD_DIM ** -0.5 * LOG2_E)
                r32 = pltpu.roll(xq, HEAD_DIM // 2, 1)
                r96 = pltpu.roll(xq, LANES - HEAD_DIM // 2, 1)
                r64 = pltpu.roll(xq, HEAD_DIM, 1)
                stay = xq * cos + jnp.where(first_half, r96, r32) * sin
                move = r64 * cos + jnp.where(first_half, r32, r96) * sin
                keep = lo if c == 0 else ~lo
                even, odd = (stay, move) if c == 0 else (move, stay)
                qz[rows, 2 * LANES * j:2 * LANES * j + LANES] = jnp.where(keep, even, 0.0).astype(BF16)
                qz[rows, 2 * LANES * j + LANES:2 * LANES * (j + 1)] = jnp.where(keep, odd, 0.0).astype(BF16)
            return run

        return [slab(j) for j in range(N_SLABS)]

    if make_kv:
        @pl.when(t == 0)
        def _():
            kcat[0:WINDOW, :] = jnp.zeros((WINDOW, D_KV), BF16)
            vcat_t[:, 0:WINDOW] = jnp.zeros((D_KV, WINDOW), BF16)

        @pl.when(t > 0)
        def _():
            kcat[0:WINDOW, :] = kcat[tile:tile + WINDOW, :]
            vcat_t[:, 0:WINDOW] = vcat_t[:, tile:tile + WINDOW]

        def shared_kv(r0):
            def run():
                rows = slice(r0, r0 + half)
                kvn = _rms(x_ref[0, rows, :], kvg_ref[...]).astype(BF16)
                kv = jnp.dot(kvn, wkv_ref[...], preferred_element_type=F32)
                k = _rope(kv[:, :D_KV], cos_ref[rows, :], sin_ref[rows, :])
                v = kv[:, D_KV:]
                ko_ref[0, rows, :] = k
                vo_ref[0, rows, :] = v
                kcat[WINDOW + r0:WINDOW + r0 + half, :] = k.astype(BF16)
                vcat_t[:, WINDOW + r0:WINDOW + r0 + half] = v.T.astype(BF16)
            return [run]
    else:
        kcat[0:WINDOW, :] = kp_ref[0].astype(BF16)
        kcat[WINDOW:, :] = kc_ref[0].astype(BF16)
        vcat_t[:, 0:WINDOW] = vp_ref[0].T.astype(BF16)
        vcat_t[:, WINDOW:] = vc_ref[0].T.astype(BF16)
        shared_kv = lambda r0: []

    key_r = lax.broadcasted_iota(jnp.int32, (WINDOW, 2 * WINDOW), 0)
    query = lax.broadcasted_iota(jnp.int32, (WINDOW, 2 * WINDOW), 1) % WINDOW
    from_prev = key_r > query

    def scores(n, j):
        rows = slice(WINDOW * n, WINDOW * (n + 1))
        keys = kcat[WINDOW * n:WINDOW * (n + 2), :]
        q2 = jnp.concatenate([qz[rows, 2 * LANES * j:2 * LANES * j + LANES],
                              qz[rows, 2 * LANES * j + LANES:2 * LANES * (j + 1)]], axis=0)
        s = lax.dot_general(keys, q2, _NT, preferred_element_type=F32)
        s_prev, s_own = s[:WINDOW, :], s[WINDOW:, :]
        if n == 0:
            s_prev = jnp.where(t > 0, s_prev, NEG)
        visible = jnp.where(from_prev, s_prev, s_own)
        p0, den0 = _sink_softmax(visible[:, :WINDOW], sink_ref[sink_row, 2 * j] * LOG2_E, axis=0)
        p1, den1 = _sink_softmax(visible[:, WINDOW:], sink_ref[sink_row, 2 * j + 1] * LOG2_E, axis=0)
        p = jnp.concatenate([p0, p1], axis=1).astype(BF16)
        zero = jnp.zeros_like(p)
        return jnp.concatenate([jnp.where(from_prev, p, zero), jnp.where(from_prev, zero, p)], axis=0), den0, den1

    def outputs(n, j, p, den0, den1):
        rows = slice(WINDOW * n, WINDOW * (n + 1))
        c = j // (GROUP // 2)
        vals_t = vcat_t[HEAD_DIM * c:HEAD_DIM * (c + 1), WINDOW * n:WINDOW * (n + 2)]
        o_t = jnp.dot(vals_t, p, preferred_element_type=F32)
        y_t = jnp.concatenate([o_t[:, :WINDOW] / den0, o_t[:, WINDOW:] / den1], axis=0)
        g_main = proj_s[rows, D_MAIN + LANES * j:D_MAIN + LANES * (j + 1)]
        z[rows, LANES * j:LANES * (j + 1)] = (y_t.T * _silu_of_half(g_main)).astype(BF16)

    def attend(r0):
        units = [(functools.partial(scores, n, j), functools.partial(outputs, n, j))
                 for n in range(r0 // WINDOW, (r0 + half) // WINDOW) for j in range(N_SLABS)]
        return _skewed(units + _mem_units(proj_s, mk_ref, mv_ref, z, r0, half), ATTN_SKEW)

    def out_project(r0):
        rows = slice(r0, r0 + half)

        def final_norm():
            y_ref[0, rows, :] = _rms(xo_ref[0, rows, :], fg_ref[...])

        return _out_project_thunks(x_ref, z, wout_ref, xo_ref, r0, half) + ([final_norm] if final else [])

    project = lambda r0: _project_thunks(x_ref, ng_ref, win_ref, hn_s, proj_s, r0, half)
    first, second = 0, half
    _run(project(first) + shared_kv(first))
    _run(_interleave(project(second) + shared_kv(second), rope_queries(first)))
    _run(_interleave(attend(first), rope_queries(second)))
    _run(_interleave(attend(second), out_project(first), side_thunks))
    _run(out_project(second))
    side_end()


def _prompt_swa_part(x, W, layer, mk, mv_t, kv, cos, sin, final_g=None, tile=256):
    batch, seq, _ = x.shape
    final = final_g is not None
    make_kv = kv is None
    blocks = tile // WINDOW
    tok = lambda b, t: (b, t, 0)
    mem = lambda b, t: (layer, b, 0, 0)
    prev_blk = lambda b, t: (b, jnp.maximum(t * blocks - 1, 0), 0)
    in_specs = [
        pl.BlockSpec((1, tile, D_MODEL), tok),
        _layer_spec((1, D_MODEL), layer), _layer_spec((D_MODEL, D_IN), layer),
        _layer_spec((D_CAT, D_MODEL), layer),
        pl.BlockSpec((None, 1, N_MEM, D_MEMW), mem), pl.BlockSpec((None, 1, D_MEMW, N_MEM), mem),
        pl.BlockSpec(memory_space=pltpu.SMEM)]
    args = [x, W["norm_g"], W["w_in"], W["w_out"], mk, mv_t, W["sinks"]]
    if make_kv:
        in_specs += [_const_spec((1, D_MODEL)), _const_spec((D_MODEL, 2 * D_KV))]
        args += [W["kv_norm_g"], W["w_kv"]]
    else:
        k, v = kv
        in_specs += [pl.BlockSpec((1, WINDOW, D_KV), prev_blk), pl.BlockSpec((1, tile, D_KV), tok),
                     pl.BlockSpec((1, WINDOW, D_KV), prev_blk), pl.BlockSpec((1, tile, D_KV), tok)]
        args += [k, k, v, v]
    in_specs += [pl.BlockSpec((tile, LANES), lambda b, t: (t, 0)), pl.BlockSpec((tile, LANES), lambda b, t: (t, 0))]
    args += [cos, sin]
    out_specs = [pl.BlockSpec((1, tile, D_MODEL), tok)]
    out_shape = [jax.ShapeDtypeStruct((batch, seq, D_MODEL), F32)]
    if make_kv:
        out_specs += [pl.BlockSpec((1, tile, D_KV), tok)] * 2
        out_shape += [jax.ShapeDtypeStruct((batch, seq, D_KV), F32)] * 2
    if final:
        in_specs.append(_const_spec((1, D_MODEL)))
        args.append(final_g)
        out_specs.append(pl.BlockSpec((1, tile, D_MODEL), tok))
        out_shape.append(jax.ShapeDtypeStruct((batch, seq, D_MODEL), F32))
    return dict(
        body=functools.partial(_prompt_swa_kernel, make_kv, final, tile, layer - N_A), grid=(batch, seq // tile),
        in_specs=in_specs, args=args, out_specs=out_specs, out_shape=out_shape,
        scratch=[pltpu.VMEM((tile, D_CAT), BF16),
                 pltpu.VMEM((tile, 2 * D_MAIN), BF16),
                 pltpu.VMEM((WINDOW + tile, D_KV), BF16),
                 pltpu.VMEM((D_KV, WINDOW + tile), BF16),
                 pltpu.VMEM((tile, D_MODEL), BF16),
                 pltpu.VMEM((tile, D_IN), F32)],
        name="swa" + ("_kv" if make_kv else "") + ("_final" if final else ""))


def _sample_mem_unit(b, bb, qm_s, mk_ref, mv_ref, om_s):
    def scores():
        row = lax.broadcasted_iota(jnp.int32, (SUBLANES, MEM_HD), 0)
        q = qm_s[pl.ds(b, 1), :]
        qh = jnp.zeros((SUBLANES, MEM_HD), F32)
        for hh in range(MEM_HEADS):
            q_head = jnp.broadcast_to(q[:, MEM_HD * hh:MEM_HD * (hh + 1)], (SUBLANES, MEM_HD))
            qh = jnp.where(row == hh, q_head, qh)
        s = lax.dot_general(qh.astype(BF16), mk_ref[bb].astype(BF16), _NT,
                            preferred_element_type=F32) * (MEM_HD ** -0.5 * LOG2_E)
        srow = lax.broadcasted_iota(jnp.int32, s.shape, 0)
        scol = lax.broadcasted_iota(jnp.int32, s.shape, 1)
        s = jnp.where((scol % MEM_HEADS) == srow, s, NEG)
        m = jnp.max(s, axis=-1, keepdims=True)
        p = jnp.exp2(s - m)
        return p.astype(BF16), jnp.sum(p, axis=-1, keepdims=True)

    def outputs(p, den):
        o = jnp.dot(p, mv_ref[bb].astype(BF16), preferred_element_type=F32) / den
        om_s[pl.ds(b, 1), :] = jnp.concatenate([o[hh:hh + 1, :] for hh in range(MEM_HEADS)], axis=1)

    return scores, outputs


def _sample_lru_side(seqs, step, n_steps, w, x_ref, h0_ref, c0_ref, mk_ref, mv_ref, xo_ref, ho_ref, co_ref,
                     z_s, qm_s, gm_s, om_s):
    def begin():
        @pl.when(step == 0)
        def _():
            hn = _rms(x_ref[...], w["ng"][...]).astype(BF16)
            proj = jnp.dot(hn, w["win"][...], preferred_element_type=F32)
            u = proj[:, :D_MAIN]
            taps = [c0_ref[k] for k in range(CONV_W - 1)] + [u]
            uc = w["cb"][...] + taps[0] * w["cw"][0:1, :]
            for k in range(1, CONV_W):
                uc = uc + taps[k] * w["cw"][k:k + 1, :]
            for k in range(CONV_W - 1):
                co_ref[k] = taps[k + 1]
            decay_c = _decay_exponent(w["lam"][...])
            ucb = uc.astype(BF16)
            for j in range(N_GATE_TILES):
                sl = slice(GATE_TILE * j, GATE_TILE * (j + 1))
                pre = jnp.dot(ucb[:, sl], w["wg"][j], preferred_element_type=F32)
                a, b = _rglru_coeffs(uc[:, sl], pre, w["ba"][:, sl], w["bx"][:, sl], decay_c[:, sl])
                h = a * h0_ref[:, sl] + b
                ho_ref[:, sl] = h
                z_s[:, sl] = (h * _silu_of_half(
                    proj[:, D_MAIN + GATE_TILE * j:D_MAIN + GATE_TILE * (j + 1)])).astype(BF16)
            qm_s[...] = proj[:, 2 * D_MAIN:2 * D_MAIN + D_MEMW]
            gm_s[...] = _silu_of_half(proj[:, 2 * D_MAIN + D_MEMW:])

    thunks = _skewed([_sample_mem_unit(step * seqs + bb, bb, qm_s, mk_ref, mv_ref, om_s) for bb in range(seqs)],
                     ATTN_SKEW)

    def end():
        @pl.when(step == n_steps - 1)
        def _():
            z_s[:, D_MAIN:] = (om_s[...] * gm_s[...]).astype(BF16)
            xo_ref[...] = x_ref[...] + jnp.dot(z_s[...], w["wout"][...], preferred_element_type=F32)

    return begin, thunks, end


def _cache_spec(layer, seqs, steps_per_row):
    return pl.BlockSpec((None, seqs, N_MEM * MEM_HEADS, MEM_HD),
                        lambda b, t: (layer, b * steps_per_row + t, 0, 0))


def _whole_spec(shape):
    zeros = (0,) * len(shape)
    return pl.BlockSpec(shape, lambda *_: zeros)


def _sample_lru_part(x, layer, state_h, state_conv_t, cache_k, cache_v, grid):
    batch = x.shape[0]
    seqs = batch // (grid[0] * grid[1])
    cache_spec = _cache_spec(layer, seqs, grid[1])
    return dict(
        body=functools.partial(_sample_lru_side, seqs),
        in_specs=[_const_spec((batch, D_MODEL)),
                  _layer_spec((batch, D_MAIN), layer), _layer_spec((CONV_W - 1, batch, D_MAIN), layer),
                  cache_spec, cache_spec],
        args=[x, state_h, state_conv_t, cache_k, cache_v],
        out_specs=[_whole_spec((batch, D_MODEL)), _whole_spec((batch, D_MAIN)),
                   _whole_spec((CONV_W - 1, batch, D_MAIN))],
        out_shape=[jax.ShapeDtypeStruct((batch, D_MODEL), F32),
                   jax.ShapeDtypeStruct((batch, D_MAIN), F32),
                   jax.ShapeDtypeStruct((CONV_W - 1, batch, D_MAIN), F32)],
        scratch=[pltpu.VMEM((batch, D_CAT), BF16), pltpu.VMEM((batch, D_MEMW), F32),
                 pltpu.VMEM((batch, D_MEMW), F32), pltpu.VMEM((batch, D_MEMW), F32)])


def _sample_swa_side(first, final, seqs, step, n_steps, w, *refs):
    (x_ref, sink_ref, mk_ref, mv_ref, wk_ref, wv_ref) = refs[:6]
    pos = 6
    if first:
        kvg_ref, wkv_ref = refs[pos:pos + 2]
        pos += 2
    inv_ref, sign_ref = refs[pos:pos + 2]
    pos += 2
    xo_ref = refs[pos]
    pos += 1
    if first:
        wko_ref, wvo_ref = refs[pos:pos + 2]
        pos += 2
    if final:
        y_ref = refs[pos]
        pos += 1
    z_s, q_s, gmain_s, y_s, qm_s, gm_s, om_s, kn_s, vn_s = refs[pos:]

    def begin():
        @pl.when(step == 0)
        def _():
            x = x_ref[...]
            hn = _rms(x, w["ng"][...]).astype(BF16)
            proj = jnp.dot(hn, w["win"][...], preferred_element_type=F32)
            ang = float(PAST_LEN) * inv_ref[...]
            cos = jnp.cos(ang)
            sin = jnp.sin(ang) * sign_ref[...]
            for j in range(N_SLABS):
                cols = slice(LANES * j, LANES * (j + 1))
                q_s[:, cols] = _rope(proj[:, cols], cos, sin)
            gmain_s[...] = _silu_of_half(proj[:, D_MAIN:2 * D_MAIN])
            qm_s[...] = proj[:, 2 * D_MAIN:2 * D_MAIN + D_MEMW]
            gm_s[...] = _silu_of_half(proj[:, 2 * D_MAIN + D_MEMW:])
            if first:
                kvn = _rms(x, kvg_ref[...]).astype(BF16)
                kv = jnp.dot(kvn, wkv_ref[...], preferred_element_type=F32)
                kn_s[...] = _rope(kv[:, :D_KV], cos, sin).T
                vn_s[...] = kv[:, D_KV:].T

    row = lax.broadcasted_iota(jnp.int32, (N_HEADS, LANES), 0)
    lane = lax.broadcasted_iota(jnp.int32, (N_HEADS, LANES), 1)
    kv_half = (lane // HEAD_DIM) == (row // GROUP)
    in_place = (row % 2) == (row // GROUP)
    lane1 = lax.broadcasted_iota(jnp.int32, (1, LANES), 1)
    seq_lane = lax.broadcasted_iota(jnp.int32, kn_s.shape, 1)
    slot = lax.broadcasted_iota(jnp.int32, (D_KV, WINDOW), 1)

    def slide(win_t, new_t_s, b):
        if new_t_s.shape[1] == WINDOW:
            col = pltpu.roll(new_t_s[...], WINDOW - 1 - b, 1)
        else:
            col = jnp.sum(jnp.where(seq_lane == b, new_t_s[...], 0.0), axis=1, keepdims=True)
        return jnp.where(slot == WINDOW - 1, col, pltpu.roll(win_t, WINDOW - 1, 1))

    def window_unit(b, bb):
        def scores():
            if first:
                kw_t = slide(wk_ref[bb], kn_s, b)
                wko_ref[bb] = kw_t
            else:
                kw_t = wk_ref[bb]
            q = jnp.broadcast_to(q_s[pl.ds(b, 1), :], (N_HEADS, D_MAIN))
            e = jnp.zeros((N_HEADS, LANES), F32)
            for j in range(N_SLABS):
                e = jnp.where((row // 2) == j, q[:, LANES * j:LANES * (j + 1)], e)
            qh = jnp.where(kv_half, jnp.where(in_place, e, pltpu.roll(e, HEAD_DIM, 1)), 0.0).astype(BF16)
            s = jnp.dot(qh, kw_t.astype(BF16), preferred_element_type=F32) * (HEAD_DIM ** -0.5 * LOG2_E)
            p, den = _sink_softmax(s, sink_ref[...] * LOG2_E)
            return p.astype(BF16), den

        def outputs(p, den):
            if first:
                vw_t = slide(wv_ref[bb], vn_s, b)
                wvo_ref[bb] = vw_t
            else:
                vw_t = wv_ref[bb]
            o = lax.dot_general(p, vw_t.astype(BF16), _NT, preferred_element_type=F32) / den
            f = jnp.where(in_place, o, pltpu.roll(o, HEAD_DIM, 1))
            y_s[pl.ds(b, 1), :] = jnp.concatenate(
                [jnp.where(lane1 < HEAD_DIM, f[2 * j:2 * j + 1, :], f[2 * j + 1:2 * j + 2, :])
                 for j in range(N_SLABS)], axis=1)

        return scores, outputs

    units = []
    for bb in range(seqs):
        units.append(window_unit(step * seqs + bb, bb))
        units.append(_sample_mem_unit(step * seqs + bb, bb, qm_s, mk_ref, mv_ref, om_s))

    def end():
        @pl.when(step == n_steps - 1)
        def _():
            z_s[:, :D_MAIN] = (y_s[...] * gmain_s[...]).astype(BF16)
            z_s[:, D_MAIN:] = (om_s[...] * gm_s[...]).astype(BF16)
            xn = x_ref[...] + jnp.dot(z_s[...], w["wout"][...], preferred_element_type=F32)
            xo_ref[...] = xn
            if final:
                y_ref[...] = _rms(xn, w["fg"][...])

    return begin, _skewed(units, ATTN_SKEW), end


def _sample_swa_part(x, W, layer, cache_k, cache_v, win_k_t, win_v_t, first, final, grid):
    batch = x.shape[0]
    seqs = batch // (grid[0] * grid[1])
    cache_spec = _cache_spec(layer, seqs, grid[1])
    win_spec = pl.BlockSpec((seqs, D_KV, WINDOW), lambda b, t: (b * grid[1] + t, 0, 0))
    in_specs = [_const_spec((batch, D_MODEL)), _layer_spec((N_HEADS, 1), layer - N_A),
                cache_spec, cache_spec, win_spec, win_spec]
    args = [x, W["sinks_col"], cache_k, cache_v, win_k_t, win_v_t]
    if first:
        in_specs += [_const_spec((1, D_MODEL)), _const_spec((D_MODEL, 2 * D_KV))]
        args += [W["kv_norm_g"], W["w_kv"]]
    in_specs += [_const_spec((1, LANES)), _const_spec((1, LANES))]
    args += [W["inv128"], W["sign128"]]
    out_specs = [_whole_spec((batch, D_MODEL))]
    out_shape = [jax.ShapeDtypeStruct((batch, D_MODEL), F32)]
    if first:
        out_specs += [win_spec, win_spec]
        out_shape += [jax.ShapeDtypeStruct((batch, D_KV, WINDOW), F32)] * 2
    if final:
        out_specs.append(_whole_spec((batch, D_MODEL)))
        out_shape.append(jax.ShapeDtypeStruct((batch, D_MODEL), F32))
    return dict(
        body=functools.partial(_sample_swa_side, first, final, seqs),
        in_specs=in_specs, args=args, out_specs=out_specs, out_shape=out_shape,
        scratch=[pltpu.VMEM((batch, D_CAT), BF16), pltpu.VMEM((batch, D_MAIN), F32),
                 pltpu.VMEM((batch, D_MAIN), F32), pltpu.VMEM((batch, D_MAIN), F32),
                 pltpu.VMEM((batch, D_MEMW), F32), pltpu.VMEM((batch, D_MEMW), F32),
                 pltpu.VMEM((batch, D_MEMW), F32),
                 pltpu.VMEM((D_KV, batch), F32), pltpu.VMEM((D_KV, batch), F32)])


def _layer_kernel(prompt_body, sample_body, n_prompt, n_sample, *refs):
    it = iter(refs)
    take = lambda n: [next(it) for _ in range(n)]
    p_in, s_in = take(n_prompt[0]), take(n_sample[0])
    p_out, s_out = take(n_prompt[1]), take(n_sample[1])
    p_scr, s_scr = take(n_prompt[2]), take(n_sample[2])
    step = pl.program_id(0) * pl.num_programs(1) + pl.program_id(1)
    n_steps = pl.num_programs(0) * pl.num_programs(1)
    make_side = lambda weights: sample_body(step, n_steps, weights, *s_in, *s_out, *s_scr)
    prompt_body(make_side, *p_in, *p_out, *p_scr)


def _layer_call(prompt, sample):
    counts = lambda part: (len(part["in_specs"]), len(part["out_specs"]), len(part["scratch"]))
    outs = pl.pallas_call(
        functools.partial(_layer_kernel, prompt["body"], sample["body"], counts(prompt), counts(sample)),
        grid=prompt["grid"],
        in_specs=prompt["in_specs"] + sample["in_specs"],
        out_specs=prompt["out_specs"] + sample["out_specs"],
        out_shape=prompt["out_shape"] + sample["out_shape"],
        scratch_shapes=prompt["scratch"] + sample["scratch"],
        compiler_params=pltpu.CompilerParams(
            dimension_semantics=("arbitrary", "arbitrary"), vmem_limit_bytes=VMEM_LIMIT_BYTES),
        name="layer_" + prompt["name"],
    )(*prompt["args"], *sample["args"])
    n = len(prompt["out_specs"])
    return outs[:n], outs[n:]


def _trunks(xp, xs, W, mkb, mvb_t, cache_mem_k, cache_mem_v, state_h, state_conv, win_k, win_v, tile=512):
    pb, seq, _ = xp.shape
    sb = xs.shape[0]
    grid = (pb, seq // tile)
    cos, sin = _rope_tables(seq, W["inv128"], W["sign128"])
    xs = xs.reshape(sb, D_MODEL)
    ck = cache_mem_k.reshape(DEPTH, sb, N_MEM * MEM_HEADS, MEM_HD)
    cv = cache_mem_v.reshape(DEPTH, sb, N_MEM * MEM_HEADS, MEM_HD)
    conv_t = jnp.transpose(state_conv, (0, 2, 1, 3))
    wk = jnp.transpose(win_k, (0, 2, 3, 1)).reshape(sb, D_KV, WINDOW)
    wv = jnp.transpose(win_v, (0, 2, 3, 1)).reshape(sb, D_KV, WINDOW)
    hp, cp, hs, cs = [], [], [], []
    for l in range(N_A):
        p_out, s_out = _layer_call(
            _prompt_lru_part(xp, W, l, mkb, mvb_t, tile=tile),
            _sample_lru_part(xs, l, state_h, conv_t, ck, cv, grid))
        xp, h_last, conv_tail = p_out
        hp.append(h_last.reshape(pb, D_MAIN))
        cp.append(conv_tail[:, SUBLANES - (CONV_W - 1):, :])
        xs, h, c = s_out
        hs.append(h)
        cs.append(c)
    yp = ys = kv = None
    for l in range(N_A, DEPTH):
        first, last = l == N_A, l == DEPTH - 1
        p_out, s_out = _layer_call(
            _prompt_swa_part(xp, W, l, mkb, mvb_t, kv, cos, sin, final_g=W["final_g"] if last else None, tile=tile),
            _sample_swa_part(xs, W, l, ck, cv, wk, wv, first, last, grid))
        xp, xs = p_out[0], s_out[0]
        if first:
            kv = (p_out[1], p_out[2])
            wk, wv = s_out[1], s_out[2]
        if last:
            yp, ys = p_out[-1], s_out[-1]
    k, v = kv
    win_kp = k[:, seq - WINDOW:, :].reshape(pb, WINDOW, N_KV, HEAD_DIM)
    win_vp = v[:, seq - WINDOW:, :].reshape(pb, WINDOW, N_KV, HEAD_DIM)
    unslide = lambda w_t: jnp.transpose(w_t.reshape(sb, N_KV, HEAD_DIM, WINDOW), (0, 3, 1, 2))
    prompt_out = (yp, jnp.stack(hp), jnp.stack(cp), win_kp, win_vp)
    sample_out = (ys.reshape(sb, 1, D_MODEL), jnp.stack(hs), jnp.transpose(jnp.stack(cs), (0, 2, 1, 3)),
                  unslide(wk), unslide(wv))
    return prompt_out, sample_out


def kernel(x_prompt, x_sample, cache_mem_k, cache_mem_v, state_lru_h, state_conv, cache_win_k, cache_win_v, mem_prompt, norm_g, w_in, w_out, mem_norm_g, w_mem_kv, conv_w, conv_b, lru_wa, lru_ba, lru_wx, lru_bx, lru_lambda, kv_norm_g, w_kv, sinks, final_g):
    W = _prepare_weights(norm_g, w_in, w_out, conv_w, conv_b, lru_wa, lru_ba, lru_wx, lru_bx, lru_lambda, kv_norm_g, w_kv, sinks, final_g)
    mk, mv, mkb, mvb = _mem_kv(mem_prompt, mem_norm_g, w_mem_kv)
    (y_p, h_p, conv_p, wk_p, wv_p), (y_s, h_s, conv_s, wk_s, wv_s) = _trunks(
        x_prompt, x_sample, W, mkb, mvb, cache_mem_k, cache_mem_v, state_lru_h, state_conv,
        cache_win_k, cache_win_v)
    batch = x_prompt.shape[0]
    mk_p = mk.reshape(DEPTH, batch, N_MEM, MEM_HEADS, MEM_HD)
    mv_p = mv.reshape(DEPTH, batch, N_MEM, MEM_HEADS, MEM_HD)
    return (y_p, y_s, mk_p, mv_p, h_p, conv_p, wk_p, wv_p, h_s, conv_s, wk_s, wv_s)
```

```python
import functools

import jax
import jax.numpy as jnp
from jax import lax
from jax.experimental import pallas as pl
from jax.experimental.pallas import tpu as pltpu

F32 = jnp.float32
BF16 = jnp.bfloat16

D_MODEL = 1024
DEPTH = 4
N_A = DEPTH // 2
N_HEADS = 16
HEAD_DIM = 64
N_KV = 2
GROUP = N_HEADS // N_KV
D_MAIN = N_HEADS * HEAD_DIM
N_BLOCKS = 16
BLOCK_W = D_MAIN // N_BLOCKS
CONV_W = 4
LRU_C = 8.0
WINDOW = 128
ROPE_THETA = 10000.0
N_MEM = 256
MEM_HEADS = 4
MEM_HD = 128
D_MEMW = MEM_HEADS * MEM_HD
D_IN = 2 * D_MAIN + 2 * D_MEMW
D_CAT = D_MAIN + D_MEMW
D_KV = N_KV * HEAD_DIM
EPS = 1e-6
NEG = -1e30
LOG2_E = 1.4426950408889634
F32_TINY = 1.1754943508222875e-38
PAST_LEN = 8192

SUBLANES = 8
LANES = 128
N_SLABS = D_MAIN // LANES
GATE_TILE = 256
N_GATE_TILES = D_MAIN // GATE_TILE
VMEM_LIMIT_BYTES = 56 * 1024 * 1024
ATTN_SKEW = 8
MEM_CHUNK = 256
SCAN_CHUNKS = 16
PROJ_CHUNK = 512
OUT_CHUNK = 256

_NT = (((1,), (1,)), ((), ()))


def _const_spec(shape):
    zeros = (0,) * len(shape)
    return pl.BlockSpec(shape, lambda *_: zeros, pipeline_mode=pl.Buffered(1))


def _layer_spec(shape, layer):
    zeros = (0,) * len(shape)
    return pl.BlockSpec((None,) + tuple(shape), lambda *_: (layer,) + zeros, pipeline_mode=pl.Buffered(1))


def _rms(x, g):
    return x * lax.rsqrt(jnp.mean(x * x, axis=-1, keepdims=True) + EPS) * g


def _silu_of_half(hx):
    return hx * jnp.tanh(hx) + hx


def _softplus(x):
    return jnp.maximum(x, 0.0) + jnp.log1p(jnp.exp(-jnp.abs(x)))


def _decay_exponent(lam):
    return (-0.5 * LRU_C * LOG2_E) * _softplus(-lam)


def _rope(x, cos, sin_signed):
    lane = lax.broadcasted_iota(jnp.int32, x.shape, 1)
    first_half = (lane % HEAD_DIM) < (HEAD_DIM // 2)
    swapped = jnp.where(first_half,
                        pltpu.roll(x, LANES - HEAD_DIM // 2, 1),
                        pltpu.roll(x, HEAD_DIM // 2, 1))
    return x * cos + swapped * sin_signed


def _rglru_coeffs(uc, half_pre, half_ba, half_bx, c):
    half = half_pre.shape[-1] // 2
    t_r = jnp.tanh(half_pre[:, :half] + half_ba)
    t_i = jnp.tanh(half_pre[:, half:] + half_bx)
    a = jnp.exp2(c * t_r + c)
    w = 1.0 - a * a
    root = w * lax.rsqrt(jnp.maximum(w, F32_TINY))
    return a, root * ((0.5 * t_i + 0.5) * uc)


def _mem_units(proj, mk_ref, mv_ref, z, row0, n_rows):
    chunk = min(n_rows, MEM_CHUNK)
    units = []
    for ch in range(n_rows // chunk):
        for hh in range(MEM_HEADS):
            rows = slice(row0 + chunk * ch, row0 + chunk * (ch + 1))
            cols = slice(MEM_HD * hh, MEM_HD * (hh + 1))
            q_cols = slice(2 * D_MAIN + MEM_HD * hh, 2 * D_MAIN + MEM_HD * (hh + 1))
            g_cols = slice(2 * D_MAIN + D_MEMW + MEM_HD * hh, 2 * D_MAIN + D_MEMW + MEM_HD * (hh + 1))
            z_cols = slice(D_MAIN + MEM_HD * hh, D_MAIN + MEM_HD * (hh + 1))

            def scores(rows=rows, cols=cols, q_cols=q_cols):
                q = proj[rows, q_cols].astype(BF16)
                s = lax.dot_general(mk_ref[0, :, cols], q, _NT,
                                    preferred_element_type=F32) * (MEM_HD ** -0.5 * LOG2_E)
                m = jnp.max(s, axis=0, keepdims=True)
                p = jnp.exp2(s - m)
                return p.astype(BF16), jnp.sum(p, axis=0, keepdims=True)

            def outputs(p, den, rows=rows, cols=cols, g_cols=g_cols, z_cols=z_cols):
                o_t = jnp.dot(mv_ref[0, cols, :], p, preferred_element_type=F32)
                z[rows, z_cols] = ((o_t / den).T * _silu_of_half(proj[rows, g_cols])).astype(BF16)

            units.append((scores, outputs))
    return units


def _skewed(units, skew):
    results = []
    first = lambda scores: lambda: results.append(scores())
    second = lambda outputs: lambda: outputs(*results.pop(0))
    thunks, pending = [], []
    for scores, outputs in units:
        thunks.append(first(scores))
        pending.append(second(outputs))
        if len(pending) > skew:
            thunks.append(pending.pop(0))
    return thunks + pending


def _interleave(*lists):
    lists = [l for l in lists if l]
    pos = [0] * len(lists)
    merged = []
    for _ in range(sum(len(l) for l in lists)):
        i = min((i for i in range(len(lists)) if pos[i] < len(lists[i])),
                key=lambda i: pos[i] / len(lists[i]))
        merged.append(lists[i][pos[i]])
        pos[i] += 1
    return merged


def _run(thunks):
    for thunk in thunks:
        thunk()


def _project_thunks(x_ref, ng_ref, win_ref, hn_s, proj_s, r0, n_rows):
    rows = slice(r0, r0 + n_rows)

    def norm():
        hn_s[rows, :] = _rms(x_ref[0, rows, :], ng_ref[...]).astype(BF16)

    def chunk(c0):
        def run():
            proj_s[rows, c0:c0 + PROJ_CHUNK] = jnp.dot(
                hn_s[rows, :], win_ref[:, c0:c0 + PROJ_CHUNK], preferred_element_type=F32)
        return run

    return [norm] + [chunk(c0) for c0 in range(0, D_IN, PROJ_CHUNK)]


def _out_project_thunks(x_ref, z, wout_ref, xo_ref, r0, n_rows):
    rows = slice(r0, r0 + n_rows)

    def chunk(c0):
        def run():
            cols = slice(c0, c0 + OUT_CHUNK)
            xo_ref[0, rows, cols] = x_ref[0, rows, cols] + jnp.dot(
                z[rows, :], wout_ref[:, cols], preferred_element_type=F32)
        return run

    return [chunk(c0) for c0 in range(0, D_MODEL, OUT_CHUNK)]


def _gate_tiles(wa, wx):
    per_tile = GATE_TILE // BLOCK_W
    eye = jnp.eye(per_tile, dtype=wa.dtype)

    def dense(w):
        w5 = w.reshape(N_A, N_GATE_TILES, per_tile, BLOCK_W, BLOCK_W)
        return jnp.einsum("ljicd,ik->ljickd", w5, eye).reshape(N_A, N_GATE_TILES, GATE_TILE, GATE_TILE)

    return jnp.concatenate([dense(wa), dense(wx)], axis=-1).astype(BF16)


def _prepare_weights(norm_g, w_in, w_out, conv_w, conv_b, lru_wa, lru_ba, lru_wx, lru_bx, lru_lambda,
                     kv_norm_g, w_kv, sinks, final_g):
    col = jnp.arange(D_IN)
    is_gate = ((col >= D_MAIN) & (col < 2 * D_MAIN)) | (col >= 2 * D_MAIN + D_MEMW)
    half = HEAD_DIM // 2
    inv = ROPE_THETA ** (-jnp.arange(half, dtype=F32) / half)
    sign = jnp.concatenate([-jnp.ones((half,), F32), jnp.ones((half,), F32)])
    return dict(
        norm_g=norm_g.reshape(DEPTH, 1, D_MODEL),
        w_in=(w_in * jnp.where(is_gate, 0.5, 1.0).astype(F32)).astype(BF16),
        w_out=w_out.astype(BF16),
        conv_w=conv_w, conv_b=conv_b.reshape(N_A, 1, D_MAIN),
        w_gate=_gate_tiles(0.5 * lru_wa, 0.5 * lru_wx),
        ba=0.5 * lru_ba.reshape(N_A, 1, D_MAIN), bx=0.5 * lru_bx.reshape(N_A, 1, D_MAIN),
        lam=lru_lambda.reshape(N_A, 1, D_MAIN),
        sinks=sinks, sinks_col=sinks.reshape(DEPTH - N_A, N_HEADS, 1),
        kv_norm_g=kv_norm_g.reshape(1, D_MODEL), w_kv=w_kv.astype(BF16),
        final_g=final_g.reshape(1, D_MODEL),
        inv128=jnp.tile(inv, LANES // half).reshape(1, LANES),
        sign128=jnp.tile(sign, LANES // HEAD_DIM).reshape(1, LANES))


def _rope_table_kernel(inv_ref, sign_ref, cos_ref, sin_ref):
    rows = cos_ref.shape[0]
    bases = rows // WINDOW
    inv = inv_ref[...]
    offset = lax.broadcasted_iota(jnp.int32, (WINDOW, LANES), 0).astype(F32) * inv
    cos_o, sin_o = jnp.cos(offset), jnp.sin(offset)
    base_pos = pl.program_id(0) * rows + WINDOW * lax.broadcasted_iota(jnp.int32, (bases, LANES), 0)
    base = base_pos.astype(F32) * inv
    cos_b, sin_b = jnp.cos(base), jnp.sin(base)
    for s in range(bases):
        blk = slice(WINDOW * s, WINDOW * (s + 1))
        cb, sb = cos_b[s:s + 1, :], sin_b[s:s + 1, :]
        cos_ref[blk, :] = cb * cos_o - sb * sin_o
        sin_ref[blk, :] = (sb * cos_o + cb * sin_o) * sign_ref[...]


def _rope_tables(seq, inv128, sign128, rows=1024):
    rows = min(rows, seq)
    return pl.pallas_call(
        _rope_table_kernel,
        grid=(seq // rows,),
        in_specs=[_const_spec((1, LANES)), _const_spec((1, LANES))],
        out_specs=[pl.BlockSpec((rows, LANES), lambda i: (i, 0)),
                   pl.BlockSpec((rows, LANES), lambda i: (i, 0))],
        out_shape=[jax.ShapeDtypeStruct((seq, LANES), F32)] * 2,
        name="rope_tables",
    )(inv128, sign128)


def _mem_kv_kernel(m_ref, g_ref, w_ref, k_ref, v_ref, kb_ref, vb_ref):
    batch = m_ref.shape[0]
    mn = _rms(m_ref[...].reshape(batch * N_MEM, D_MODEL), g_ref[0]).astype(BF16)
    kv = jnp.dot(mn, w_ref[0].astype(BF16), preferred_element_type=F32)
    for b in range(batch):
        rows = slice(N_MEM * b, N_MEM * (b + 1))
        k, v = kv[rows, :D_MEMW], kv[rows, D_MEMW:]
        for hh in range(MEM_HEADS):
            cols = slice(MEM_HD * hh, MEM_HD * (hh + 1))
            k_ref[0, b, pl.ds(hh, N_MEM, stride=MEM_HEADS), :] = k[:, cols]
            v_ref[0, b, pl.ds(hh, N_MEM, stride=MEM_HEADS), :] = v[:, cols]
        kb_ref[0, b] = k.astype(BF16)
        vb_ref[0, b] = v.T.astype(BF16)


def _mem_kv(mem_prompt, mem_norm_g, w_mem_kv):
    batch = mem_prompt.shape[0]
    layer_blk = lambda *shape: pl.BlockSpec((1,) + shape, lambda l: (l,) + (0,) * len(shape))
    return pl.pallas_call(
        _mem_kv_kernel,
        grid=(DEPTH,),
        in_specs=[_const_spec((batch, N_MEM, D_MODEL)), layer_blk(1, D_MODEL),
                  layer_blk(D_MODEL, 2 * D_MEMW)],
        out_specs=[layer_blk(batch, N_MEM * MEM_HEADS, MEM_HD), layer_blk(batch, N_MEM * MEM_HEADS, MEM_HD),
                   layer_blk(batch, N_MEM, D_MEMW), layer_blk(batch, D_MEMW, N_MEM)],
        out_shape=[jax.ShapeDtypeStruct((DEPTH, batch, N_MEM * MEM_HEADS, MEM_HD), F32)] * 2
                  + [jax.ShapeDtypeStruct((DEPTH, batch, N_MEM, D_MEMW), BF16),
                     jax.ShapeDtypeStruct((DEPTH, batch, D_MEMW, N_MEM), BF16)],
        compiler_params=pltpu.CompilerParams(vmem_limit_bytes=VMEM_LIMIT_BYTES),
        name="mem_kv",
    )(mem_prompt, mem_norm_g.reshape(DEPTH, 1, D_MODEL), w_mem_kv)


def _prompt_lru_kernel(tile, make_side, x_ref, ng_ref, win_ref, wout_ref, cw_ref, cb_ref, wg_ref, ba_ref,
                       bx_ref, lam_ref, mk_ref, mv_ref, xo_ref, hl_ref, ct_ref,
                       tails, a3, b3, h3, hc, z, hn_s, proj_s, uc_s):
    half = tile // 2
    t = pl.program_id(1)
    side_begin, side_thunks, side_end = make_side(dict(
        ng=ng_ref, win=win_ref, wout=wout_ref, cw=cw_ref, cb=cb_ref, wg=wg_ref, ba=ba_ref, bx=bx_ref, lam=lam_ref))

    @pl.when(t == 0)
    def _():
        tails[...] = jnp.zeros((SUBLANES, D_MAIN), F32)
        hc[...] = jnp.zeros((SUBLANES, LANES), F32)

    side_begin()

    decay_c = _decay_exponent(lam_ref[...])
    carry = [hc[...]]


    project = lambda r0: _project_thunks(x_ref, ng_ref, win_ref, hn_s, proj_s, r0, half)

    def gates(r0):
        rows = slice(r0, r0 + half)
        g0, gn = r0 // SUBLANES, half // SUBLANES

        def conv():
            u = proj_s[rows, 0:D_MAIN]
            first_row = lax.broadcasted_iota(jnp.int32, (half, D_MAIN), 0) == 0
            acc = u * cw_ref[0:1, :]
            for k in range(1, CONV_W):
                delayed = jnp.where(first_row, tails[k - 1:k, :], pltpu.roll(acc, 1, 0))
                tails[k - 1:k, :] = acc[half - 1:half, :]
                acc = delayed + u * cw_ref[k:k + 1, :]
            uc_s[rows, :] = acc + cb_ref[...]

        def gate(j):
            def run():
                sl = slice(GATE_TILE * j, GATE_TILE * (j + 1))
                uc = uc_s[rows, sl]
                pre = jnp.dot(uc.astype(BF16), wg_ref[j], preferred_element_type=F32)
                a, b = _rglru_coeffs(uc, pre, ba_ref[:, sl], bx_ref[:, sl], decay_c[:, sl])
                for jj in range(GATE_TILE // LANES):
                    slab = j * (GATE_TILE // LANES) + jj
                    srows = slice(SUBLANES * slab, SUBLANES * (slab + 1))
                    cols = slice(LANES * jj, LANES * (jj + 1))
                    a3[g0:g0 + gn, srows, :] = a[:, cols].reshape(gn, SUBLANES, LANES)
                    b3[g0:g0 + gn, srows, :] = b[:, cols].reshape(gn, SUBLANES, LANES)
            return run

        return [conv] + [gate(j) for j in range(N_GATE_TILES)]

    def scan(r0):
        def piece(first, count):
            def run():
                h = carry[0]
                for g in range(first, first + count):
                    for r in range(SUBLANES):
                        step = pl.ds(r, N_SLABS, stride=SUBLANES)
                        h = a3[g, step, :] * h + b3[g, step, :]
                        h3[g, step, :] = h
                carry[0] = h
            return run

        g0, gn = r0 // SUBLANES, half // SUBLANES
        count = max(gn // SCAN_CHUNKS, 1)
        return [piece(g, count) for g in range(g0, g0 + gn, count)]

    def gating(r0):
        def slab(j):
            def run():
                g0, gn = r0 // SUBLANES, half // SUBLANES
                y = h3[g0:g0 + gn, SUBLANES * j:SUBLANES * (j + 1), :].reshape(half, LANES)
                g_main = proj_s[r0:r0 + half, D_MAIN + LANES * j:D_MAIN + LANES * (j + 1)]
                z[r0:r0 + half, LANES * j:LANES * (j + 1)] = (y * _silu_of_half(g_main)).astype(BF16)
            return run
        return [slab(j) for j in range(N_SLABS)]

    out_project = lambda r0: _out_project_thunks(x_ref, z, wout_ref, xo_ref, r0, half)
    mem =lambda r0: _skewed(_mem_units(proj_s, mk_ref, mv_ref, z, r0, half), ATTN_SKEW)
    first, second = 0, half
    n_conv_in = 1 + D_MAIN // PROJ_CHUNK
    p_first, p_second = project(first), project(second)
    _run(p_first[:n_conv_in])
    _run(_interleave(p_first[n_conv_in:], gates(first)))
    _run(p_second[:n_conv_in])
    _run(_interleave(p_second[n_conv_in:], gates(second), mem(first), scan(first)))
    _run(_interleave(gating(first) + out_project(first), mem(second), scan(second), side_thunks))
    _run(gating(second))
    _run(out_project(second))

    h = carry[0]
    hc[...] = h

    @pl.when(t == pl.num_programs(1) - 1)
    def _():
        hl_ref[0] = h
        ct_ref[0] = proj_s[tile - SUBLANES:tile, 0:D_MAIN]

    side_end()


def _prompt_lru_part(x, W, layer, mk, mv_t, tile=256):
    batch, seq, _ = x.shape
    tok = lambda b, t: (b, t, 0)
    per_b = lambda b, t: (b, 0, 0)
    mem = lambda b, t: (layer, b, 0, 0)
    in_specs = [
        pl.BlockSpec((1, tile, D_MODEL), tok),
        _layer_spec((1, D_MODEL), layer), _layer_spec((D_MODEL, D_IN), layer),
        _layer_spec((D_CAT, D_MODEL), layer),
        _layer_spec((CONV_W, D_MAIN), layer), _layer_spec((1, D_MAIN), layer),
        _layer_spec((N_GATE_TILES, GATE_TILE, 2 * GATE_TILE), layer),
        _layer_spec((1, D_MAIN), layer), _layer_spec((1, D_MAIN), layer), _layer_spec((1, D_MAIN), layer),
        pl.BlockSpec((None, 1, N_MEM, D_MEMW), mem), pl.BlockSpec((None, 1, D_MEMW, N_MEM), mem),
    ]
    args = [x, W["norm_g"], W["w_in"], W["w_out"], W["conv_w"], W["conv_b"], W["w_gate"],
            W["ba"], W["bx"], W["lam"], mk, mv_t]
    out_specs = [pl.BlockSpec((1, tile, D_MODEL), tok),
                 pl.BlockSpec((1, SUBLANES, LANES), per_b),
                 pl.BlockSpec((1, SUBLANES, D_MAIN), per_b)]
    out_shape = [jax.ShapeDtypeStruct((batch, seq, D_MODEL), F32),
                 jax.ShapeDtypeStruct((batch, SUBLANES, LANES), F32),
                 jax.ShapeDtypeStruct((batch, SUBLANES, D_MAIN), F32)]
    groups = tile // SUBLANES
    return dict(
        body=functools.partial(_prompt_lru_kernel, tile), grid=(batch, seq // tile),
        in_specs=in_specs, args=args, out_specs=out_specs, out_shape=out_shape,
        scratch=[pltpu.VMEM((SUBLANES, D_MAIN), F32),
                 pltpu.VMEM((groups, SUBLANES * N_SLABS, LANES), F32),
                 pltpu.VMEM((groups, SUBLANES * N_SLABS, LANES), F32),
                 pltpu.VMEM((groups, SUBLANES * N_SLABS, LANES), F32),
                 pltpu.VMEM((SUBLANES, LANES), F32),
                 pltpu.VMEM((tile, D_CAT), BF16),
                 pltpu.VMEM((tile, D_MODEL), BF16),
                 pltpu.VMEM((tile, D_IN), F32),
                 pltpu.VMEM((tile, D_MAIN), F32)],
        name="lru")


def _sink_softmax(s, sink, axis=-1):
    m = jnp.maximum(jnp.max(s, axis=axis, keepdims=True), sink)
    p = jnp.exp2(s - m)
    den = jnp.sum(p, axis=axis, keepdims=True) + jnp.exp2(sink - m)
    return p, den


def _prompt_swa_kernel(make_kv, final, tile, sink_row, make_side, *refs):
    it = iter(refs)
    take = lambda n: [next(it) for _ in range(n)]
    x_ref, ng_ref, win_ref, wout_ref, mk_ref, mv_ref, sink_ref = take(7)
    if make_kv:
        kvg_ref, wkv_ref = take(2)
    else:
        kp_ref, kc_ref, vp_ref, vc_ref = take(4)
    cos_ref, sin_ref = take(2)
    fg_ref = take(1)[0] if final else None
    xo_ref, = take(1)
    if make_kv:
        ko_ref, vo_ref = take(2)
    if final:
        y_ref, = take(1)
    z, qz, kcat, vcat_t, hn_s, proj_s = take(6)
    half = tile // 2
    t = pl.program_id(1)
    side_begin, side_thunks, side_end = make_side(dict(ng=ng_ref, win=win_ref, wout=wout_ref, fg=fg_ref))
    side_begin()

    def rope_queries(r0):
        rows = slice(r0, r0 + half)

        def slab(j):
            def run():
                cos = cos_ref[rows, :]
                sin = sin_ref[rows, :]
                lane = lax.broadcasted_iota(jnp.int32, (half, LANES), 1)
                lo = lane < HEAD_DIM
                first_half = (lane % HEAD_DIM) < (HEAD_DIM // 2)
                c = j // (GROUP // 2)
                xq = proj_s[rows, LANES * j:LANES * (j + 1)] * (HEAD_DIM ** -0.5 * LOG2_E)
                r32 = pltpu.roll(xq, HEAD_DIM // 2, 1)
                r96 = pltpu.roll(xq, LANES - HEAD_DIM // 2, 1)
                r64 = pltpu.roll(xq, HEAD_DIM, 1)
                stay = xq * cos + jnp.where(first_half, r96, r32) * sin
                move = r64 * cos + jnp.where(first_half, r32, r96) * sin
                keep = lo if c == 0 else ~lo
                even, odd = (stay, move) if c == 0 else (move, stay)
                qz[rows, 2 * LANES * j:2 * LANES * j + LANES] = jnp.where(keep, even, 0.0).astype(BF16)
                qz[rows, 2 * LANES * j + LANES:2 * LANES * (j + 1)] = jnp.where(keep, odd, 0.0).astype(BF16)
            return run

        return [slab(j) for j in range(N_SLABS)]

    if make_kv:
        @pl.when(t == 0)
        def _():
            kcat[0:WINDOW, :] = jnp.zeros((WINDOW, D_KV), BF16)
            vcat_t[:, 0:WINDOW] = jnp.zeros((D_KV, WINDOW), BF16)

        @pl.when(t > 0)
        def _():
            kcat[0:WINDOW, :] = kcat[tile:tile + WINDOW, :]
            vcat_t[:, 0:WINDOW] = vcat_t[:, tile:tile + WINDOW]

        def shared_kv(r0):
            def run():
                rows = slice(r0, r0 + half)
                kvn = _rms(x_ref[0, rows, :], kvg_ref[...]).astype(BF16)
                kv = jnp.dot(kvn, wkv_ref[...], preferred_element_type=F32)
                k = _rope(kv[:, :D_KV], cos_ref[rows, :], sin_ref[rows, :])
                v = kv[:, D_KV:]
                ko_ref[0, rows, :] = k
                vo_ref[0, rows, :] = v
                kcat[WINDOW + r0:WINDOW + r0 + half, :] = k.astype(BF16)
                vcat_t[:, WINDOW + r0:WINDOW + r0 + half] = v.T.astype(BF16)
            return [run]
    else:
        kcat[0:WINDOW, :] = kp_ref[0].astype(BF16)
        kcat[WINDOW:, :] = kc_ref[0].astype(BF16)
        vcat_t[:, 0:WINDOW] = vp_ref[0].T.astype(BF16)
        vcat_t[:, WINDOW:] = vc_ref[0].T.astype(BF16)
        shared_kv = lambda r0: []

    key_r = lax.broadcasted_iota(jnp.int32, (WINDOW, 2 * WINDOW), 0)
    query = lax.broadcasted_iota(jnp.int32, (WINDOW, 2 * WINDOW), 1) % WINDOW
    from_prev = key_r > query

    def scores(n, j):
        rows = slice(WINDOW * n, WINDOW * (n + 1))
        keys = kcat[WINDOW * n:WINDOW * (n + 2), :]
        q2 = jnp.concatenate([qz[rows, 2 * LANES * j:2 * LANES * j + LANES],
                              qz[rows, 2 * LANES * j + LANES:2 * LANES * (j + 1)]], axis=0)
        s = lax.dot_general(keys, q2, _NT, preferred_element_type=F32)
        s_prev, s_own = s[:WINDOW, :], s[WINDOW:, :]
        if n == 0:
            s_prev = jnp.where(t > 0, s_prev, NEG)
        visible = jnp.where(from_prev, s_prev, s_own)
        p0, den0 = _sink_softmax(visible[:, :WINDOW], sink_ref[sink_row, 2 * j] * LOG2_E, axis=0)
        p1, den1 = _sink_softmax(visible[:, WINDOW:], sink_ref[sink_row, 2 * j + 1] * LOG2_E, axis=0)
        p = jnp.concatenate([p0, p1], axis=1).astype(BF16)
        zero = jnp.zeros_like(p)
        return jnp.concatenate([jnp.where(from_prev, p, zero), jnp.where(from_prev, zero, p)], axis=0), den0, den1

    def outputs(n, j, p, den0, den1):
        rows = slice(WINDOW * n, WINDOW * (n + 1))
        c = j // (GROUP // 2)
        vals_t = vcat_t[HEAD_DIM * c:HEAD_DIM * (c + 1), WINDOW * n:WINDOW * (n + 2)]
        o_t = jnp.dot(vals_t, p, preferred_element_type=F32)
        y_t = jnp.concatenate([o_t[:, :WINDOW] / den0, o_t[:, WINDOW:] / den1], axis=0)
        g_main = proj_s[rows, D_MAIN + LANES * j:D_MAIN + LANES * (j + 1)]
        z[rows, LANES * j:LANES * (j + 1)] = (y_t.T * _silu_of_half(g_main)).astype(BF16)

    def attend(r0):
        units = [(functools.partial(scores, n, j), functools.partial(outputs, n, j))
                 for n in range(r0 // WINDOW, (r0 + half) // WINDOW) for j in range(N_SLABS)]
        return _skewed(units + _mem_units(proj_s, mk_ref, mv_ref, z, r0, half), ATTN_SKEW)

    def out_project(r0):
        rows = slice(r0, r0 + half)

        def final_norm():
            y_ref[0, rows, :] = _rms(xo_ref[0, rows, :], fg_ref[...])

        return _out_project_thunks(x_ref, z, wout_ref, xo_ref, r0, half) + ([final_norm] if final else [])

    project = lambda r0: _project_thunks(x_ref, ng_ref, win_ref, hn_s, proj_s, r0, half)
    first, second = 0, half
    _run(project(first) + shared_kv(first))
    _run(_interleave(project(second) + shared_kv(second), rope_queries(first)))
    _run(_interleave(attend(first), rope_queries(second)))
    _run(_interleave(attend(second), out_project(first), side_thunks))
    _run(out_project(second))
    side_end()


def _prompt_swa_part(x, W, layer, mk, mv_t, kv, cos, sin, final_g=None, tile=256):
    batch, seq, _ = x.shape
    final = final_g is not None
    make_kv = kv is None
    blocks = tile // WINDOW
    tok = lambda b, t: (b, t, 0)
    mem = lambda b, t: (layer, b, 0, 0)
    prev_blk = lambda b, t: (b, jnp.maximum(t * blocks - 1, 0), 0)
    in_specs = [
        pl.BlockSpec((1, tile, D_MODEL), tok),
        _layer_spec((1, D_MODEL), layer), _layer_spec((D_MODEL, D_IN), layer),
        _layer_spec((D_CAT, D_MODEL), layer),
        pl.BlockSpec((None, 1, N_MEM, D_MEMW), mem), pl.BlockSpec((None, 1, D_MEMW, N_MEM), mem),
        pl.BlockSpec(memory_space=pltpu.SMEM)]
    args = [x, W["norm_g"], W["w_in"], W["w_out"], mk, mv_t, W["sinks"]]
    if make_kv:
        in_specs += [_const_spec((1, D_MODEL)), _const_spec((D_MODEL, 2 * D_KV))]
        args += [W["kv_norm_g"], W["w_kv"]]
    else:
        k, v = kv
        in_specs += [pl.BlockSpec((1, WINDOW, D_KV), prev_blk), pl.BlockSpec((1, tile, D_KV), tok),
                     pl.BlockSpec((1, WINDOW, D_KV), prev_blk), pl.BlockSpec((1, tile, D_KV), tok)]
        args += [k, k, v, v]
    in_specs += [pl.BlockSpec((tile, LANES), lambda b, t: (t, 0)), pl.BlockSpec((tile, LANES), lambda b, t: (t, 0))]
    args += [cos, sin]
    out_specs = [pl.BlockSpec((1, tile, D_MODEL), tok)]
    out_shape = [jax.ShapeDtypeStruct((batch, seq, D_MODEL), F32)]
    if make_kv:
        out_specs += [pl.BlockSpec((1, tile, D_KV), tok)] * 2
        out_shape += [jax.ShapeDtypeStruct((batch, seq, D_KV), F32)] * 2
    if final:
        in_specs.append(_const_spec((1, D_MODEL)))
        args.append(final_g)
        out_specs.append(pl.BlockSpec((1, tile, D_MODEL), tok))
        out_shape.append(jax.ShapeDtypeStruct((batch, seq, D_MODEL), F32))
    return dict(
        body=functools.partial(_prompt_swa_kernel, make_kv, final, tile, layer - N_A), grid=(batch, seq // tile),
        in_specs=in_specs, args=args, out_specs=out_specs, out_shape=out_shape,
        scratch=[pltpu.VMEM((tile, D_CAT), BF16),
                 pltpu.VMEM((tile, 2 * D_MAIN), BF16),
                 pltpu.VMEM((WINDOW + tile, D_KV), BF16),
                 pltpu.VMEM((D_KV, WINDOW + tile), BF16),
                 pltpu.VMEM((tile, D_MODEL), BF16),
                 pltpu.VMEM((tile, D_IN), F32)],
        name="swa" + ("_kv" if make_kv else "") + ("_final" if final else ""))


def _sample_mem_unit(b, bb, qm_s, mk_ref, mv_ref, om_s):
    def scores():
        row = lax.broadcasted_iota(jnp.int32, (SUBLANES, MEM_HD), 0)
        q = qm_s[pl.ds(b, 1), :]
        qh = jnp.zeros((SUBLANES, MEM_HD), F32)
        for hh in range(MEM_HEADS):
            q_head = jnp.broadcast_to(q[:, MEM_HD * hh:MEM_HD * (hh + 1)], (SUBLANES, MEM_HD))
            qh = jnp.where(row == hh, q_head, qh)
        s = lax.dot_general(qh.astype(BF16), mk_ref[bb].astype(BF16), _NT,
                            preferred_element_type=F32) * (MEM_HD ** -0.5 * LOG2_E)
        srow = lax.broadcasted_iota(jnp.int32, s.shape, 0)
        scol = lax.broadcasted_iota(jnp.int32, s.shape, 1)
        s = jnp.where((scol % MEM_HEADS) == srow, s, NEG)
        m = jnp.max(s, axis=-1, keepdims=True)
        p = jnp.exp2(s - m)
        return p.astype(BF16), jnp.sum(p, axis=-1, keepdims=True)

    def outputs(p, den):
        o = jnp.dot(p, mv_ref[bb].astype(BF16), preferred_element_type=F32) / den
        om_s[pl.ds(b, 1), :] = jnp.concatenate([o[hh:hh + 1, :] for hh in range(MEM_HEADS)], axis=1)

    return scores, outputs


def _sample_lru_side(seqs, step, n_steps, w, x_ref, h0_ref, c0_ref, mk_ref, mv_ref, xo_ref, ho_ref, co_ref,
                     z_s, qm_s, gm_s, om_s):
    def begin():
        @pl.when(step == 0)
        def _():
            hn = _rms(x_ref[...], w["ng"][...]).astype(BF16)
            proj = jnp.dot(hn, w["win"][...], preferred_element_type=F32)
            u = proj[:, :D_MAIN]
            taps = [c0_ref[k] for k in range(CONV_W - 1)] + [u]
            uc = w["cb"][...] + taps[0] * w["cw"][0:1, :]
            for k in range(1, CONV_W):
                uc = uc + taps[k] * w["cw"][k:k + 1, :]
            for k in range(CONV_W - 1):
                co_ref[k] = taps[k + 1]
            decay_c = _decay_exponent(w["lam"][...])
            ucb = uc.astype(BF16)
            for j in range(N_GATE_TILES):
                sl = slice(GATE_TILE * j, GATE_TILE * (j + 1))
                pre = jnp.dot(ucb[:, sl], w["wg"][j], preferred_element_type=F32)
                a, b = _rglru_coeffs(uc[:, sl], pre, w["ba"][:, sl], w["bx"][:, sl], decay_c[:, sl])
                h = a * h0_ref[:, sl] + b
                ho_ref[:, sl] = h
                z_s[:, sl] = (h * _silu_of_half(
                    proj[:, D_MAIN + GATE_TILE * j:D_MAIN + GATE_TILE * (j + 1)])).astype(BF16)
            qm_s[...] = proj[:, 2 * D_MAIN:2 * D_MAIN + D_MEMW]
            gm_s[...] = _silu_of_half(proj[:, 2 * D_MAIN + D_MEMW:])

    thunks = _skewed([_sample_mem_unit(step * seqs + bb, bb, qm_s, mk_ref, mv_ref, om_s) for bb in range(seqs)],
                     ATTN_SKEW)

    def end():
        @pl.when(step == n_steps - 1)
        def _():
            z_s[:, D_MAIN:] = (om_s[...] * gm_s[...]).astype(BF16)
            xo_ref[...] = x_ref[...] + jnp.dot(z_s[...], w["wout"][...], preferred_element_type=F32)

    return begin, thunks, end


def _cache_spec(layer, seqs, steps_per_row):
    return pl.BlockSpec((None, seqs, N_MEM * MEM_HEADS, MEM_HD),
                        lambda b, t: (layer, b * steps_per_row + t, 0, 0))


def _whole_spec(shape):
    zeros = (0,) * len(shape)
    return pl.BlockSpec(shape, lambda *_: zeros)


def _sample_lru_part(x, layer, state_h, state_conv_t, cache_k, cache_v, grid):
    batch = x.shape[0]
    seqs = batch // (grid[0] * grid[1])
    cache_spec = _cache_spec(layer, seqs, grid[1])
    return dict(
        body=functools.partial(_sample_lru_side, seqs),
        in_specs=[_const_spec((batch, D_MODEL)),
                  _layer_spec((batch, D_MAIN), layer), _layer_spec((CONV_W - 1, batch, D_MAIN), layer),
                  cache_spec, cache_spec],
        args=[x, state_h, state_conv_t, cache_k, cache_v],
        out_specs=[_whole_spec((batch, D_MODEL)), _whole_spec((batch, D_MAIN)),
                   _whole_spec((CONV_W - 1, batch, D_MAIN))],
        out_shape=[jax.ShapeDtypeStruct((batch, D_MODEL), F32),
                   jax.ShapeDtypeStruct((batch, D_MAIN), F32),
                   jax.ShapeDtypeStruct((CONV_W - 1, batch, D_MAIN), F32)],
        scratch=[pltpu.VMEM((batch, D_CAT), BF16), pltpu.VMEM((batch, D_MEMW), F32),
                 pltpu.VMEM((batch, D_MEMW), F32), pltpu.VMEM((batch, D_MEMW), F32)])


def _sample_swa_side(first, final, seqs, step, n_steps, w, *refs):
    (x_ref, sink_ref, mk_ref, mv_ref, wk_ref, wv_ref) = refs[:6]
    pos = 6
    if first:
        kvg_ref, wkv_ref = refs[pos:pos + 2]
        pos += 2
    inv_ref, sign_ref = refs[pos:pos + 2]
    pos += 2
    xo_ref = refs[pos]
    pos += 1
    if first:
        wko_ref, wvo_ref = refs[pos:pos + 2]
        pos += 2
    if final:
        y_ref = refs[pos]
        pos += 1
    z_s, q_s, gmain_s, y_s, qm_s, gm_s, om_s, kn_s, vn_s = refs[pos:]

    def begin():
        @pl.when(step == 0)
        def _():
            x = x_ref[...]
            hn = _rms(x, w["ng"][...]).astype(BF16)
            proj = jnp.dot(hn, w["win"][...], preferred_element_type=F32)
            ang = float(PAST_LEN) * inv_ref[...]
            cos = jnp.cos(ang)
            sin = jnp.sin(ang) * sign_ref[...]
            for j in range(N_SLABS):
                cols = slice(LANES * j, LANES * (j + 1))
                q_s[:, cols] = _rope(proj[:, cols], cos, sin)
            gmain_s[...] = _silu_of_half(proj[:, D_MAIN:2 * D_MAIN])
            qm_s[...] = proj[:, 2 * D_MAIN:2 * D_MAIN + D_MEMW]
            gm_s[...] = _silu_of_half(proj[:, 2 * D_MAIN + D_MEMW:])
            if first:
                kvn = _rms(x, kvg_ref[...]).astype(BF16)
                kv = jnp.dot(kvn, wkv_ref[...], preferred_element_type=F32)
                kn_s[...] = _rope(kv[:, :D_KV], cos, sin).T
                vn_s[...] = kv[:, D_KV:].T

    row = lax.broadcasted_iota(jnp.int32, (N_HEADS, LANES), 0)
    lane = lax.broadcasted_iota(jnp.int32, (N_HEADS, LANES), 1)
    kv_half = (lane // HEAD_DIM) == (row // GROUP)
    in_place = (row % 2) == (row // GROUP)
    lane1 = lax.broadcasted_iota(jnp.int32, (1, LANES), 1)
    seq_lane = lax.broadcasted_iota(jnp.int32, kn_s.shape, 1)
    slot = lax.broadcasted_iota(jnp.int32, (D_KV, WINDOW), 1)

    def slide(win_t, new_t_s, b):
        if new_t_s.shape[1] == WINDOW:
            col = pltpu.roll(new_t_s[...], WINDOW - 1 - b, 1)
        else:
            col = jnp.sum(jnp.where(seq_lane == b, new_t_s[...], 0.0), axis=1, keepdims=True)
        return jnp.where(slot == WINDOW - 1, col, pltpu.roll(win_t, WINDOW - 1, 1))

    def window_unit(b, bb):
        def scores():
            if first:
                kw_t = slide(wk_ref[bb], kn_s, b)
                wko_ref[bb] = kw_t
            else:
                kw_t = wk_ref[bb]
            q = jnp.broadcast_to(q_s[pl.ds(b, 1), :], (N_HEADS, D_MAIN))
            e = jnp.zeros((N_HEADS, LANES), F32)
            for j in range(N_SLABS):
                e = jnp.where((row // 2) == j, q[:, LANES * j:LANES * (j + 1)], e)
            qh = jnp.where(kv_half, jnp.where(in_place, e, pltpu.roll(e, HEAD_DIM, 1)), 0.0).astype(BF16)
            s = jnp.dot(qh, kw_t.astype(BF16), preferred_element_type=F32) * (HEAD_DIM ** -0.5 * LOG2_E)
            p, den = _sink_softmax(s, sink_ref[...] * LOG2_E)
            return p.astype(BF16), den

        def outputs(p, den):
            if first:
                vw_t = slide(wv_ref[bb], vn_s, b)
                wvo_ref[bb] = vw_t
            else:
                vw_t = wv_ref[bb]
            o = lax.dot_general(p, vw_t.astype(BF16), _NT, preferred_element_type=F32) / den
            f = jnp.where(in_place, o, pltpu.roll(o, HEAD_DIM, 1))
            y_s[pl.ds(b, 1), :] = jnp.concatenate(
                [jnp.where(lane1 < HEAD_DIM, f[2 * j:2 * j + 1, :], f[2 * j + 1:2 * j + 2, :])
                 for j in range(N_SLABS)], axis=1)

        return scores, outputs

    units = []
    for bb in range(seqs):
        units.append(window_unit(step * seqs + bb, bb))
        units.append(_sample_mem_unit(step * seqs + bb, bb, qm_s, mk_ref, mv_ref, om_s))

    def end():
        @pl.when(step == n_steps - 1)
        def _():
            z_s[:, :D_MAIN] = (y_s[...] * gmain_s[...]).astype(BF16)
            z_s[:, D_MAIN:] = (om_s[...] * gm_s[...]).astype(BF16)
            xn = x_ref[...] + jnp.dot(z_s[...], w["wout"][...], preferred_element_type=F32)
            xo_ref[...] = xn
            if final:
                y_ref[...] = _rms(xn, w["fg"][...])

    return begin, _skewed(units, ATTN_SKEW), end


def _sample_swa_part(x, W, layer, cache_k, cache_v, win_k_t, win_v_t, first, final, grid):
    batch = x.shape[0]
    seqs = batch // (grid[0] * grid[1])
    cache_spec = _cache_spec(layer, seqs, grid[1])
    win_spec = pl.BlockSpec((seqs, D_KV, WINDOW), lambda b, t: (b * grid[1] + t, 0, 0))
    in_specs = [_const_spec((batch, D_MODEL)), _layer_spec((N_HEADS, 1), layer - N_A),
                cache_spec, cache_spec, win_spec, win_spec]
    args = [x, W["sinks_col"], cache_k, cache_v, win_k_t, win_v_t]
    if first:
        in_specs += [_const_spec((1, D_MODEL)), _const_spec((D_MODEL, 2 * D_KV))]
        args += [W["kv_norm_g"], W["w_kv"]]
    in_specs += [_const_spec((1, LANES)), _const_spec((1, LANES))]
    args += [W["inv128"], W["sign128"]]
    out_specs = [_whole_spec((batch, D_MODEL))]
    out_shape = [jax.ShapeDtypeStruct((batch, D_MODEL), F32)]
    if first:
        out_specs += [win_spec, win_spec]
        out_shape += [jax.ShapeDtypeStruct((batch, D_KV, WINDOW), F32)] * 2
    if final:
        out_specs.append(_whole_spec((batch, D_MODEL)))
        out_shape.append(jax.ShapeDtypeStruct((batch, D_MODEL), F32))
    return dict(
        body=functools.partial(_sample_swa_side, first, final, seqs),
        in_specs=in_specs, args=args, out_specs=out_specs, out_shape=out_shape,
        scratch=[pltpu.VMEM((batch, D_CAT), BF16), pltpu.VMEM((batch, D_MAIN), F32),
                 pltpu.VMEM((batch, D_MAIN), F32), pltpu.VMEM((batch, D_MAIN), F32),
                 pltpu.VMEM((batch, D_MEMW), F32), pltpu.VMEM((batch, D_MEMW), F32),
                 pltpu.VMEM((batch, D_MEMW), F32),
                 pltpu.VMEM((D_KV, batch), F32), pltpu.VMEM((D_KV, batch), F32)])


def _layer_kernel(prompt_body, sample_body, n_prompt, n_sample, *refs):
    it = iter(refs)
    take = lambda n: [next(it) for _ in range(n)]
    p_in, s_in = take(n_prompt[0]), take(n_sample[0])
    p_out, s_out = take(n_prompt[1]), take(n_sample[1])
    p_scr, s_scr = take(n_prompt[2]), take(n_sample[2])
    step = pl.program_id(0) * pl.num_programs(1) + pl.program_id(1)
    n_steps = pl.num_programs(0) * pl.num_programs(1)
    make_side = lambda weights: sample_body(step, n_steps, weights, *s_in, *s_out, *s_scr)
    prompt_body(make_side, *p_in, *p_out, *p_scr)


def _layer_call(prompt, sample):
    counts = lambda part: (len(part["in_specs"]), len(part["out_specs"]), len(part["scratch"]))
    outs = pl.pallas_call(
        functools.partial(_layer_kernel, prompt["body"], sample["body"], counts(prompt), counts(sample)),
        grid=prompt["grid"],
        in_specs=prompt["in_specs"] + sample["in_specs"],
        out_specs=prompt["out_specs"] + sample["out_specs"],
        out_shape=prompt["out_shape"] + sample["out_shape"],
        scratch_shapes=prompt["scratch"] + sample["scratch"],
        compiler_params=pltpu.CompilerParams(
            dimension_semantics=("arbitrary", "arbitrary"), vmem_limit_bytes=VMEM_LIMIT_BYTES),
        name="layer_" + prompt["name"],
    )(*prompt["args"], *sample["args"])
    n = len(prompt["out_specs"])
    return outs[:n], outs[n:]


def _trunks(xp, xs, W, mkb, mvb_t, cache_mem_k, cache_mem_v, state_h, state_conv, win_k, win_v, tile=512):
    pb, seq, _ = xp.shape
    sb = xs.shape[0]
    grid = (pb, seq // tile)
    cos, sin = _rope_tables(seq, W["inv128"], W["sign128"])
    xs = xs.reshape(sb, D_MODEL)
    ck = cache_mem_k.reshape(DEPTH, sb, N_MEM * MEM_HEADS, MEM_HD)
    cv = cache_mem_v.reshape(DEPTH, sb, N_MEM * MEM_HEADS, MEM_HD)
    conv_t = jnp.transpose(state_conv, (0, 2, 1, 3))
    wk = jnp.transpose(win_k, (0, 2, 3, 1)).reshape(sb, D_KV, WINDOW)
    wv = jnp.transpose(win_v, (0, 2, 3, 1)).reshape(sb, D_KV, WINDOW)
    hp, cp, hs, cs = [], [], [], []
    for l in range(N_A):
        p_out, s_out = _layer_call(
            _prompt_lru_part(xp, W, l, mkb, mvb_t, tile=tile),
            _sample_lru_part(xs, l, state_h, conv_t, ck, cv, grid))
        xp, h_last, conv_tail = p_out
        hp.append(h_last.reshape(pb, D_MAIN))
        cp.append(conv_tail[:, SUBLANES - (CONV_W - 1):, :])
        xs, h, c = s_out
        hs.append(h)
        cs.append(c)
    yp = ys = kv = None
    for l in range(N_A, DEPTH):
        first, last = l == N_A, l == DEPTH - 1
        p_out, s_out = _layer_call(
            _prompt_swa_part(xp, W, l, mkb, mvb_t, kv, cos, sin, final_g=W["final_g"] if last else None, tile=tile),
            _sample_swa_part(xs, W, l, ck, cv, wk, wv, first, last, grid))
        xp, xs = p_out[0], s_out[0]
        if first:
            kv = (p_out[1], p_out[2])
            wk, wv = s_out[1], s_out[2]
        if last:
            yp, ys = p_out[-1], s_out[-1]
    k, v = kv
    win_kp = k[:, seq - WINDOW:, :].reshape(pb, WINDOW, N_KV, HEAD_DIM)
    win_vp = v[:, seq - WINDOW:, :].reshape(pb, WINDOW, N_KV, HEAD_DIM)
    unslide = lambda w_t: jnp.transpose(w_t.reshape(sb, N_KV, HEAD_DIM, WINDOW), (0, 3, 1, 2))
    prompt_out = (yp, jnp.stack(hp), jnp.stack(cp), win_kp, win_vp)
    sample_out = (ys.reshape(sb, 1, D_MODEL), jnp.stack(hs), jnp.transpose(jnp.stack(cs), (0, 2, 1, 3)),
                  unslide(wk), unslide(wv))
    return prompt_out, sample_out


def kernel(x_prompt, x_sample, cache_mem_k, cache_mem_v, state_lru_h, state_conv, cache_win_k, cache_win_v, mem_prompt, norm_g, w_in, w_out, mem_norm_g, w_mem_kv, conv_w, conv_b, lru_wa, lru_ba, lru_wx, lru_bx, lru_lambda, kv_norm_g, w_kv, sinks, final_g):
    W = _prepare_weights(norm_g, w_in, w_out, conv_w, conv_b, lru_wa, lru_ba, lru_wx, lru_bx, lru_lambda, kv_norm_g, w_kv, sinks, final_g)
    mk, mv, mkb, mvb = _mem_kv(mem_prompt, mem_norm_g, w_mem_kv)
    (y_p, h_p, conv_p, wk_p, wv_p), (y_s, h_s, conv_s, wk_s, wv_s) = _trunks(
        x_prompt, x_sample, W, mkb, mvb, cache_mem_k, cache_mem_v, state_lru_h, state_conv,
        cache_win_k, cache_win_v)
    batch = x_prompt.shape[0]
    mk_p = mk.reshape(DEPTH, batch, N_MEM, MEM_HEADS, MEM_HD)
    mv_p = mv.reshape(DEPTH, batch, N_MEM, MEM_HEADS, MEM_HD)
    return (y_p, y_s, mk_p, mv_p, h_p, conv_p, wk_p, wv_p, h_s, conv_s, wk_s, wv_s)
```

```python
import functools

import jax
import jax.numpy as jnp
from jax import lax
from jax.experimental import pallas as pl
from jax.experimental.pallas import tpu as pltpu

F32 = jnp.float32
BF16 = jnp.bfloat16

D_MODEL = 1024
DEPTH = 4
N_A = DEPTH // 2
N_HEADS = 16
HEAD_DIM = 64
N_KV = 2
GROUP = N_HEADS // N_KV
D_MAIN = N_HEADS * HEAD_DIM
N_BLOCKS = 16
BLOCK_W = D_MAIN // N_BLOCKS
CONV_W = 4
LRU_C = 8.0
WINDOW = 128
ROPE_THETA = 10000.0
N_MEM = 256
MEM_HEADS = 4
MEM_HD = 128
D_MEMW = MEM_HEADS * MEM_HD
D_IN = 2 * D_MAIN + 2 * D_MEMW
D_CAT = D_MAIN + D_MEMW
D_KV = N_KV * HEAD_DIM
EPS = 1e-6
NEG = -1e30
LOG2_E = 1.4426950408889634
F32_TINY = 1.1754943508222875e-38
PAST_LEN = 8192

SUBLANES = 8
LANES = 128
N_SLABS = D_MAIN // LANES
GATE_TILE = 256
N_GATE_TILES = D_MAIN // GATE_TILE
VMEM_LIMIT_BYTES = 56 * 1024 * 1024
ATTN_SKEW = 8
MEM_CHUNK = 256
SCAN_CHUNKS = 16
PROJ_CHUNK = 512
OUT_CHUNK = 256

_NT = (((1,), (1,)), ((), ()))


def _const_spec(shape):
    zeros = (0,) * len(shape)
    return pl.BlockSpec(shape, lambda *_: zeros, pipeline_mode=pl.Buffered(1))


def _layer_spec(shape, layer):
    zeros = (0,) * len(shape)
    return pl.BlockSpec((None,) + tuple(shape), lambda *_: (layer,) + zeros, pipeline_mode=pl.Buffered(1))


def _rms(x, g):
    return x * lax.rsqrt(jnp.mean(x * x, axis=-1, keepdims=True) + EPS) * g


def _silu_of_half(hx):
    return hx * jnp.tanh(hx) + hx


def _softplus(x):
    return jnp.maximum(x, 0.0) + jnp.log1p(jnp.exp(-jnp.abs(x)))


def _decay_exponent(lam):
    return (-0.5 * LRU_C * LOG2_E) * _softplus(-lam)


def _rope(x, cos, sin_signed):
    lane = lax.broadcasted_iota(jnp.int32, x.shape, 1)
    first_half = (lane % HEAD_DIM) < (HEAD_DIM // 2)
    swapped = jnp.where(first_half,
                        pltpu.roll(x, LANES - HEAD_DIM // 2, 1),
                        pltpu.roll(x, HEAD_DIM // 2, 1))
    return x * cos + swapped * sin_signed


def _rglru_coeffs(uc, half_pre, half_ba, half_bx, c):
    half = half_pre.shape[-1] // 2
    t_r = jnp.tanh(half_pre[:, :half] + half_ba)
    t_i = jnp.tanh(half_pre[:, half:] + half_bx)
    a = jnp.exp2(c * t_r + c)
    w = 1.0 - a * a
    root = w * lax.rsqrt(jnp.maximum(w, F32_TINY))
    return a, root * ((0.5 * t_i + 0.5) * uc)


def _mem_units(proj, mk_ref, mv_ref, z, row0, n_rows):
    chunk = min(n_rows, MEM_CHUNK)
    units = []
    for ch in range(n_rows // chunk):
        for hh in range(MEM_HEADS):
            rows = slice(row0 + chunk * ch, row0 + chunk * (ch + 1))
            cols = slice(MEM_HD * hh, MEM_HD * (hh + 1))
            q_cols = slice(2 * D_MAIN + MEM_HD * hh, 2 * D_MAIN + MEM_HD * (hh + 1))
            g_cols = slice(2 * D_MAIN + D_MEMW + MEM_HD * hh, 2 * D_MAIN + D_MEMW + MEM_HD * (hh + 1))
            z_cols = slice(D_MAIN + MEM_HD * hh, D_MAIN + MEM_HD * (hh + 1))

            def scores(rows=rows, cols=cols, q_cols=q_cols):
                q = proj[rows, q_cols].astype(BF16)
                s = lax.dot_general(mk_ref[0, :, cols], q, _NT,
                                    preferred_element_type=F32) * (MEM_HD ** -0.5 * LOG2_E)
                m = jnp.max(s, axis=0, keepdims=True)
                p = jnp.exp2(s - m)
                return p.astype(BF16), jnp.sum(p, axis=0, keepdims=True)

            def outputs(p, den, rows=rows, cols=cols, g_cols=g_cols, z_cols=z_cols):
                o_t = jnp.dot(mv_ref[0, cols, :], p, preferred_element_type=F32)
                z[rows, z_cols] = ((o_t / den).T * _silu_of_half(proj[rows, g_cols])).astype(BF16)

            units.append((scores, outputs))
    return units


def _skewed(units, skew):
    results = []
    first = lambda scores: lambda: results.append(scores())
    second = lambda outputs: lambda: outputs(*results.pop(0))
    thunks, pending = [], []
    for scores, outputs in units:
        thunks.append(first(scores))
        pending.append(second(outputs))
        if len(pending) > skew:
            thunks.append(pending.pop(0))
    return thunks + pending


def _interleave(*lists):
    lists = [l for l in lists if l]
    pos = [0] * len(lists)
    merged = []
    for _ in range(sum(len(l) for l in lists)):
        i = min((i for i in range(len(lists)) if pos[i] < len(lists[i])),
                key=lambda i: pos[i] / len(lists[i]))
        merged.append(lists[i][pos[i]])
        pos[i] += 1
    return merged


def _run(thunks):
    for thunk in thunks:
        thunk()


def _project_thunks(x_ref, ng_ref, win_ref, hn_s, proj_s, r0, n_rows):
    rows = slice(r0, r0 + n_rows)

    def norm():
        hn_s[rows, :] = _rms(x_ref[0, rows, :], ng_ref[...]).astype(BF16)

    def chunk(c0):
        def run():
            proj_s[rows, c0:c0 + PROJ_CHUNK] = jnp.dot(
                hn_s[rows, :], win_ref[:, c0:c0 + PROJ_CHUNK], preferred_element_type=F32)
        return run

    return [norm] + [chunk(c0) for c0 in range(0, D_IN, PROJ_CHUNK)]


def _out_project_thunks(x_ref, z, wout_ref, xo_ref, r0, n_rows):
    rows = slice(r0, r0 + n_rows)

    def chunk(c0):
        def run():
            cols = slice(c0, c0 + OUT_CHUNK)
            xo_ref[0, rows, cols] = x_ref[0, rows, cols] + jnp.dot(
                z[rows, :], wout_ref[:, cols], preferred_element_type=F32)
        return run

    return [chunk(c0) for c0 in range(0, D_MODEL, OUT_CHUNK)]


def _gate_tiles(wa, wx):
    per_tile = GATE_TILE // BLOCK_W
    eye = jnp.eye(per_tile, dtype=wa.dtype)

    def dense(w):
        w5 = w.reshape(N_A, N_GATE_TILES, per_tile, BLOCK_W, BLOCK_W)
        return jnp.einsum("ljicd,ik->ljickd", w5, eye).reshape(N_A, N_GATE_TILES, GATE_TILE, GATE_TILE)

    return jnp.concatenate([dense(wa), dense(wx)], axis=-1).astype(BF16)


def _prepare_weights(norm_g, w_in, w_out, conv_w, conv_b, lru_wa, lru_ba, lru_wx, lru_bx, lru_lambda,
                     kv_norm_g, w_kv, sinks, final_g):
    col = jnp.arange(D_IN)
    is_gate = ((col >= D_MAIN) & (col < 2 * D_MAIN)) | (col >= 2 * D_MAIN + D_MEMW)
    half = HEAD_DIM // 2
    inv = ROPE_THETA ** (-jnp.arange(half, dtype=F32) / half)
    sign = jnp.concatenate([-jnp.ones((half,), F32), jnp.ones((half,), F32)])
    return dict(
        norm_g=norm_g.reshape(DEPTH, 1, D_MODEL),
        w_in0=(w_in[0] * jnp.where(is_gate, 0.5, 1.0).astype(F32)).astype(BF16),
        w_out0=w_out[0].astype(BF16),
        w_in=w_in, w_out=w_out, in_scale=jnp.where(is_gate, 0.5, 1.0).astype(F32).reshape(1, D_IN),
        conv_w=conv_w, conv_b=conv_b.reshape(N_A, 1, D_MAIN),
        w_gate=_gate_tiles(0.5 * lru_wa, 0.5 * lru_wx),
        ba=0.5 * lru_ba.reshape(N_A, 1, D_MAIN), bx=0.5 * lru_bx.reshape(N_A, 1, D_MAIN),
        lam=lru_lambda.reshape(N_A, 1, D_MAIN),
        sinks=sinks, sinks_col=sinks.reshape(DEPTH - N_A, N_HEADS, 1),
        kv_norm_g=kv_norm_g.reshape(1, D_MODEL), w_kv=w_kv.astype(BF16),
        final_g=final_g.reshape(1, D_MODEL),
        inv128=jnp.tile(inv, LANES // half).reshape(1, LANES),
        sign128=jnp.tile(sign, LANES // HEAD_DIM).reshape(1, LANES))


def _rope_table_kernel(inv_ref, sign_ref, cos_ref, sin_ref):
    rows = cos_ref.shape[0]
    bases = rows // WINDOW
    inv = inv_ref[...]
    offset = lax.broadcasted_iota(jnp.int32, (WINDOW, LANES), 0).astype(F32) * inv
    cos_o, sin_o = jnp.cos(offset), jnp.sin(offset)
    base_pos = pl.program_id(0) * rows + WINDOW * lax.broadcasted_iota(jnp.int32, (bases, LANES), 0)
    base = base_pos.astype(F32) * inv
    cos_b, sin_b = jnp.cos(base), jnp.sin(base)
    for s in range(bases):
        blk = slice(WINDOW * s, WINDOW * (s + 1))
        cb, sb = cos_b[s:s + 1, :], sin_b[s:s + 1, :]
        cos_ref[blk, :] = cb * cos_o - sb * sin_o
        sin_ref[blk, :] = (sb * cos_o + cb * sin_o) * sign_ref[...]


def _rope_tables(seq, inv128, sign128, rows=1024):
    rows = min(rows, seq)
    return pl.pallas_call(
        _rope_table_kernel,
        grid=(seq // rows,),
        in_specs=[_const_spec((1, LANES)), _const_spec((1, LANES))],
        out_specs=[pl.BlockSpec((rows, LANES), lambda i: (i, 0)),
                   pl.BlockSpec((rows, LANES), lambda i: (i, 0))],
        out_shape=[jax.ShapeDtypeStruct((seq, LANES), F32)] * 2,
        name="rope_tables",
    )(inv128, sign128)


def _mem_kv_kernel(m_ref, g_ref, w_ref, k_ref, v_ref, kb_ref, vb_ref):
    batch = m_ref.shape[0]
    mn = _rms(m_ref[...].reshape(batch * N_MEM, D_MODEL), g_ref[0]).astype(BF16)
    kv = jnp.dot(mn, w_ref[0].astype(BF16), preferred_element_type=F32)
    for b in range(batch):
        rows = slice(N_MEM * b, N_MEM * (b + 1))
        k, v = kv[rows, :D_MEMW], kv[rows, D_MEMW:]
        for hh in range(MEM_HEADS):
            cols = slice(MEM_HD * hh, MEM_HD * (hh + 1))
            k_ref[0, b, pl.ds(hh, N_MEM, stride=MEM_HEADS), :] = k[:, cols]
            v_ref[0, b, pl.ds(hh, N_MEM, stride=MEM_HEADS), :] = v[:, cols]
        kb_ref[0, b] = k.astype(BF16)
        vb_ref[0, b] = v.T.astype(BF16)


def _mem_kv(mem_prompt, mem_norm_g, w_mem_kv):
    batch = mem_prompt.shape[0]
    layer_blk = lambda *shape: pl.BlockSpec((1,) + shape, lambda l: (l,) + (0,) * len(shape))
    return pl.pallas_call(
        _mem_kv_kernel,
        grid=(DEPTH,),
        in_specs=[_const_spec((batch, N_MEM, D_MODEL)), layer_blk(1, D_MODEL),
                  layer_blk(D_MODEL, 2 * D_MEMW)],
        out_specs=[layer_blk(batch, N_MEM * MEM_HEADS, MEM_HD), layer_blk(batch, N_MEM * MEM_HEADS, MEM_HD),
                   layer_blk(batch, N_MEM, D_MEMW), layer_blk(batch, D_MEMW, N_MEM)],
        out_shape=[jax.ShapeDtypeStruct((DEPTH, batch, N_MEM * MEM_HEADS, MEM_HD), F32)] * 2
                  + [jax.ShapeDtypeStruct((DEPTH, batch, N_MEM, D_MEMW), BF16),
                     jax.ShapeDtypeStruct((DEPTH, batch, D_MEMW, N_MEM), BF16)],
        compiler_params=pltpu.CompilerParams(vmem_limit_bytes=VMEM_LIMIT_BYTES),
        name="mem_kv",
    )(mem_prompt, mem_norm_g.reshape(DEPTH, 1, D_MODEL), w_mem_kv)


def _prompt_lru_kernel(tile, make_side, x_ref, ng_ref, win_ref, wout_ref, cw_ref, cb_ref, wg_ref, ba_ref,
                       bx_ref, lam_ref, mk_ref, mv_ref, xo_ref, hl_ref, ct_ref,
                       tails, a3, b3, h3, hc, z, hn_s, proj_s, uc_s):
    half = tile // 2
    t = pl.program_id(1)
    side_begin, side_thunks, side_end = make_side(dict(
        ng=ng_ref, win=win_ref, wout=wout_ref, cw=cw_ref, cb=cb_ref, wg=wg_ref, ba=ba_ref, bx=bx_ref, lam=lam_ref))

    @pl.when(t == 0)
    def _():
        tails[...] = jnp.zeros((SUBLANES, D_MAIN), F32)
        hc[...] = jnp.zeros((SUBLANES, LANES), F32)

    side_begin()

    decay_c = _decay_exponent(lam_ref[...])
    carry = [hc[...]]


    project = lambda r0: _project_thunks(x_ref, ng_ref, win_ref, hn_s, proj_s, r0, half)

    def gates(r0):
        rows = slice(r0, r0 + half)
        g0, gn = r0 // SUBLANES, half // SUBLANES

        def conv():
            u = proj_s[rows, 0:D_MAIN]
            first_row = lax.broadcasted_iota(jnp.int32, (half, D_MAIN), 0) == 0
            acc = u * cw_ref[0:1, :]
            for k in range(1, CONV_W):
                delayed = jnp.where(first_row, tails[k - 1:k, :], pltpu.roll(acc, 1, 0))
                tails[k - 1:k, :] = acc[half - 1:half, :]
                acc = delayed + u * cw_ref[k:k + 1, :]
            uc_s[rows, :] = acc + cb_ref[...]

        def gate(j):
            def run():
                sl = slice(GATE_TILE * j, GATE_TILE * (j + 1))
                uc = uc_s[rows, sl]
                pre = jnp.dot(uc.astype(BF16), wg_ref[j], preferred_element_type=F32)
                a, b = _rglru_coeffs(uc, pre, ba_ref[:, sl], bx_ref[:, sl], decay_c[:, sl])
                for jj in range(GATE_TILE // LANES):
                    slab = j * (GATE_TILE // LANES) + jj
                    srows = slice(SUBLANES * slab, SUBLANES * (slab + 1))
                    cols = slice(LANES * jj, LANES * (jj + 1))
                    a3[g0:g0 + gn, srows, :] = a[:, cols].reshape(gn, SUBLANES, LANES)
                    b3[g0:g0 + gn, srows, :] = b[:, cols].reshape(gn, SUBLANES, LANES)
            return run

        return [conv] + [gate(j) for j in range(N_GATE_TILES)]

    def scan(r0):
        def piece(first, count):
            def run():
                h = carry[0]
                for g in range(first, first + count):
                    for r in range(SUBLANES):
                        step = pl.ds(r, N_SLABS, stride=SUBLANES)
                        h = a3[g, step, :] * h + b3[g, step, :]
                        h3[g, step, :] = h
                carry[0] = h
            return run

        g0, gn = r0 // SUBLANES, half // SUBLANES
        count = max(gn // SCAN_CHUNKS, 1)
        return [piece(g, count) for g in range(g0, g0 + gn, count)]

    def gating(r0):
        def slab(j):
            def run():
                g0, gn = r0 // SUBLANES, half // SUBLANES
                y = h3[g0:g0 + gn, SUBLANES * j:SUBLANES * (j + 1), :].reshape(half, LANES)
                g_main = proj_s[r0:r0 + half, D_MAIN + LANES * j:D_MAIN + LANES * (j + 1)]
                z[r0:r0 + half, LANES * j:LANES * (j + 1)] = (y * _silu_of_half(g_main)).astype(BF16)
            return run
        return [slab(j) for j in range(N_SLABS)]

    out_project = lambda r0: _out_project_thunks(x_ref, z, wout_ref, xo_ref, r0, half)
    mem =lambda r0: _skewed(_mem_units(proj_s, mk_ref, mv_ref, z, r0, half), ATTN_SKEW)
    first, second = 0, half
    n_conv_in = 1 + D_MAIN // PROJ_CHUNK
    p_first, p_second = project(first), project(second)
    _run(p_first[:n_conv_in])
    _run(_interleave(p_first[n_conv_in:], gates(first)))
    _run(p_second[:n_conv_in])
    _run(_interleave(p_second[n_conv_in:], gates(second), mem(first), scan(first)))
    _run(_interleave(gating(first) + out_project(first), mem(second), scan(second), side_thunks))
    _run(gating(second))
    _run(out_project(second))

    h = carry[0]
    hc[...] = h

    @pl.when(t == pl.num_programs(1) - 1)
    def _():
        hl_ref[0] = h
        ct_ref[0] = proj_s[tile - SUBLANES:tile, 0:D_MAIN]

    side_end()


def _prompt_lru_part(x, W, w_in_b, w_out_b, layer, mk, mv_t, tile=256):
    batch, seq, _ = x.shape
    tok = lambda b, t: (b, t, 0)
    per_b = lambda b, t: (b, 0, 0)
    mem = lambda b, t: (layer, b, 0, 0)
    in_specs = [
        pl.BlockSpec((1, tile, D_MODEL), tok),
        _layer_spec((1, D_MODEL), layer), _const_spec((D_MODEL, D_IN)), _const_spec((D_CAT, D_MODEL)),
        _layer_spec((CONV_W, D_MAIN), layer), _layer_spec((1, D_MAIN), layer),
        _layer_spec((N_GATE_TILES, GATE_TILE, 2 * GATE_TILE), layer),
        _layer_spec((1, D_MAIN), layer), _layer_spec((1, D_MAIN), layer), _layer_spec((1, D_MAIN), layer),
        pl.BlockSpec((None, 1, N_MEM, D_MEMW), mem), pl.BlockSpec((None, 1, D_MEMW, N_MEM), mem),
    ]
    args = [x, W["norm_g"], w_in_b, w_out_b, W["conv_w"], W["conv_b"], W["w_gate"],
            W["ba"], W["bx"], W["lam"], mk, mv_t]
    out_specs = [pl.BlockSpec((1, tile, D_MODEL), tok),
                 pl.BlockSpec((1, SUBLANES, LANES), per_b),
                 pl.BlockSpec((1, SUBLANES, D_MAIN), per_b)]
    out_shape = [jax.ShapeDtypeStruct((batch, seq, D_MODEL), F32),
                 jax.ShapeDtypeStruct((batch, SUBLANES, LANES), F32),
                 jax.ShapeDtypeStruct((batch, SUBLANES, D_MAIN), F32)]
    groups = tile // SUBLANES
    return dict(
        body=functools.partial(_prompt_lru_kernel, tile), grid=(batch, seq // tile),
        in_specs=in_specs, args=args, out_specs=out_specs, out_shape=out_shape,
        scratch=[pltpu.VMEM((SUBLANES, D_MAIN), F32),
                 pltpu.VMEM((groups, SUBLANES * N_SLABS, LANES), F32),
                 pltpu.VMEM((groups, SUBLANES * N_SLABS, LANES), F32),
                 pltpu.VMEM((groups, SUBLANES * N_SLABS, LANES), F32),
                 pltpu.VMEM((SUBLANES, LANES), F32),
                 pltpu.VMEM((tile, D_CAT), BF16),
                 pltpu.VMEM((tile, D_MODEL), BF16),
                 pltpu.VMEM((tile, D_IN), F32),
                 pltpu.VMEM((tile, D_MAIN), F32)],
        name="lru")


def _sink_softmax(s, sink, axis=-1):
    m = jnp.maximum(jnp.max(s, axis=axis, keepdims=True), sink)
    p = jnp.exp2(s - m)
    den = jnp.sum(p, axis=axis, keepdims=True) + jnp.exp2(sink - m)
    return p, den


def _prompt_swa_kernel(make_kv, final, tile, sink_row, make_side, *refs):
    it = iter(refs)
    take = lambda n: [next(it) for _ in range(n)]
    x_ref, ng_ref, win_ref, wout_ref, mk_ref, mv_ref, sink_ref = take(7)
    if make_kv:
        kvg_ref, wkv_ref = take(2)
    else:
        kp_ref, kc_ref, vp_ref, vc_ref = take(4)
    cos_ref, sin_ref = take(2)
    fg_ref = take(1)[0] if final else None
    xo_ref, = take(1)
    if make_kv:
        ko_ref, vo_ref = take(2)
    if final:
        y_ref, = take(1)
    z, qz, kcat, vcat_t, hn_s, proj_s = take(6)
    half = tile // 2
    t = pl.program_id(1)
    side_begin, side_thunks, side_end = make_side(dict(ng=ng_ref, win=win_ref, wout=wout_ref, fg=fg_ref))
    side_begin()

    def rope_queries(r0):
        rows = slice(r0, r0 + half)

        def slab(j):
            def run():
                cos = cos_ref[rows, :]
                sin = sin_ref[rows, :]
                lane = lax.broadcasted_iota(jnp.int32, (half, LANES), 1)
                lo = lane < HEAD_DIM
                first_half = (lane % HEAD_DIM) < (HEAD_DIM // 2)
                c = j // (GROUP // 2)
                xq = proj_s[rows, LANES * j:LANES * (j + 1)] * (HEAD_DIM ** -0.5 * LOG2_E)
                r32 = pltpu.roll(xq, HEAD_DIM // 2, 1)
                r96 = pltpu.roll(xq, LANES - HEAD_DIM // 2, 1)
                r64 = pltpu.roll(xq, HEAD_DIM, 1)
                stay = xq * cos + jnp.where(first_half, r96, r32) * sin
                move = r64 * cos + jnp.where(first_half, r32, r96) * sin
                keep = lo if c == 0 else ~lo
                even, odd = (stay, move) if c == 0 else (move, stay)
                qz[rows, 2 * LANES * j:2 * LANES * j + LANES] = jnp.where(keep, even, 0.0).astype(BF16)
                qz[rows, 2 * LANES * j + LANES:2 * LANES * (j + 1)] = jnp.where(keep, odd, 0.0).astype(BF16)
            return run

        return [slab(j) for j in range(N_SLABS)]

    if make_kv:
        @pl.when(t == 0)
        def _():
            kcat[0:WINDOW, :] = jnp.zeros((WINDOW, D_KV), BF16)
            vcat_t[:, 0:WINDOW] = jnp.zeros((D_KV, WINDOW), BF16)

        @pl.when(t > 0)
        def _():
            kcat[0:WINDOW, :] = kcat[tile:tile + WINDOW, :]
            vcat_t[:, 0:WINDOW] = vcat_t[:, tile:tile + WINDOW]

        def shared_kv(r0):
            def run():
                rows = slice(r0, r0 + half)
                kvn = _rms(x_ref[0, rows, :], kvg_ref[...]).astype(BF16)
                kv = jnp.dot(kvn, wkv_ref[...], preferred_element_type=F32)
                k = _rope(kv[:, :D_KV], cos_ref[rows, :], sin_ref[rows, :])
                v = kv[:, D_KV:]
                ko_ref[0, rows, :] = k
                vo_ref[0, rows, :] = v
                kcat[WINDOW + r0:WINDOW + r0 + half, :] = k.astype(BF16)
                vcat_t[:, WINDOW + r0:WINDOW + r0 + half] = v.T.astype(BF16)
            return [run]
    else:
        kcat[0:WINDOW, :] = kp_ref[0].astype(BF16)
        kcat[WINDOW:, :] = kc_ref[0].astype(BF16)
        vcat_t[:, 0:WINDOW] = vp_ref[0].T.astype(BF16)
        vcat_t[:, WINDOW:] = vc_ref[0].T.astype(BF16)
        shared_kv = lambda r0: []

    key_r = lax.broadcasted_iota(jnp.int32, (WINDOW, 2 * WINDOW), 0)
    query = lax.broadcasted_iota(jnp.int32, (WINDOW, 2 * WINDOW), 1) % WINDOW
    from_prev = key_r > query

    def scores(n, j):
        rows = slice(WINDOW * n, WINDOW * (n + 1))
        keys = kcat[WINDOW * n:WINDOW * (n + 2), :]
        q2 = jnp.concatenate([qz[rows, 2 * LANES * j:2 * LANES * j + LANES],
                              qz[rows, 2 * LANES * j + LANES:2 * LANES * (j + 1)]], axis=0)
        s = lax.dot_general(keys, q2, _NT, preferred_element_type=F32)
        s_prev, s_own = s[:WINDOW, :], s[WINDOW:, :]
        if n == 0:
            s_prev = jnp.where(t > 0, s_prev, NEG)
        visible = jnp.where(from_prev, s_prev, s_own)
        p0, den0 = _sink_softmax(visible[:, :WINDOW], sink_ref[sink_row, 2 * j] * LOG2_E, axis=0)
        p1, den1 = _sink_softmax(visible[:, WINDOW:], sink_ref[sink_row, 2 * j + 1] * LOG2_E, axis=0)
        p = jnp.concatenate([p0, p1], axis=1).astype(BF16)
        zero = jnp.zeros_like(p)
        return jnp.concatenate([jnp.where(from_prev, p, zero), jnp.where(from_prev, zero, p)], axis=0), den0, den1

    def outputs(n, j, p, den0, den1):
        rows = slice(WINDOW * n, WINDOW * (n + 1))
        c = j // (GROUP // 2)
        vals_t = vcat_t[HEAD_DIM * c:HEAD_DIM * (c + 1), WINDOW * n:WINDOW * (n + 2)]
        o_t = jnp.dot(vals_t, p, preferred_element_type=F32)
        y_t = jnp.concatenate([o_t[:, :WINDOW] / den0, o_t[:, WINDOW:] / den1], axis=0)
        g_main = proj_s[rows, D_MAIN + LANES * j:D_MAIN + LANES * (j + 1)]
        z[rows, LANES * j:LANES * (j + 1)] = (y_t.T * _silu_of_half(g_main)).astype(BF16)

    def attend(r0):
        units = [(functools.partial(scores, n, j), functools.partial(outputs, n, j))
                 for n in range(r0 // WINDOW, (r0 + half) // WINDOW) for j in range(N_SLABS)]
        return _skewed(units + _mem_units(proj_s, mk_ref, mv_ref, z, r0, half), ATTN_SKEW)

    def out_project(r0):
        rows = slice(r0, r0 + half)

        def final_norm():
            y_ref[0, rows, :] = _rms(xo_ref[0, rows, :], fg_ref[...])

        return _out_project_thunks(x_ref, z, wout_ref, xo_ref, r0, half) + ([final_norm] if final else [])

    project = lambda r0: _project_thunks(x_ref, ng_ref, win_ref, hn_s, proj_s, r0, half)
    first, second = 0, half
    _run(project(first) + shared_kv(first))
    _run(_interleave(project(second) + shared_kv(second), rope_queries(first)))
    _run(_interleave(attend(first), rope_queries(second)))
    _run(_interleave(attend(second), out_project(first), side_thunks))
    _run(out_project(second))
    side_end()


def _prompt_swa_part(x, W, w_in_b, w_out_b, layer, mk, mv_t, kv, cos, sin, final_g=None, tile=256):
    batch, seq, _ = x.shape
    final = final_g is not None
    make_kv = kv is None
    blocks = tile // WINDOW
    tok = lambda b, t: (b, t, 0)
    mem = lambda b, t: (layer, b, 0, 0)
    prev_blk = lambda b, t: (b, jnp.maximum(t * blocks - 1, 0), 0)
    in_specs = [
        pl.BlockSpec((1, tile, D_MODEL), tok),
        _layer_spec((1, D_MODEL), layer), _const_spec((D_MODEL, D_IN)), _const_spec((D_CAT, D_MODEL)),
        pl.BlockSpec((None, 1, N_MEM, D_MEMW), mem), pl.BlockSpec((None, 1, D_MEMW, N_MEM), mem),
        pl.BlockSpec(memory_space=pltpu.SMEM)]
    args = [x, W["norm_g"], w_in_b, w_out_b, mk, mv_t, W["sinks"]]
    if make_kv:
        in_specs += [_const_spec((1, D_MODEL)), _const_spec((D_MODEL, 2 * D_KV))]
        args += [W["kv_norm_g"], W["w_kv"]]
    else:
        k, v = kv
        in_specs += [pl.BlockSpec((1, WINDOW, D_KV), prev_blk), pl.BlockSpec((1, tile, D_KV), tok),
                     pl.BlockSpec((1, WINDOW, D_KV), prev_blk), pl.BlockSpec((1, tile, D_KV), tok)]
        args += [k, k, v, v]
    in_specs += [pl.BlockSpec((tile, LANES), lambda b, t: (t, 0)), pl.BlockSpec((tile, LANES), lambda b, t: (t, 0))]
    args += [cos, sin]
    out_specs = [pl.BlockSpec((1, tile, D_MODEL), tok)]
    out_shape = [jax.ShapeDtypeStruct((batch, seq, D_MODEL), F32)]
    if make_kv:
        out_specs += [pl.BlockSpec((1, tile, D_KV), tok)] * 2
        out_shape += [jax.ShapeDtypeStruct((batch, seq, D_KV), F32)] * 2
    if final:
        in_specs.append(_const_spec((1, D_MODEL)))
        args.append(final_g)
        out_specs.append(pl.BlockSpec((1, tile, D_MODEL), tok))
        out_shape.append(jax.ShapeDtypeStruct((batch, seq, D_MODEL), F32))
    return dict(
        body=functools.partial(_prompt_swa_kernel, make_kv, final, tile, layer - N_A), grid=(batch, seq // tile),
        in_specs=in_specs, args=args, out_specs=out_specs, out_shape=out_shape,
        scratch=[pltpu.VMEM((tile, D_CAT), BF16),
                 pltpu.VMEM((tile, 2 * D_MAIN), BF16),
                 pltpu.VMEM((WINDOW + tile, D_KV), BF16),
                 pltpu.VMEM((D_KV, WINDOW + tile), BF16),
                 pltpu.VMEM((tile, D_MODEL), BF16),
                 pltpu.VMEM((tile, D_IN), F32)],
        name="swa" + ("_kv" if make_kv else "") + ("_final" if final else ""))


def _sample_mem_unit(b, bb, qm_s, mk_ref, mv_ref, om_s):
    def scores():
        row = lax.broadcasted_iota(jnp.int32, (SUBLANES, MEM_HD), 0)
        q = qm_s[pl.ds(b, 1), :]
        qh = jnp.zeros((SUBLANES, MEM_HD), F32)
        for hh in range(MEM_HEADS):
            q_head = jnp.broadcast_to(q[:, MEM_HD * hh:MEM_HD * (hh + 1)], (SUBLANES, MEM_HD))
            qh = jnp.where(row == hh, q_head, qh)
        s = lax.dot_general(qh.astype(BF16), mk_ref[bb].astype(BF16), _NT,
                            preferred_element_type=F32) * (MEM_HD ** -0.5 * LOG2_E)
        srow = lax.broadcasted_iota(jnp.int32, s.shape, 0)
        scol = lax.broadcasted_iota(jnp.int32, s.shape, 1)
        s = jnp.where((scol % MEM_HEADS) == srow, s, NEG)
        m = jnp.max(s, axis=-1, keepdims=True)
        p = jnp.exp2(s - m)
        return p.astype(BF16), jnp.sum(p, axis=-1, keepdims=True)

    def outputs(p, den):
        o = jnp.dot(p, mv_ref[bb].astype(BF16), preferred_element_type=F32) / den
        om_s[pl.ds(b, 1), :] = jnp.concatenate([o[hh:hh + 1, :] for hh in range(MEM_HEADS)], axis=1)

    return scores, outputs


def _sample_lru_side(seqs, step, n_steps, w, x_ref, h0_ref, c0_ref, mk_ref, mv_ref, xo_ref, ho_ref, co_ref,
                     z_s, qm_s, gm_s, om_s):
    def begin():
        @pl.when(step == 0)
        def _():
            hn = _rms(x_ref[...], w["ng"][...]).astype(BF16)
            proj = jnp.dot(hn, w["win"][...], preferred_element_type=F32)
            u = proj[:, :D_MAIN]
            taps = [c0_ref[k] for k in range(CONV_W - 1)] + [u]
            uc = w["cb"][...] + taps[0] * w["cw"][0:1, :]
            for k in range(1, CONV_W):
                uc = uc + taps[k] * w["cw"][k:k + 1, :]
            for k in range(CONV_W - 1):
                co_ref[k] = taps[k + 1]
            decay_c = _decay_exponent(w["lam"][...])
            ucb = uc.astype(BF16)
            for j in range(N_GATE_TILES):
                sl = slice(GATE_TILE * j, GATE_TILE * (j + 1))
                pre = jnp.dot(ucb[:, sl], w["wg"][j], preferred_element_type=F32)
                a, b = _rglru_coeffs(uc[:, sl], pre, w["ba"][:, sl], w["bx"][:, sl], decay_c[:, sl])
                h = a * h0_ref[:, sl] + b
                ho_ref[:, sl] = h
                z_s[:, sl] = (h * _silu_of_half(
                    proj[:, D_MAIN + GATE_TILE * j:D_MAIN + GATE_TILE * (j + 1)])).astype(BF16)
            qm_s[...] = proj[:, 2 * D_MAIN:2 * D_MAIN + D_MEMW]
            gm_s[...] = _silu_of_half(proj[:, 2 * D_MAIN + D_MEMW:])

    thunks = _skewed([_sample_mem_unit(step * seqs + bb, bb, qm_s, mk_ref, mv_ref, om_s) for bb in range(seqs)],
                     ATTN_SKEW)

    def end():
        @pl.when(step == n_steps - 1)
        def _():
            z_s[:, D_MAIN:] = (om_s[...] * gm_s[...]).astype(BF16)
            xo_ref[...] = x_ref[...] + jnp.dot(z_s[...], w["wout"][...], preferred_element_type=F32)

    return begin, thunks, end


def _cache_spec(layer, seqs, steps_per_row):
    return pl.BlockSpec((None, seqs, N_MEM * MEM_HEADS, MEM_HD),
                        lambda b, t: (layer, b * steps_per_row + t, 0, 0))


def _whole_spec(shape):
    zeros = (0,) * len(shape)
    return pl.BlockSpec(shape, lambda *_: zeros)


def _sample_lru_part(x, layer, state_h, state_conv_t, cache_k, cache_v, grid):
    batch = x.shape[0]
    seqs = batch // (grid[0] * grid[1])
    cache_spec = _cache_spec(layer, seqs, grid[1])
    return dict(
        body=functools.partial(_sample_lru_side, seqs),
        in_specs=[_const_spec((batch, D_MODEL)),
                  _layer_spec((batch, D_MAIN), layer), _layer_spec((CONV_W - 1, batch, D_MAIN), layer),
                  cache_spec, cache_spec],
        args=[x, state_h, state_conv_t, cache_k, cache_v],
        out_specs=[_whole_spec((batch, D_MODEL)), _whole_spec((batch, D_MAIN)),
                   _whole_spec((CONV_W - 1, batch, D_MAIN))],
        out_shape=[jax.ShapeDtypeStruct((batch, D_MODEL), F32),
                   jax.ShapeDtypeStruct((batch, D_MAIN), F32),
                   jax.ShapeDtypeStruct((CONV_W - 1, batch, D_MAIN), F32)],
        scratch=[pltpu.VMEM((batch, D_CAT), BF16), pltpu.VMEM((batch, D_MEMW), F32),
                 pltpu.VMEM((batch, D_MEMW), F32), pltpu.VMEM((batch, D_MEMW), F32)])


def _sample_swa_side(first, final, seqs, step, n_steps, w, *refs):
    (x_ref, sink_ref, mk_ref, mv_ref, wk_ref, wv_ref) = refs[:6]
    pos = 6
    if first:
        kvg_ref, wkv_ref = refs[pos:pos + 2]
        pos += 2
    inv_ref, sign_ref = refs[pos:pos + 2]
    pos += 2
    xo_ref = refs[pos]
    pos += 1
    if first:
        wko_ref, wvo_ref = refs[pos:pos + 2]
        pos += 2
    if final:
        y_ref = refs[pos]
        pos += 1
    z_s, q_s, gmain_s, y_s, qm_s, gm_s, om_s, kn_s, vn_s = refs[pos:]

    def begin():
        @pl.when(step == 0)
        def _():
            x = x_ref[...]
            hn = _rms(x, w["ng"][...]).astype(BF16)
            proj = jnp.dot(hn, w["win"][...], preferred_element_type=F32)
            ang = float(PAST_LEN) * inv_ref[...]
            cos = jnp.cos(ang)
            sin = jnp.sin(ang) * sign_ref[...]
            for j in range(N_SLABS):
                cols = slice(LANES * j, LANES * (j + 1))
                q_s[:, cols] = _rope(proj[:, cols], cos, sin)
            gmain_s[...] = _silu_of_half(proj[:, D_MAIN:2 * D_MAIN])
            qm_s[...] = proj[:, 2 * D_MAIN:2 * D_MAIN + D_MEMW]
            gm_s[...] = _silu_of_half(proj[:, 2 * D_MAIN + D_MEMW:])
            if first:
                kvn = _rms(x, kvg_ref[...]).astype(BF16)
                kv = jnp.dot(kvn, wkv_ref[...], preferred_element_type=F32)
                kn_s[...] = _rope(kv[:, :D_KV], cos, sin).T
                vn_s[...] = kv[:, D_KV:].T

    row = lax.broadcasted_iota(jnp.int32, (N_HEADS, LANES), 0)
    lane = lax.broadcasted_iota(jnp.int32, (N_HEADS, LANES), 1)
    kv_half = (lane // HEAD_DIM) == (row // GROUP)
    in_place = (row % 2) == (row // GROUP)
    lane1 = lax.broadcasted_iota(jnp.int32, (1, LANES), 1)
    seq_lane = lax.broadcasted_iota(jnp.int32, kn_s.shape, 1)
    slot = lax.broadcasted_iota(jnp.int32, (D_KV, WINDOW), 1)

    def slide(win_t, new_t_s, b):
        if new_t_s.shape[1] == WINDOW:
            col = pltpu.roll(new_t_s[...], WINDOW - 1 - b, 1)
        else:
            col = jnp.sum(jnp.where(seq_lane == b, new_t_s[...], 0.0), axis=1, keepdims=True)
        return jnp.where(slot == WINDOW - 1, col, pltpu.roll(win_t, WINDOW - 1, 1))

    def window_unit(b, bb):
        def scores():
            if first:
                kw_t = slide(wk_ref[bb], kn_s, b)
                wko_ref[bb] = kw_t
            else:
                kw_t = wk_ref[bb]
            q = jnp.broadcast_to(q_s[pl.ds(b, 1), :], (N_HEADS, D_MAIN))
            e = jnp.zeros((N_HEADS, LANES), F32)
            for j in range(N_SLABS):
                e = jnp.where((row // 2) == j, q[:, LANES * j:LANES * (j + 1)], e)
            qh = jnp.where(kv_half, jnp.where(in_place, e, pltpu.roll(e, HEAD_DIM, 1)), 0.0).astype(BF16)
            s = jnp.dot(qh, kw_t.astype(BF16), preferred_element_type=F32) * (HEAD_DIM ** -0.5 * LOG2_E)
            p, den = _sink_softmax(s, sink_ref[...] * LOG2_E)
            return p.astype(BF16), den

        def outputs(p, den):
            if first:
                vw_t = slide(wv_ref[bb], vn_s, b)
                wvo_ref[bb] = vw_t
            else:
                vw_t = wv_ref[bb]
            o = lax.dot_general(p, vw_t.astype(BF16), _NT, preferred_element_type=F32) / den
            f = jnp.where(in_place, o, pltpu.roll(o, HEAD_DIM, 1))
            y_s[pl.ds(b, 1), :] = jnp.concatenate(
                [jnp.where(lane1 < HEAD_DIM, f[2 * j:2 * j + 1, :], f[2 * j + 1:2 * j + 2, :])
                 for j in range(N_SLABS)], axis=1)

        return scores, outputs

    units = []
    for bb in range(seqs):
        units.append(window_unit(step * seqs + bb, bb))
        units.append(_sample_mem_unit(step * seqs + bb, bb, qm_s, mk_ref, mv_ref, om_s))

    def end():
        @pl.when(step == n_steps - 1)
        def _():
            z_s[:, :D_MAIN] = (y_s[...] * gmain_s[...]).astype(BF16)
            z_s[:, D_MAIN:] = (om_s[...] * gm_s[...]).astype(BF16)
            xn = x_ref[...] + jnp.dot(z_s[...], w["wout"][...], preferred_element_type=F32)
            xo_ref[...] = xn
            if final:
                y_ref[...] = _rms(xn, w["fg"][...])

    return begin, _skewed(units, ATTN_SKEW), end


def _sample_swa_part(x, W, layer, cache_k, cache_v, win_k_t, win_v_t, first, final, grid):
    batch = x.shape[0]
    seqs = batch // (grid[0] * grid[1])
    cache_spec = _cache_spec(layer, seqs, grid[1])
    win_spec = pl.BlockSpec((seqs, D_KV, WINDOW), lambda b, t: (b * grid[1] + t, 0, 0))
    in_specs = [_const_spec((batch, D_MODEL)), _layer_spec((N_HEADS, 1), layer - N_A),
                cache_spec, cache_spec, win_spec, win_spec]
    args = [x, W["sinks_col"], cache_k, cache_v, win_k_t, win_v_t]
    if first:
        in_specs += [_const_spec((1, D_MODEL)), _const_spec((D_MODEL, 2 * D_KV))]
        args += [W["kv_norm_g"], W["w_kv"]]
    in_specs += [_const_spec((1, LANES)), _const_spec((1, LANES))]
    args += [W["inv128"], W["sign128"]]
    out_specs = [_whole_spec((batch, D_MODEL))]
    out_shape = [jax.ShapeDtypeStruct((batch, D_MODEL), F32)]
    if first:
        out_specs += [win_spec, win_spec]
        out_shape += [jax.ShapeDtypeStruct((batch, D_KV, WINDOW), F32)] * 2
    if final:
        out_specs.append(_whole_spec((batch, D_MODEL)))
        out_shape.append(jax.ShapeDtypeStruct((batch, D_MODEL), F32))
    return dict(
        body=functools.partial(_sample_swa_side, first, final, seqs),
        in_specs=in_specs, args=args, out_specs=out_specs, out_shape=out_shape,
        scratch=[pltpu.VMEM((batch, D_CAT), BF16), pltpu.VMEM((batch, D_MAIN), F32),
                 pltpu.VMEM((batch, D_MAIN), F32), pltpu.VMEM((batch, D_MAIN), F32),
                 pltpu.VMEM((batch, D_MEMW), F32), pltpu.VMEM((batch, D_MEMW), F32),
                 pltpu.VMEM((batch, D_MEMW), F32),
                 pltpu.VMEM((D_KV, batch), F32), pltpu.VMEM((D_KV, batch), F32)])


def _cast_next_weights(wi_ref, scale_ref, wo_ref, wi_o_ref, wo_o_ref):
    wi_o_ref[...] = (wi_ref[...] * scale_ref[...]).astype(BF16)
    wo_o_ref[...] = wo_ref[...].astype(BF16)


def _cast_part(W, next_layer, grid):
    steps = grid[0] * grid[1]
    slab = lambda b, t: (b * grid[1] + t, 0)
    layer_slab = lambda b, t: (next_layer, b * grid[1] + t, 0)
    return dict(
        body=_cast_next_weights,
        in_specs=[pl.BlockSpec((None, D_MODEL // steps, D_IN), layer_slab), _const_spec((1, D_IN)),
                  pl.BlockSpec((None, D_CAT // steps, D_MODEL), layer_slab)],
        args=[W["w_in"], W["in_scale"], W["w_out"]],
        out_specs=[pl.BlockSpec((D_MODEL // steps, D_IN), slab), pl.BlockSpec((D_CAT // steps, D_MODEL), slab)],
        out_shape=[jax.ShapeDtypeStruct((D_MODEL, D_IN), BF16), jax.ShapeDtypeStruct((D_CAT, D_MODEL), BF16)],
        scratch=[])


def _layer_kernel(prompt_body, sample_body, extra_body, n_prompt, n_sample, n_extra, *refs):
    it = iter(refs)
    take = lambda n: [next(it) for _ in range(n)]
    p_in, s_in, e_in = take(n_prompt[0]), take(n_sample[0]), take(n_extra[0])
    p_out, s_out, e_out = take(n_prompt[1]), take(n_sample[1]), take(n_extra[1])
    p_scr, s_scr = take(n_prompt[2]), take(n_sample[2])
    if extra_body is not None:
        extra_body(*e_in, *e_out)
    step = pl.program_id(0) * pl.num_programs(1) + pl.program_id(1)
    n_steps = pl.num_programs(0) * pl.num_programs(1)
    make_side = lambda weights: sample_body(step, n_steps, weights, *s_in, *s_out, *s_scr)
    prompt_body(make_side, *p_in, *p_out, *p_scr)


def _layer_call(prompt, sample, extra=None):
    extra = extra or dict(body=None, in_specs=[], args=[], out_specs=[], out_shape=[], scratch=[])
    parts = (prompt, sample, extra)
    counts = lambda part: (len(part["in_specs"]), len(part["out_specs"]), len(part["scratch"]))
    gather = lambda key: [item for part in parts for item in part[key]]
    outs = pl.pallas_call(
        functools.partial(_layer_kernel, prompt["body"], sample["body"], extra["body"],
                          counts(prompt), counts(sample), counts(extra)),
        grid=prompt["grid"],
        in_specs=gather("in_specs"), out_specs=gather("out_specs"), out_shape=gather("out_shape"),
        scratch_shapes=gather("scratch"),
        compiler_params=pltpu.CompilerParams(
            dimension_semantics=("arbitrary", "arbitrary"), vmem_limit_bytes=VMEM_LIMIT_BYTES),
        name="layer_" + prompt["name"],
    )(*gather("args"))
    n_p, n_s = len(prompt["out_specs"]), len(sample["out_specs"])
    return outs[:n_p], outs[n_p:n_p + n_s], outs[n_p + n_s:]


def _trunks(xp, xs, W, mkb, mvb_t, cache_mem_k, cache_mem_v, state_h, state_conv, win_k, win_v, tile=512):
    pb, seq, _ = xp.shape
    sb = xs.shape[0]
    grid = (pb, seq // tile)
    cos, sin = _rope_tables(seq, W["inv128"], W["sign128"])
    xs = xs.reshape(sb, D_MODEL)
    ck = cache_mem_k.reshape(DEPTH, sb, N_MEM * MEM_HEADS, MEM_HD)
    cv = cache_mem_v.reshape(DEPTH, sb, N_MEM * MEM_HEADS, MEM_HD)
    conv_t = jnp.transpose(state_conv, (0, 2, 1, 3))
    wk = jnp.transpose(win_k, (0, 2, 3, 1)).reshape(sb, D_KV, WINDOW)
    wv = jnp.transpose(win_v, (0, 2, 3, 1)).reshape(sb, D_KV, WINDOW)
    hp, cp, hs, cs = [], [], [], []
    w_in_b, w_out_b = W["w_in0"], W["w_out0"]
    cast_next = lambda l: _cast_part(W, l + 1, grid) if l + 1 < DEPTH else None
    for l in range(N_A):
        p_out, s_out, (w_in_b, w_out_b) = _layer_call(
            _prompt_lru_part(xp, W, w_in_b, w_out_b, l, mkb, mvb_t, tile=tile),
            _sample_lru_part(xs, l, state_h, conv_t, ck, cv, grid), cast_next(l))
        xp, h_last, conv_tail = p_out
        hp.append(h_last.reshape(pb, D_MAIN))
        cp.append(conv_tail[:, SUBLANES - (CONV_W - 1):, :])
        xs, h, c = s_out
        hs.append(h)
        cs.append(c)
    yp = ys = kv = None
    for l in range(N_A, DEPTH):
        first, last = l == N_A, l == DEPTH - 1
        p_out, s_out, next_w = _layer_call(
            _prompt_swa_part(xp, W, w_in_b, w_out_b, l, mkb, mvb_t, kv, cos, sin,
                             final_g=W["final_g"] if last else None, tile=tile),
            _sample_swa_part(xs, W, l, ck, cv, wk, wv, first, last, grid), cast_next(l))
        if not last:
            w_in_b, w_out_b = next_w
        xp, xs = p_out[0], s_out[0]
        if first:
            kv = (p_out[1], p_out[2])
            wk, wv = s_out[1], s_out[2]
        if last:
            yp, ys = p_out[-1], s_out[-1]
    k, v = kv
    win_kp = k[:, seq - WINDOW:, :].reshape(pb, WINDOW, N_KV, HEAD_DIM)
    win_vp = v[:, seq - WINDOW:, :].reshape(pb, WINDOW, N_KV, HEAD_DIM)
    unslide = lambda w_t: jnp.transpose(w_t.reshape(sb, N_KV, HEAD_DIM, WINDOW), (0, 3, 1, 2))
    prompt_out = (yp, jnp.stack(hp), jnp.stack(cp), win_kp, win_vp)
    sample_out = (ys.reshape(sb, 1, D_MODEL), jnp.stack(hs), jnp.transpose(jnp.stack(cs), (0, 2, 1, 3)),
                  unslide(wk), unslide(wv))
    return prompt_out, sample_out


def kernel(x_prompt, x_sample, cache_mem_k, cache_mem_v, state_lru_h, state_conv, cache_win_k, cache_win_v, mem_prompt, norm_g, w_in, w_out, mem_norm_g, w_mem_kv, conv_w, conv_b, lru_wa, lru_ba, lru_wx, lru_bx, lru_lambda, kv_norm_g, w_kv, sinks, final_g):
    W = _prepare_weights(norm_g, w_in, w_out, conv_w, conv_b, lru_wa, lru_ba, lru_wx, lru_bx, lru_lambda, kv_norm_g, w_kv, sinks, final_g)
    mk, mv, mkb, mvb = _mem_kv(mem_prompt, mem_norm_g, w_mem_kv)
    (y_p, h_p, conv_p, wk_p, wv_p), (y_s, h_s, conv_s, wk_s, wv_s) = _trunks(
        x_prompt, x_sample, W, mkb, mvb, cache_mem_k, cache_mem_v, state_lru_h, state_conv,
        cache_win_k, cache_win_v)
    batch = x_prompt.shape[0]
    mk_p = mk.reshape(DEPTH, batch, N_MEM, MEM_HEADS, MEM_HD)
    mv_p = mv.reshape(DEPTH, batch, N_MEM, MEM_HEADS, MEM_HD)
    return (y_p, y_s, mk_p, mv_p, h_p, conv_p, wk_p, wv_p, h_s, conv_s, wk_s, wv_s)
```

```python
import functools

import jax
import jax.numpy as jnp
from jax import lax
from jax.experimental import pallas as pl
from jax.experimental.pallas import tpu as pltpu

F32 = jnp.float32
BF16 = jnp.bfloat16

D_MODEL = 1024
DEPTH = 4
N_A = DEPTH // 2
N_HEADS = 16
HEAD_DIM = 64
N_KV = 2
GROUP = N_HEADS // N_KV
D_MAIN = N_HEADS * HEAD_DIM
N_BLOCKS = 16
BLOCK_W = D_MAIN // N_BLOCKS
CONV_W = 4
LRU_C = 8.0
WINDOW = 128
ROPE_THETA = 10000.0
N_MEM = 256
MEM_HEADS = 4
MEM_HD = 128
D_MEMW = MEM_HEADS * MEM_HD
D_IN = 2 * D_MAIN + 2 * D_MEMW
D_CAT = D_MAIN + D_MEMW
D_KV = N_KV * HEAD_DIM
EPS = 1e-6
NEG = -1e30
LOG2_E = 1.4426950408889634
F32_TINY = 1.1754943508222875e-38
PAST_LEN = 8192

SUBLANES = 8
LANES = 128
N_SLABS = D_MAIN // LANES
GATE_TILE = 256
N_GATE_TILES = D_MAIN // GATE_TILE
VMEM_LIMIT_BYTES = 56 * 1024 * 1024
ROW_PAD = 4
ATTN_SKEW = 8
MEM_CHUNK = 256
SCAN_CHUNKS = 16
PROJ_CHUNK = 512
OUT_CHUNK = 256

_NT = (((1,), (1,)), ((), ()))


def _const_spec(shape):
    zeros = (0,) * len(shape)
    return pl.BlockSpec(shape, lambda *_: zeros, pipeline_mode=pl.Buffered(1))


def _layer_spec(shape, layer):
    zeros = (0,) * len(shape)
    return pl.BlockSpec((None,) + tuple(shape), lambda *_: (layer,) + zeros, pipeline_mode=pl.Buffered(1))


def _rms(x, g):
    return x * lax.rsqrt(jnp.mean(x * x, axis=-1, keepdims=True) + EPS) * g


def _silu_of_half(hx):
    return hx * jnp.tanh(hx) + hx


def _softplus(x):
    return jnp.maximum(x, 0.0) + jnp.log1p(jnp.exp(-jnp.abs(x)))


def _decay_exponent(lam):
    return (-0.5 * LRU_C * LOG2_E) * _softplus(-lam)


def _rope(x, cos, sin_signed):
    lane = lax.broadcasted_iota(jnp.int32, x.shape, 1)
    first_half = (lane % HEAD_DIM) < (HEAD_DIM // 2)
    swapped = jnp.where(first_half,
                        pltpu.roll(x, LANES - HEAD_DIM // 2, 1),
                        pltpu.roll(x, HEAD_DIM // 2, 1))
    return x * cos + swapped * sin_signed


def _rglru_coeffs(uc, half_pre, half_ba, half_bx, c):
    half = half_pre.shape[-1] // 2
    t_r = jnp.tanh(half_pre[:, :half] + half_ba)
    t_i = jnp.tanh(half_pre[:, half:] + half_bx)
    a = jnp.exp2(c * t_r + c)
    w = 1.0 - a * a
    root = w * lax.rsqrt(jnp.maximum(w, F32_TINY))
    return a, root * ((0.5 * t_i + 0.5) * uc)


def _mem_units(proj, mk_ref, mv_ref, z, row0, n_rows):
    chunk = min(n_rows, MEM_CHUNK)
    units = []
    for ch in range(n_rows // chunk):
        for hh in range(MEM_HEADS):
            rows = slice(row0 + chunk * ch, row0 + chunk * (ch + 1))
            cols = slice(MEM_HD * hh, MEM_HD * (hh + 1))
            q_cols = slice(2 * D_MAIN + MEM_HD * hh, 2 * D_MAIN + MEM_HD * (hh + 1))
            g_cols = slice(2 * D_MAIN + D_MEMW + MEM_HD * hh, 2 * D_MAIN + D_MEMW + MEM_HD * (hh + 1))
            z_cols = slice(D_MAIN + MEM_HD * hh, D_MAIN + MEM_HD * (hh + 1))

            def scores(rows=rows, cols=cols, q_cols=q_cols):
                q = proj[rows, q_cols].astype(BF16)
                s = lax.dot_general(mk_ref[0, :, cols], q, _NT,
                                    preferred_element_type=F32) * (MEM_HD ** -0.5 * LOG2_E)
                m = jnp.max(s, axis=0, keepdims=True)
                p = jnp.exp2(s - m)
                return p.astype(BF16), jnp.sum(p, axis=0, keepdims=True)

            def outputs(p, den, rows=rows, cols=cols, g_cols=g_cols, z_cols=z_cols):
                o_t = jnp.dot(mv_ref[0, cols, :], p, preferred_element_type=F32)
                z[rows, z_cols] = ((o_t / den).T * _silu_of_half(proj[rows, g_cols])).astype(BF16)

            units.append((scores, outputs))
    return units


def _skewed(units, skew):
    results = []
    first = lambda scores: lambda: results.append(scores())
    second = lambda outputs: lambda: outputs(*results.pop(0))
    thunks, pending = [], []
    for scores, outputs in units:
        thunks.append(first(scores))
        pending.append(second(outputs))
        if len(pending) > skew:
            thunks.append(pending.pop(0))
    return thunks + pending


def _interleave(*lists):
    lists = [l for l in lists if l]
    pos = [0] * len(lists)
    merged = []
    for _ in range(sum(len(l) for l in lists)):
        i = min((i for i in range(len(lists)) if pos[i] < len(lists[i])),
                key=lambda i: pos[i] / len(lists[i]))
        merged.append(lists[i][pos[i]])
        pos[i] += 1
    return merged


def _run(thunks):
    for thunk in thunks:
        thunk()


def _project_thunks(x_ref, ng_ref, win_ref, hn_s, proj_s, r0, n_rows):
    rows = slice(r0, r0 + n_rows)

    def norm():
        hn_s[rows, :] = _rms(x_ref[0, rows, :], ng_ref[...]).astype(BF16)

    def chunk(c0):
        def run():
            proj_s[rows, c0:c0 + PROJ_CHUNK] = jnp.dot(
                hn_s[rows, :], win_ref[:, c0:c0 + PROJ_CHUNK], preferred_element_type=F32)
        return run

    return [norm] + [chunk(c0) for c0 in range(0, D_IN, PROJ_CHUNK)]


def _out_project_thunks(x_ref, z, wout_ref, xo_ref, r0, n_rows):
    rows = slice(r0, r0 + n_rows)

    def chunk(c0):
        def run():
            cols = slice(c0, c0 + OUT_CHUNK)
            xo_ref[0, rows, cols] = x_ref[0, rows, cols] + jnp.dot(
                z[rows, :], wout_ref[:, cols], preferred_element_type=F32)
        return run

    return [chunk(c0) for c0 in range(0, D_MODEL, OUT_CHUNK)]


def _gate_tiles(wa, wx):
    per_tile = GATE_TILE // BLOCK_W
    eye = jnp.eye(per_tile, dtype=wa.dtype)

    def dense(w):
        w5 = w.reshape(N_A, N_GATE_TILES, per_tile, BLOCK_W, BLOCK_W)
        return jnp.einsum("ljicd,ik->ljickd", w5, eye).reshape(N_A, N_GATE_TILES, GATE_TILE, GATE_TILE)

    return jnp.concatenate([dense(wa), dense(wx)], axis=-1).astype(BF16)


def _prepare_weights(norm_g, w_in, w_out, conv_w, conv_b, lru_wa, lru_ba, lru_wx, lru_bx, lru_lambda,
                     kv_norm_g, w_kv, sinks, final_g):
    col = jnp.arange(D_IN)
    is_gate = ((col >= D_MAIN) & (col < 2 * D_MAIN)) | (col >= 2 * D_MAIN + D_MEMW)
    half = HEAD_DIM // 2
    inv = ROPE_THETA ** (-jnp.arange(half, dtype=F32) / half)
    sign = jnp.concatenate([-jnp.ones((half,), F32), jnp.ones((half,), F32)])
    rows = lambda a, n=ROW_PAD: jnp.pad(a, [(0, 0)] * (a.ndim - 2) + [(0, n - a.shape[-2]), (0, 0)])
    return dict(
        norm_g=rows(norm_g.reshape(DEPTH, 1, D_MODEL)),
        w_in0=(w_in[0] * jnp.where(is_gate, 0.5, 1.0).astype(F32)).astype(BF16),
        w_out0=w_out[0].astype(BF16),
        w_in=w_in, w_out=w_out, in_scale=rows(jnp.where(is_gate, 0.5, 1.0).astype(F32).reshape(1, D_IN)),
        conv_w=conv_w, conv_b=rows(conv_b.reshape(N_A, 1, D_MAIN)),
        w_gate=_gate_tiles(0.5 * lru_wa, 0.5 * lru_wx),
        ba=rows(0.5 * lru_ba.reshape(N_A, 1, D_MAIN)), bx=rows(0.5 * lru_bx.reshape(N_A, 1, D_MAIN)),
        lam=rows(lru_lambda.reshape(N_A, 1, D_MAIN)),
        sinks=sinks, sinks_col=rows(sinks.reshape(DEPTH - N_A, N_HEADS, 1), 2 * N_HEADS),
        kv_norm_g=rows(kv_norm_g.reshape(1, D_MODEL)), w_kv=w_kv.astype(BF16),
        final_g=rows(final_g.reshape(1, D_MODEL)),
        inv128=rows(jnp.tile(inv, LANES // half).reshape(1, LANES), 4 * SUBLANES),
        sign128=rows(jnp.tile(sign, LANES // HEAD_DIM).reshape(1, LANES), 4 * SUBLANES))


def _rope_table_kernel(inv_ref, sign_ref, cos_ref, sin_ref):
    rows = cos_ref.shape[0]
    bases = rows // WINDOW
    inv = inv_ref[0:1, :]
    offset = lax.broadcasted_iota(jnp.int32, (WINDOW, LANES), 0).astype(F32) * inv
    cos_o, sin_o = jnp.cos(offset), jnp.sin(offset)
    base_pos = pl.program_id(0) * rows + WINDOW * lax.broadcasted_iota(jnp.int32, (bases, LANES), 0)
    base = base_pos.astype(F32) * inv
    cos_b, sin_b = jnp.cos(base), jnp.sin(base)
    for s in range(bases):
        blk = slice(WINDOW * s, WINDOW * (s + 1))
        cb, sb = cos_b[s:s + 1, :], sin_b[s:s + 1, :]
        cos_ref[blk, :] = cb * cos_o - sb * sin_o
        sin_ref[blk, :] = (sb * cos_o + cb * sin_o) * sign_ref[0:1, :]


def _rope_tables(seq, inv128, sign128, rows=1024):
    rows = min(rows, seq)
    return pl.pallas_call(
        _rope_table_kernel,
        grid=(seq // rows,),
        in_specs=[_const_spec(inv128.shape), _const_spec(sign128.shape)],
        out_specs=[pl.BlockSpec((rows, LANES), lambda i: (i, 0)),
                   pl.BlockSpec((rows, LANES), lambda i: (i, 0))],
        out_shape=[jax.ShapeDtypeStruct((seq, LANES), F32)] * 2,
        name="rope_tables",
    )(inv128, sign128)


def _mem_kv_kernel(m_ref, g_ref, w_ref, k_ref, v_ref, kb_ref, vb_ref):
    batch = m_ref.shape[0]
    mn = _rms(m_ref[...].reshape(batch * N_MEM, D_MODEL), g_ref[0]).astype(BF16)
    kv = jnp.dot(mn, w_ref[0].astype(BF16), preferred_element_type=F32)
    for b in range(batch):
        rows = slice(N_MEM * b, N_MEM * (b + 1))
        k, v = kv[rows, :D_MEMW], kv[rows, D_MEMW:]
        for hh in range(MEM_HEADS):
            cols = slice(MEM_HD * hh, MEM_HD * (hh + 1))
            k_ref[0, b, pl.ds(hh, N_MEM, stride=MEM_HEADS), :] = k[:, cols]
            v_ref[0, b, pl.ds(hh, N_MEM, stride=MEM_HEADS), :] = v[:, cols]
        kb_ref[0, b] = k.astype(BF16)
        vb_ref[0, b] = v.T.astype(BF16)


def _mem_kv(mem_prompt, mem_norm_g, w_mem_kv):
    batch = mem_prompt.shape[0]
    layer_blk = lambda *shape: pl.BlockSpec((1,) + shape, lambda l: (l,) + (0,) * len(shape))
    return pl.pallas_call(
        _mem_kv_kernel,
        grid=(DEPTH,),
        in_specs=[_const_spec((batch, N_MEM, D_MODEL)), layer_blk(1, D_MODEL),
                  layer_blk(D_MODEL, 2 * D_MEMW)],
        out_specs=[layer_blk(batch, N_MEM * MEM_HEADS, MEM_HD), layer_blk(batch, N_MEM * MEM_HEADS, MEM_HD),
                   layer_blk(batch, N_MEM, D_MEMW), layer_blk(batch, D_MEMW, N_MEM)],
        out_shape=[jax.ShapeDtypeStruct((DEPTH, batch, N_MEM * MEM_HEADS, MEM_HD), F32)] * 2
                  + [jax.ShapeDtypeStruct((DEPTH, batch, N_MEM, D_MEMW), BF16),
                     jax.ShapeDtypeStruct((DEPTH, batch, D_MEMW, N_MEM), BF16)],
        compiler_params=pltpu.CompilerParams(vmem_limit_bytes=VMEM_LIMIT_BYTES),
        name="mem_kv",
    )(mem_prompt, mem_norm_g.reshape(DEPTH, 1, D_MODEL), w_mem_kv)


def _prompt_lru_kernel(tile, make_side, x_ref, ng_ref, win_ref, wout_ref, cw_ref, cb_ref, wg_ref, ba_ref,
                       bx_ref, lam_ref, mk_ref, mv_ref, xo_ref, hl_ref, ct_ref,
                       tails, a3, b3, h3, hc, z, hn_s, proj_s, uc_s):
    half = tile // 2
    t = pl.program_id(1)
    ng_ref, cb_ref, ba_ref, bx_ref, lam_ref = (r.at[0:1, :] for r in (ng_ref, cb_ref, ba_ref, bx_ref, lam_ref))
    hc = hc.at[0:SUBLANES, :]
    side_begin, side_thunks, side_end = make_side(dict(
        ng=ng_ref, win=win_ref, wout=wout_ref, cw=cw_ref, cb=cb_ref, wg=wg_ref, ba=ba_ref, bx=bx_ref, lam=lam_ref))

    @pl.when(t == 0)
    def _():
        tails[...] = jnp.zeros((SUBLANES, D_MAIN), F32)
        hc[...] = jnp.zeros((SUBLANES, LANES), F32)

    side_begin()

    decay_c = _decay_exponent(lam_ref[...])
    carry = [hc[...]]


    project = lambda r0: _project_thunks(x_ref, ng_ref, win_ref, hn_s, proj_s, r0, half)

    def gates(r0):
        rows = slice(r0, r0 + half)
        g0, gn = r0 // SUBLANES, half // SUBLANES

        def conv():
            u = proj_s[rows, 0:D_MAIN]
            first_row = lax.broadcasted_iota(jnp.int32, (half, D_MAIN), 0) == 0
            acc = u * cw_ref[0:1, :]
            for k in range(1, CONV_W):
                delayed = jnp.where(first_row, tails[k - 1:k, :], pltpu.roll(acc, 1, 0))
                tails[k - 1:k, :] = acc[half - 1:half, :]
                acc = delayed + u * cw_ref[k:k + 1, :]
            uc_s[rows, :] = acc + cb_ref[...]

        def gate(j):
            def run():
                sl = slice(GATE_TILE * j, GATE_TILE * (j + 1))
                uc = uc_s[rows, sl]
                pre = jnp.dot(uc.astype(BF16), wg_ref[j], preferred_element_type=F32)
                a, b = _rglru_coeffs(uc, pre, ba_ref[:, sl], bx_ref[:, sl], decay_c[:, sl])
                for jj in range(GATE_TILE // LANES):
                    slab = j * (GATE_TILE // LANES) + jj
                    srows = slice(SUBLANES * slab, SUBLANES * (slab + 1))
                    cols = slice(LANES * jj, LANES * (jj + 1))
                    a3[g0:g0 + gn, srows, :] = a[:, cols].reshape(gn, SUBLANES, LANES)
                    b3[g0:g0 + gn, srows, :] = b[:, cols].reshape(gn, SUBLANES, LANES)
            return run

        return [conv] + [gate(j) for j in range(N_GATE_TILES)]

    def scan(r0):
        def piece(first, count):
            def run():
                h = carry[0]
                for g in range(first, first + count):
                    for r in range(SUBLANES):
                        step = pl.ds(r, N_SLABS, stride=SUBLANES)
                        h = a3[g, step, :] * h + b3[g, step, :]
                        h3[g, step, :] = h
                carry[0] = h
            return run

        g0, gn = r0 // SUBLANES, half // SUBLANES
        count = max(gn // SCAN_CHUNKS, 1)
        return [piece(g, count) for g in range(g0, g0 + gn, count)]

    def gating(r0):
        def slab(j):
            def run():
                g0, gn = r0 // SUBLANES, half // SUBLANES
                y = h3[g0:g0 + gn, SUBLANES * j:SUBLANES * (j + 1), :].reshape(half, LANES)
                g_main = proj_s[r0:r0 + half, D_MAIN + LANES * j:D_MAIN + LANES * (j + 1)]
                z[r0:r0 + half, LANES * j:LANES * (j + 1)] = (y * _silu_of_half(g_main)).astype(BF16)
            return run
        return [slab(j) for j in range(N_SLABS)]

    out_project = lambda r0: _out_project_thunks(x_ref, z, wout_ref, xo_ref, r0, half)
    mem =lambda r0: _skewed(_mem_units(proj_s, mk_ref, mv_ref, z, r0, half), ATTN_SKEW)
    first, second = 0, half
    n_conv_in = 1 + D_MAIN // PROJ_CHUNK
    p_first, p_second = project(first), project(second)
    _run(p_first[:n_conv_in])
    _run(_interleave(p_first[n_conv_in:], gates(first)))
    _run(p_second[:n_conv_in])
    _run(_interleave(p_second[n_conv_in:], gates(second), mem(first), scan(first)))
    _run(_interleave(gating(first) + out_project(first), mem(second), scan(second), side_thunks))
    _run(gating(second))
    _run(out_project(second))

    h = carry[0]
    hc[...] = h

    @pl.when(t == pl.num_programs(1) - 1)
    def _():
        hl_ref[0, 0:SUBLANES, :] = h
        hl_ref[0, SUBLANES:, :] = jnp.zeros((SUBLANES, LANES), F32)
        ct_ref[0] = proj_s[tile - SUBLANES:tile, 0:D_MAIN]

    side_end()


def _prompt_lru_part(x, W, w_in_b, w_out_b, layer, mk, mv_t, tile=256):
    batch, seq, _ = x.shape
    tok = lambda b, t: (b, t, 0)
    per_b = lambda b, t: (b, 0, 0)
    mem = lambda b, t: (layer, b, 0, 0)
    in_specs = [
        pl.BlockSpec((1, tile, D_MODEL), tok),
        _layer_spec((ROW_PAD, D_MODEL), layer), _const_spec((D_MODEL, D_IN)), _const_spec((D_CAT, D_MODEL)),
        _layer_spec((CONV_W, D_MAIN), layer), _layer_spec((ROW_PAD, D_MAIN), layer),
        _layer_spec((N_GATE_TILES, GATE_TILE, 2 * GATE_TILE), layer),
        _layer_spec((ROW_PAD, D_MAIN), layer), _layer_spec((ROW_PAD, D_MAIN), layer),
        _layer_spec((ROW_PAD, D_MAIN), layer),
        pl.BlockSpec((None, 1, N_MEM, D_MEMW), mem), pl.BlockSpec((None, 1, D_MEMW, N_MEM), mem),
    ]
    args = [x, W["norm_g"], w_in_b, w_out_b, W["conv_w"], W["conv_b"], W["w_gate"],
            W["ba"], W["bx"], W["lam"], mk, mv_t]
    out_specs = [pl.BlockSpec((1, tile, D_MODEL), tok),
                 pl.BlockSpec((1, 2 * SUBLANES, LANES), per_b),
                 pl.BlockSpec((1, SUBLANES, D_MAIN), per_b)]
    out_shape = [jax.ShapeDtypeStruct((batch, seq, D_MODEL), F32),
                 jax.ShapeDtypeStruct((batch, 2 * SUBLANES, LANES), F32),
                 jax.ShapeDtypeStruct((batch, SUBLANES, D_MAIN), F32)]
    groups = tile // SUBLANES
    return dict(
        body=functools.partial(_prompt_lru_kernel, tile), grid=(batch, seq // tile),
        in_specs=in_specs, args=args, out_specs=out_specs, out_shape=out_shape,
        scratch=[pltpu.VMEM((SUBLANES, D_MAIN), F32),
                 pltpu.VMEM((groups, SUBLANES * N_SLABS, LANES), F32),
                 pltpu.VMEM((groups, SUBLANES * N_SLABS, LANES), F32),
                 pltpu.VMEM((groups, SUBLANES * N_SLABS, LANES), F32),
                 pltpu.VMEM((4 * SUBLANES, LANES), F32),
                 pltpu.VMEM((tile, D_CAT), BF16),
                 pltpu.VMEM((tile, D_MODEL), BF16),
                 pltpu.VMEM((tile, D_IN), F32),
                 pltpu.VMEM((tile, D_MAIN), F32)],
        name="lru")


def _sink_softmax(s, sink, axis=-1):
    m = jnp.maximum(jnp.max(s, axis=axis, keepdims=True), sink)
    p = jnp.exp2(s - m)
    den = jnp.sum(p, axis=axis, keepdims=True) + jnp.exp2(sink - m)
    return p, den


def _prompt_swa_kernel(make_kv, final, tile, sink_row, make_side, *refs):
    it = iter(refs)
    take = lambda n: [next(it) for _ in range(n)]
    x_ref, ng_ref, win_ref, wout_ref, mk_ref, mv_ref, sink_ref = take(7)
    if make_kv:
        kvg_ref, wkv_ref = take(2)
    else:
        kp_ref, kc_ref, vp_ref, vc_ref = take(4)
    cos_ref, sin_ref = take(2)
    fg_ref = take(1)[0] if final else None
    xo_ref, = take(1)
    if make_kv:
        ko_ref, vo_ref = take(2)
    if final:
        y_ref, = take(1)
    z, qz, kcat, vcat_t, hn_s, proj_s = take(6)
    half = tile // 2
    t = pl.program_id(1)
    ng_ref = ng_ref.at[0:1, :]
    if make_kv:
        kvg_ref = kvg_ref.at[0:1, :]
    if final:
        fg_ref = fg_ref.at[0:1, :]
    side_begin, side_thunks, side_end = make_side(dict(ng=ng_ref, win=win_ref, wout=wout_ref, fg=fg_ref))
    side_begin()

    def rope_queries(r0):
        rows = slice(r0, r0 + half)

        def slab(j):
            def run():
                cos = cos_ref[rows, :]
                sin = sin_ref[rows, :]
                lane = lax.broadcasted_iota(jnp.int32, (half, LANES), 1)
                lo = lane < HEAD_DIM
                first_half = (lane % HEAD_DIM) < (HEAD_DIM // 2)
                c = j // (GROUP // 2)
                xq = proj_s[rows, LANES * j:LANES * (j + 1)] * (HEAD_DIM ** -0.5 * LOG2_E)
                r32 = pltpu.roll(xq, HEAD_DIM // 2, 1)
                r96 = pltpu.roll(xq, LANES - HEAD_DIM // 2, 1)
                r64 = pltpu.roll(xq, HEAD_DIM, 1)
                stay = xq * cos + jnp.where(first_half, r96, r32) * sin
                move = r64 * cos + jnp.where(first_half, r32, r96) * sin
                keep = lo if c == 0 else ~lo
                even, odd = (stay, move) if c == 0 else (move, stay)
                qz[rows, 2 * LANES * j:2 * LANES * j + LANES] = jnp.where(keep, even, 0.0).astype(BF16)
                qz[rows, 2 * LANES * j + LANES:2 * LANES * (j + 1)] = jnp.where(keep, odd, 0.0).astype(BF16)
            return run

        return [slab(j) for j in range(N_SLABS)]

    if make_kv:
        @pl.when(t == 0)
        def _():
            kcat[0:WINDOW, :] = jnp.zeros((WINDOW, D_KV), BF16)
            vcat_t[:, 0:WINDOW] = jnp.zeros((D_KV, WINDOW), BF16)

        @pl.when(t > 0)
        def _():
            kcat[0:WINDOW, :] = kcat[tile:tile + WINDOW, :]
            vcat_t[:, 0:WINDOW] = vcat_t[:, tile:tile + WINDOW]

        def shared_kv(r0):
            def run():
                rows = slice(r0, r0 + half)
                kvn = _rms(x_ref[0, rows, :], kvg_ref[...]).astype(BF16)
                kv = jnp.dot(kvn, wkv_ref[...], preferred_element_type=F32)
                k = _rope(kv[:, :D_KV], cos_ref[rows, :], sin_ref[rows, :])
                v = kv[:, D_KV:]
                ko_ref[0, rows, :] = k
                vo_ref[0, rows, :] = v
                kcat[WINDOW + r0:WINDOW + r0 + half, :] = k.astype(BF16)
                vcat_t[:, WINDOW + r0:WINDOW + r0 + half] = v.T.astype(BF16)
            return [run]
    else:
        kcat[0:WINDOW, :] = kp_ref[0].astype(BF16)
        kcat[WINDOW:, :] = kc_ref[0].astype(BF16)
        vcat_t[:, 0:WINDOW] = vp_ref[0].T.astype(BF16)
        vcat_t[:, WINDOW:] = vc_ref[0].T.astype(BF16)
        shared_kv = lambda r0: []

    key_r = lax.broadcasted_iota(jnp.int32, (WINDOW, 2 * WINDOW), 0)
    query = lax.broadcasted_iota(jnp.int32, (WINDOW, 2 * WINDOW), 1) % WINDOW
    from_prev = key_r > query

    def scores(n, j):
        rows = slice(WINDOW * n, WINDOW * (n + 1))
        keys = kcat[WINDOW * n:WINDOW * (n + 2), :]
        q2 = jnp.concatenate([qz[rows, 2 * LANES * j:2 * LANES * j + LANES],
                              qz[rows, 2 * LANES * j + LANES:2 * LANES * (j + 1)]], axis=0)
        s = lax.dot_general(keys, q2, _NT, preferred_element_type=F32)
        s_prev, s_own = s[:WINDOW, :], s[WINDOW:, :]
        if n == 0:
            s_prev = jnp.where(t > 0, s_prev, NEG)
        visible = jnp.where(from_prev, s_prev, s_own)
        p0, den0 = _sink_softmax(visible[:, :WINDOW], sink_ref[sink_row, 2 * j] * LOG2_E, axis=0)
        p1, den1 = _sink_softmax(visible[:, WINDOW:], sink_ref[sink_row, 2 * j + 1] * LOG2_E, axis=0)
        p = jnp.concatenate([p0, p1], axis=1).astype(BF16)
        zero = jnp.zeros_like(p)
        return jnp.concatenate([jnp.where(from_prev, p, zero), jnp.where(from_prev, zero, p)], axis=0), den0, den1

    def outputs(n, j, p, den0, den1):
        rows = slice(WINDOW * n, WINDOW * (n + 1))
        c = j // (GROUP // 2)
        vals_t = vcat_t[HEAD_DIM * c:HEAD_DIM * (c + 1), WINDOW * n:WINDOW * (n + 2)]
        o_t = jnp.dot(vals_t, p, preferred_element_type=F32)
        y_t = jnp.concatenate([o_t[:, :WINDOW] / den0, o_t[:, WINDOW:] / den1], axis=0)
        g_main = proj_s[rows, D_MAIN + LANES * j:D_MAIN + LANES * (j + 1)]
        z[rows, LANES * j:LANES * (j + 1)] = (y_t.T * _silu_of_half(g_main)).astype(BF16)

    def attend(r0):
        units = [(functools.partial(scores, n, j), functools.partial(outputs, n, j))
                 for n in range(r0 // WINDOW, (r0 + half) // WINDOW) for j in range(N_SLABS)]
        return _skewed(units + _mem_units(proj_s, mk_ref, mv_ref, z, r0, half), ATTN_SKEW)

    def out_project(r0):
        rows = slice(r0, r0 + half)

        def final_norm():
            y_ref[0, rows, :] = _rms(xo_ref[0, rows, :], fg_ref[...])

        return _out_project_thunks(x_ref, z, wout_ref, xo_ref, r0, half) + ([final_norm] if final else [])

    project = lambda r0: _project_thunks(x_ref, ng_ref, win_ref, hn_s, proj_s, r0, half)
    first, second = 0, half
    _run(project(first) + shared_kv(first))
    _run(_interleave(project(second) + shared_kv(second), rope_queries(first)))
    _run(_interleave(attend(first), rope_queries(second)))
    _run(_interleave(attend(second), out_project(first), side_thunks))
    _run(out_project(second))
    side_end()


def _prompt_swa_part(x, W, w_in_b, w_out_b, layer, mk, mv_t, kv, cos, sin, final_g=None, tile=256):
    batch, seq, _ = x.shape
    final = final_g is not None
    make_kv = kv is None
    blocks = tile // WINDOW
    tok = lambda b, t: (b, t, 0)
    mem = lambda b, t: (layer, b, 0, 0)
    prev_blk = lambda b, t: (b, jnp.maximum(t * blocks - 1, 0), 0)
    in_specs = [
        pl.BlockSpec((1, tile, D_MODEL), tok),
        _layer_spec((ROW_PAD, D_MODEL), layer), _const_spec((D_MODEL, D_IN)), _const_spec((D_CAT, D_MODEL)),
        pl.BlockSpec((None, 1, N_MEM, D_MEMW), mem), pl.BlockSpec((None, 1, D_MEMW, N_MEM), mem),
        pl.BlockSpec(memory_space=pltpu.SMEM)]
    args = [x, W["norm_g"], w_in_b, w_out_b, mk, mv_t, W["sinks"]]
    if make_kv:
        in_specs += [_const_spec((ROW_PAD, D_MODEL)), _const_spec((D_MODEL, 2 * D_KV))]
        args += [W["kv_norm_g"], W["w_kv"]]
    else:
        k, v = kv
        in_specs += [pl.BlockSpec((1, WINDOW, D_KV), prev_blk), pl.BlockSpec((1, tile, D_KV), tok),
                     pl.BlockSpec((1, WINDOW, D_KV), prev_blk), pl.BlockSpec((1, tile, D_KV), tok)]
        args += [k, k, v, v]
    in_specs += [pl.BlockSpec((tile, LANES), lambda b, t: (t, 0)), pl.BlockSpec((tile, LANES), lambda b, t: (t, 0))]
    args += [cos, sin]
    out_specs = [pl.BlockSpec((1, tile, D_MODEL), tok)]
    out_shape = [jax.ShapeDtypeStruct((batch, seq, D_MODEL), F32)]
    if make_kv:
        out_specs += [pl.BlockSpec((1, tile, D_KV), tok)] * 2
        out_shape += [jax.ShapeDtypeStruct((batch, seq, D_KV), F32)] * 2
    if final:
        in_specs.append(_const_spec((ROW_PAD, D_MODEL)))
        args.append(final_g)
        out_specs.append(pl.BlockSpec((1, tile, D_MODEL), tok))
        out_shape.append(jax.ShapeDtypeStruct((batch, seq, D_MODEL), F32))
    return dict(
        body=functools.partial(_prompt_swa_kernel, make_kv, final, tile, layer - N_A), grid=(batch, seq // tile),
        in_specs=in_specs, args=args, out_specs=out_specs, out_shape=out_shape,
        scratch=[pltpu.VMEM((tile, D_CAT), BF16),
                 pltpu.VMEM((tile, 2 * D_MAIN), BF16),
                 pltpu.VMEM((WINDOW + tile, D_KV), BF16),
                 pltpu.VMEM((D_KV, WINDOW + tile), BF16),
                 pltpu.VMEM((tile, D_MODEL), BF16),
                 pltpu.VMEM((tile, D_IN), F32)],
        name="swa" + ("_kv" if make_kv else "") + ("_final" if final else ""))


def _sample_mem_unit(b, bb, qm_s, mk_ref, mv_ref, om_s):
    def scores():
        row = lax.broadcasted_iota(jnp.int32, (SUBLANES, MEM_HD), 0)
        q = qm_s[pl.ds(b, 1), :]
        qh = jnp.zeros((SUBLANES, MEM_HD), F32)
        for hh in range(MEM_HEADS):
            q_head = jnp.broadcast_to(q[:, MEM_HD * hh:MEM_HD * (hh + 1)], (SUBLANES, MEM_HD))
            qh = jnp.where(row == hh, q_head, qh)
        s = lax.dot_general(qh.astype(BF16), mk_ref[bb].astype(BF16), _NT,
                            preferred_element_type=F32) * (MEM_HD ** -0.5 * LOG2_E)
        srow = lax.broadcasted_iota(jnp.int32, s.shape, 0)
        scol = lax.broadcasted_iota(jnp.int32, s.shape, 1)
        s = jnp.where((scol % MEM_HEADS) == srow, s, NEG)
        m = jnp.max(s, axis=-1, keepdims=True)
        p = jnp.exp2(s - m)
        return p.astype(BF16), jnp.sum(p, axis=-1, keepdims=True)

    def outputs(p, den):
        o = jnp.dot(p, mv_ref[bb].astype(BF16), preferred_element_type=F32) / den
        om_s[pl.ds(b, 1), :] = jnp.concatenate([o[hh:hh + 1, :] for hh in range(MEM_HEADS)], axis=1)

    return scores, outputs


def _sample_lru_side(seqs, step, n_steps, w, x_ref, h0_ref, c0_ref, mk_ref, mv_ref, xo_ref, ho_ref, co_ref,
                     z_s, qm_s, gm_s, om_s):
    def begin():
        @pl.when(step == 0)
        def _():
            hn = _rms(x_ref[...], w["ng"][...]).astype(BF16)
            proj = jnp.dot(hn, w["win"][...], preferred_element_type=F32)
            u = proj[:, :D_MAIN]
            taps = [c0_ref[k] for k in range(CONV_W - 1)] + [u]
            uc = w["cb"][...] + taps[0] * w["cw"][0:1, :]
            for k in range(1, CONV_W):
                uc = uc + taps[k] * w["cw"][k:k + 1, :]
            for k in range(CONV_W - 1):
                co_ref[k] = taps[k + 1]
            decay_c = _decay_exponent(w["lam"][...])
            ucb = uc.astype(BF16)
            for j in range(N_GATE_TILES):
                sl = slice(GATE_TILE * j, GATE_TILE * (j + 1))
                pre = jnp.dot(ucb[:, sl], w["wg"][j], preferred_element_type=F32)
                a, b = _rglru_coeffs(uc[:, sl], pre, w["ba"][:, sl], w["bx"][:, sl], decay_c[:, sl])
                h = a * h0_ref[:, sl] + b
                ho_ref[:, sl] = h
                z_s[:, sl] = (h * _silu_of_half(
                    proj[:, D_MAIN + GATE_TILE * j:D_MAIN + GATE_TILE * (j + 1)])).astype(BF16)
            qm_s[...] = proj[:, 2 * D_MAIN:2 * D_MAIN + D_MEMW]
            gm_s[...] = _silu_of_half(proj[:, 2 * D_MAIN + D_MEMW:])

    thunks = _skewed([_sample_mem_unit(step * seqs + bb, bb, qm_s, mk_ref, mv_ref, om_s) for bb in range(seqs)],
                     ATTN_SKEW)

    def end():
        @pl.when(step == n_steps - 1)
        def _():
            z_s[:, D_MAIN:] = (om_s[...] * gm_s[...]).astype(BF16)
            xo_ref[...] = x_ref[...] + jnp.dot(z_s[...], w["wout"][...], preferred_element_type=F32)

    return begin, thunks, end


def _cache_spec(layer, seqs, steps_per_row):
    return pl.BlockSpec((None, seqs, N_MEM * MEM_HEADS, MEM_HD),
                        lambda b, t: (layer, b * steps_per_row + t, 0, 0))


def _whole_spec(shape):
    zeros = (0,) * len(shape)
    return pl.BlockSpec(shape, lambda *_: zeros)


def _sample_lru_part(x, layer, state_h, state_conv_t, cache_k, cache_v, grid):
    batch = x.shape[0]
    seqs = batch // (grid[0] * grid[1])
    cache_spec = _cache_spec(layer, seqs, grid[1])
    return dict(
        body=functools.partial(_sample_lru_side, seqs),
        in_specs=[_const_spec((batch, D_MODEL)),
                  _layer_spec((batch, D_MAIN), layer), _layer_spec((CONV_W - 1, batch, D_MAIN), layer),
                  cache_spec, cache_spec],
        args=[x, state_h, state_conv_t, cache_k, cache_v],
        out_specs=[_whole_spec((batch, D_MODEL)), _whole_spec((batch, D_MAIN)),
                   _whole_spec((CONV_W - 1, batch, D_MAIN))],
        out_shape=[jax.ShapeDtypeStruct((batch, D_MODEL), F32),
                   jax.ShapeDtypeStruct((batch, D_MAIN), F32),
                   jax.ShapeDtypeStruct((CONV_W - 1, batch, D_MAIN), F32)],
        scratch=[pltpu.VMEM((batch, D_CAT), BF16), pltpu.VMEM((batch, D_MEMW), F32),
                 pltpu.VMEM((batch, D_MEMW), F32), pltpu.VMEM((batch, D_MEMW), F32)])


def _sample_swa_side(first, final, seqs, step, n_steps, w, *refs):
    (x_ref, sink_ref, mk_ref, mv_ref, wk_ref, wv_ref) = refs[:6]
    pos = 6
    if first:
        kvg_ref, wkv_ref = refs[pos:pos + 2]
        pos += 2
    inv_ref, sign_ref = (r.at[0:1, :] for r in refs[pos:pos + 2])
    sink_ref = sink_ref.at[0:N_HEADS, :]
    if first:
        kvg_ref = kvg_ref.at[0:1, :]
    pos += 2
    xo_ref = refs[pos]
    pos += 1
    if first:
        wko_ref, wvo_ref = refs[pos:pos + 2]
        pos += 2
    if final:
        y_ref = refs[pos]
        pos += 1
    z_s, q_s, gmain_s, y_s, qm_s, gm_s, om_s, kn_s, vn_s = refs[pos:]

    def begin():
        @pl.when(step == 0)
        def _():
            x = x_ref[...]
            hn = _rms(x, w["ng"][...]).astype(BF16)
            proj = jnp.dot(hn, w["win"][...], preferred_element_type=F32)
            ang = float(PAST_LEN) * inv_ref[...]
            cos = jnp.cos(ang)
            sin = jnp.sin(ang) * sign_ref[...]
            for j in range(N_SLABS):
                cols = slice(LANES * j, LANES * (j + 1))
                q_s[:, cols] = _rope(proj[:, cols], cos, sin)
            gmain_s[...] = _silu_of_half(proj[:, D_MAIN:2 * D_MAIN])
            qm_s[...] = proj[:, 2 * D_MAIN:2 * D_MAIN + D_MEMW]
            gm_s[...] = _silu_of_half(proj[:, 2 * D_MAIN + D_MEMW:])
            if first:
                kvn = _rms(x, kvg_ref[...]).astype(BF16)
                kv = jnp.dot(kvn, wkv_ref[...], preferred_element_type=F32)
                kn_s[...] = _rope(kv[:, :D_KV], cos, sin).T
                vn_s[...] = kv[:, D_KV:].T

    row = lax.broadcasted_iota(jnp.int32, (N_HEADS, LANES), 0)
    lane = lax.broadcasted_iota(jnp.int32, (N_HEADS, LANES), 1)
    kv_half = (lane // HEAD_DIM) == (row // GROUP)
    in_place = (row % 2) == (row // GROUP)
    lane1 = lax.broadcasted_iota(jnp.int32, (1, LANES), 1)
    seq_lane = lax.broadcasted_iota(jnp.int32, kn_s.shape, 1)
    slot = lax.broadcasted_iota(jnp.int32, (D_KV, WINDOW), 1)

    def slide(win_t, new_t_s, b):
        if new_t_s.shape[1] == WINDOW:
            col = pltpu.roll(new_t_s[...], WINDOW - 1 - b, 1)
        else:
            col = jnp.sum(jnp.where(seq_lane == b, new_t_s[...], 0.0), axis=1, keepdims=True)
        return jnp.where(slot == WINDOW - 1, col, pltpu.roll(win_t, WINDOW - 1, 1))

    def window_unit(b, bb):
        def scores():
            if first:
                kw_t = slide(wk_ref[bb], kn_s, b)
                wko_ref[bb] = kw_t
            else:
                kw_t = wk_ref[bb]
            q = jnp.broadcast_to(q_s[pl.ds(b, 1), :], (N_HEADS, D_MAIN))
            e = jnp.zeros((N_HEADS, LANES), F32)
            for j in range(N_SLABS):
                e = jnp.where((row // 2) == j, q[:, LANES * j:LANES * (j + 1)], e)
            qh = jnp.where(kv_half, jnp.where(in_place, e, pltpu.roll(e, HEAD_DIM, 1)), 0.0).astype(BF16)
            s = jnp.dot(qh, kw_t.astype(BF16), preferred_element_type=F32) * (HEAD_DIM ** -0.5 * LOG2_E)
            p, den = _sink_softmax(s, sink_ref[...] * LOG2_E)
            return p.astype(BF16), den

        def outputs(p, den):
            if first:
                vw_t = slide(wv_ref[bb], vn_s, b)
                wvo_ref[bb] = vw_t
            else:
                vw_t = wv_ref[bb]
            o = lax.dot_general(p, vw_t.astype(BF16), _NT, preferred_element_type=F32) / den
            f = jnp.where(in_place, o, pltpu.roll(o, HEAD_DIM, 1))
            y_s[pl.ds(b, 1), :] = jnp.concatenate(
                [jnp.where(lane1 < HEAD_DIM, f[2 * j:2 * j + 1, :], f[2 * j + 1:2 * j + 2, :])
                 for j in range(N_SLABS)], axis=1)

        return scores, outputs

    units = []
    for bb in range(seqs):
        units.append(window_unit(step * seqs + bb, bb))
        units.append(_sample_mem_unit(step * seqs + bb, bb, qm_s, mk_ref, mv_ref, om_s))

    def end():
        @pl.when(step == n_steps - 1)
        def _():
            z_s[:, :D_MAIN] = (y_s[...] * gmain_s[...]).astype(BF16)
            z_s[:, D_MAIN:] = (om_s[...] * gm_s[...]).astype(BF16)
            xn = x_ref[...] + jnp.dot(z_s[...], w["wout"][...], preferred_element_type=F32)
            xo_ref[...] = xn
            if final:
                y_ref[...] = _rms(xn, w["fg"][...])

    return begin, _skewed(units, ATTN_SKEW), end


def _sample_swa_part(x, W, layer, cache_k, cache_v, win_k_t, win_v_t, first, final, grid):
    batch = x.shape[0]
    seqs = batch // (grid[0] * grid[1])
    cache_spec = _cache_spec(layer, seqs, grid[1])
    win_spec = pl.BlockSpec((seqs, D_KV, WINDOW), lambda b, t: (b * grid[1] + t, 0, 0))
    in_specs = [_const_spec((batch, D_MODEL)), _layer_spec((2 * N_HEADS, 1), layer - N_A),
                cache_spec, cache_spec, win_spec, win_spec]
    args = [x, W["sinks_col"], cache_k, cache_v, win_k_t, win_v_t]
    if first:
        in_specs += [_const_spec((ROW_PAD, D_MODEL)), _const_spec((D_MODEL, 2 * D_KV))]
        args += [W["kv_norm_g"], W["w_kv"]]
    in_specs += [_const_spec(W["inv128"].shape), _const_spec(W["sign128"].shape)]
    args += [W["inv128"], W["sign128"]]
    out_specs = [_whole_spec((batch, D_MODEL))]
    out_shape = [jax.ShapeDtypeStruct((batch, D_MODEL), F32)]
    if first:
        out_specs += [win_spec, win_spec]
        out_shape += [jax.ShapeDtypeStruct((batch, D_KV, WINDOW), F32)] * 2
    if final:
        out_specs.append(_whole_spec((batch, D_MODEL)))
        out_shape.append(jax.ShapeDtypeStruct((batch, D_MODEL), F32))
    return dict(
        body=functools.partial(_sample_swa_side, first, final, seqs),
        in_specs=in_specs, args=args, out_specs=out_specs, out_shape=out_shape,
        scratch=[pltpu.VMEM((batch, D_CAT), BF16), pltpu.VMEM((batch, D_MAIN), F32),
                 pltpu.VMEM((batch, D_MAIN), F32), pltpu.VMEM((batch, D_MAIN), F32),
                 pltpu.VMEM((batch, D_MEMW), F32), pltpu.VMEM((batch, D_MEMW), F32),
                 pltpu.VMEM((batch, D_MEMW), F32),
                 pltpu.VMEM((D_KV, batch), F32), pltpu.VMEM((D_KV, batch), F32)])


def _cast_next_weights(wi_ref, scale_ref, wo_ref, wi_o_ref, wo_o_ref):
    wi_o_ref[...] = (wi_ref[...] * scale_ref[0:1, :]).astype(BF16)
    wo_o_ref[...] = wo_ref[...].astype(BF16)


def _cast_part(W, next_layer, grid):
    steps = grid[0] * grid[1]
    slab = lambda b, t: (b * grid[1] + t, 0)
    layer_slab = lambda b, t: (next_layer, b * grid[1] + t, 0)
    return dict(
        body=_cast_next_weights,
        in_specs=[pl.BlockSpec((None, D_MODEL // steps, D_IN), layer_slab), _const_spec((ROW_PAD, D_IN)),
                  pl.BlockSpec((None, D_CAT // steps, D_MODEL), layer_slab)],
        args=[W["w_in"], W["in_scale"], W["w_out"]],
        out_specs=[pl.BlockSpec((D_MODEL // steps, D_IN), slab), pl.BlockSpec((D_CAT // steps, D_MODEL), slab)],
        out_shape=[jax.ShapeDtypeStruct((D_MODEL, D_IN), BF16), jax.ShapeDtypeStruct((D_CAT, D_MODEL), BF16)],
        scratch=[])


def _layer_kernel(prompt_body, sample_body, extra_body, n_prompt, n_sample, n_extra, *refs):
    it = iter(refs)
    take = lambda n: [next(it) for _ in range(n)]
    p_in, s_in, e_in = take(n_prompt[0]), take(n_sample[0]), take(n_extra[0])
    p_out, s_out, e_out = take(n_prompt[1]), take(n_sample[1]), take(n_extra[1])
    p_scr, s_scr = take(n_prompt[2]), take(n_sample[2])
    if extra_body is not None:
        extra_body(*e_in, *e_out)
    step = pl.program_id(0) * pl.num_programs(1) + pl.program_id(1)
    n_steps = pl.num_programs(0) * pl.num_programs(1)
    make_side = lambda weights: sample_body(step, n_steps, weights, *s_in, *s_out, *s_scr)
    prompt_body(make_side, *p_in, *p_out, *p_scr)


def _layer_call(prompt, sample, extra=None):
    extra = extra or dict(body=None, in_specs=[], args=[], out_specs=[], out_shape=[], scratch=[])
    parts = (prompt, sample, extra)
    counts = lambda part: (len(part["in_specs"]), len(part["out_specs"]), len(part["scratch"]))
    gather = lambda key: [item for part in parts for item in part[key]]
    outs = pl.pallas_call(
        functools.partial(_layer_kernel, prompt["body"], sample["body"], extra["body"],
                          counts(prompt), counts(sample), counts(extra)),
        grid=prompt["grid"],
        in_specs=gather("in_specs"), out_specs=gather("out_specs"), out_shape=gather("out_shape"),
        scratch_shapes=gather("scratch"),
        compiler_params=pltpu.CompilerParams(
            dimension_semantics=("arbitrary", "arbitrary"), vmem_limit_bytes=VMEM_LIMIT_BYTES),
        name="layer_" + prompt["name"],
    )(*gather("args"))
    n_p, n_s = len(prompt["out_specs"]), len(sample["out_specs"])
    return outs[:n_p], outs[n_p:n_p + n_s], outs[n_p + n_s:]


def _trunks(xp, xs, W, mkb, mvb_t, cache_mem_k, cache_mem_v, state_h, state_conv, win_k, win_v, tile=512):
    pb, seq, _ = xp.shape
    sb = xs.shape[0]
    grid = (pb, seq // tile)
    cos, sin = _rope_tables(seq, W["inv128"], W["sign128"])
    xs = xs.reshape(sb, D_MODEL)
    ck = cache_mem_k.reshape(DEPTH, sb, N_MEM * MEM_HEADS, MEM_HD)
    cv = cache_mem_v.reshape(DEPTH, sb, N_MEM * MEM_HEADS, MEM_HD)
    conv_t = jnp.transpose(state_conv, (0, 2, 1, 3))
    wk = jnp.transpose(win_k, (0, 2, 3, 1)).reshape(sb, D_KV, WINDOW)
    wv = jnp.transpose(win_v, (0, 2, 3, 1)).reshape(sb, D_KV, WINDOW)
    hp, cp, hs, cs = [], [], [], []
    w_in_b, w_out_b = W["w_in0"], W["w_out0"]
    cast_next = lambda l: _cast_part(W, l + 1, grid) if l + 1 < DEPTH else None
    for l in range(N_A):
        p_out, s_out, (w_in_b, w_out_b) = _layer_call(
            _prompt_lru_part(xp, W, w_in_b, w_out_b, l, mkb, mvb_t, tile=tile),
            _sample_lru_part(xs, l, state_h, conv_t, ck, cv, grid), cast_next(l))
        xp, h_last, conv_tail = p_out
        hp.append(h_last[:, :SUBLANES, :].reshape(pb, D_MAIN))
        cp.append(conv_tail[:, SUBLANES - (CONV_W - 1):, :])
        xs, h, c = s_out
        hs.append(h)
        cs.append(c)
    yp = ys = kv = None
    for l in range(N_A, DEPTH):
        first, last = l == N_A, l == DEPTH - 1
        p_out, s_out, next_w = _layer_call(
            _prompt_swa_part(xp, W, w_in_b, w_out_b, l, mkb, mvb_t, kv, cos, sin,
                             final_g=W["final_g"] if last else None, tile=tile),
            _sample_swa_part(xs, W, l, ck, cv, wk, wv, first, last, grid), cast_next(l))
        if not last:
            w_in_b, w_out_b = next_w
        xp, xs = p_out[0], s_out[0]
        if first:
            kv = (p_out[1], p_out[2])
            wk, wv = s_out[1], s_out[2]
        if last:
            yp, ys = p_out[-1], s_out[-1]
    k, v = kv
    win_kp = k[:, seq - WINDOW:, :].reshape(pb, WINDOW, N_KV, HEAD_DIM)
    win_vp = v[:, seq - WINDOW:, :].reshape(pb, WINDOW, N_KV, HEAD_DIM)
    unslide = lambda w_t: jnp.transpose(w_t.reshape(sb, N_KV, HEAD_DIM, WINDOW), (0, 3, 1, 2))
    prompt_out = (yp, jnp.stack(hp), jnp.stack(cp), win_kp, win_vp)
    sample_out = (ys.reshape(sb, 1, D_MODEL), jnp.stack(hs), jnp.transpose(jnp.stack(cs), (0, 2, 1, 3)),
                  unslide(wk), unslide(wv))
    return prompt_out, sample_out


def kernel(x_prompt, x_sample, cache_mem_k, cache_mem_v, state_lru_h, state_conv, cache_win_k, cache_win_v, mem_prompt, norm_g, w_in, w_out, mem_norm_g, w_mem_kv, conv_w, conv_b, lru_wa, lru_ba, lru_wx, lru_bx, lru_lambda, kv_norm_g, w_kv, sinks, final_g):
    W = _prepare_weights(norm_g, w_in, w_out, conv_w, conv_b, lru_wa, lru_ba, lru_wx, lru_bx, lru_lambda, kv_norm_g, w_kv, sinks, final_g)
    mk, mv, mkb, mvb = _mem_kv(mem_prompt, mem_norm_g, w_mem_kv)
    (y_p, h_p, conv_p, wk_p, wv_p), (y_s, h_s, conv_s, wk_s, wv_s) = _trunks(
        x_prompt, x_sample, W, mkb, mvb, cache_mem_k, cache_mem_v, state_lru_h, state_conv,
        cache_win_k, cache_win_v)
    batch = x_prompt.shape[0]
    mk_p = mk.reshape(DEPTH, batch, N_MEM, MEM_HEADS, MEM_HD)
    mv_p = mv.reshape(DEPTH, batch, N_MEM, MEM_HEADS, MEM_HD)
    return (y_p, y_s, mk_p, mv_p, h_p, conv_p, wk_p, wv_p, h_s, conv_s, wk_s, wv_s)
```

```python
import functools

import jax
import jax.numpy as jnp
from jax import lax
from jax.experimental import pallas as pl
from jax.experimental.pallas import tpu as pltpu

F32 = jnp.float32
BF16 = jnp.bfloat16

D_MODEL = 1024
DEPTH = 4
N_A = DEPTH // 2
N_HEADS = 16
HEAD_DIM = 64
N_KV = 2
GROUP = N_HEADS // N_KV
D_MAIN = N_HEADS * HEAD_DIM
N_BLOCKS = 16
BLOCK_W = D_MAIN // N_BLOCKS
CONV_W = 4
LRU_C = 8.0
WINDOW = 128
ROPE_THETA = 10000.0
N_MEM = 256
MEM_HEADS = 4
MEM_HD = 128
D_MEMW = MEM_HEADS * MEM_HD
D_IN = 2 * D_MAIN + 2 * D_MEMW
D_CAT = D_MAIN + D_MEMW
D_KV = N_KV * HEAD_DIM
EPS = 1e-6
NEG = -1e30
LOG2_E = 1.4426950408889634
F32_TINY = 1.1754943508222875e-38
PAST_LEN = 8192

SUBLANES = 8
LANES = 128
N_SLABS = D_MAIN // LANES
GATE_TILE = 256
N_GATE_TILES = D_MAIN // GATE_TILE
VMEM_LIMIT_BYTES = 56 * 1024 * 1024
ATTN_SKEW = 8
MEM_CHUNK = 256
SCAN_CHUNKS = 16
PROJ_CHUNK = 512
OUT_CHUNK = 256

_NT = (((1,), (1,)), ((), ()))


def _const_spec(shape):
    zeros = (0,) * len(shape)
    return pl.BlockSpec(shape, lambda *_: zeros, pipeline_mode=pl.Buffered(1))


def _layer_spec(shape, layer):
    zeros = (0,) * len(shape)
    return pl.BlockSpec((None,) + tuple(shape), lambda *_: (layer,) + zeros, pipeline_mode=pl.Buffered(1))


def _rms(x, g):
    return x * lax.rsqrt(jnp.mean(x * x, axis=-1, keepdims=True) + EPS) * g


def _silu_of_half(hx):
    return hx * jnp.tanh(hx) + hx


def _softplus(x):
    return jnp.maximum(x, 0.0) + jnp.log1p(jnp.exp(-jnp.abs(x)))


def _decay_exponent(lam):
    return (-0.5 * LRU_C * LOG2_E) * _softplus(-lam)


def _rope(x, cos, sin_signed):
    lane = lax.broadcasted_iota(jnp.int32, x.shape, 1)
    first_half = (lane % HEAD_DIM) < (HEAD_DIM // 2)
    swapped = jnp.where(first_half,
                        pltpu.roll(x, LANES - HEAD_DIM // 2, 1),
                        pltpu.roll(x, HEAD_DIM // 2, 1))
    return x * cos + swapped * sin_signed


def _rglru_coeffs(uc, half_pre, half_ba, half_bx, c):
    half = half_pre.shape[-1] // 2
    t_r = jnp.tanh(half_pre[:, :half] + half_ba)
    t_i = jnp.tanh(half_pre[:, half:] + half_bx)
    a = jnp.exp2(c * t_r + c)
    w = 1.0 - a * a
    root = w * lax.rsqrt(jnp.maximum(w, F32_TINY))
    return a, root * ((0.5 * t_i + 0.5) * uc)


def _mem_units(proj, mk_ref, mv_ref, z, row0, n_rows):
    chunk = min(n_rows, MEM_CHUNK)
    units = []
    for ch in range(n_rows // chunk):
        for hh in range(MEM_HEADS):
            rows = slice(row0 + chunk * ch, row0 + chunk * (ch + 1))
            cols = slice(MEM_HD * hh, MEM_HD * (hh + 1))
            q_cols = slice(2 * D_MAIN + MEM_HD * hh, 2 * D_MAIN + MEM_HD * (hh + 1))
            g_cols = slice(2 * D_MAIN + D_MEMW + MEM_HD * hh, 2 * D_MAIN + D_MEMW + MEM_HD * (hh + 1))
            z_cols = slice(D_MAIN + MEM_HD * hh, D_MAIN + MEM_HD * (hh + 1))

            def scores(rows=rows, cols=cols, q_cols=q_cols):
                q = proj[rows, q_cols].astype(BF16)
                s = lax.dot_general(mk_ref[0, :, cols], q, _NT,
                                    preferred_element_type=F32) * (MEM_HD ** -0.5 * LOG2_E)
                m = jnp.max(s, axis=0, keepdims=True)
                p = jnp.exp2(s - m)
                return p.astype(BF16), jnp.sum(p, axis=0, keepdims=True)

            def outputs(p, den, rows=rows, cols=cols, g_cols=g_cols, z_cols=z_cols):
                o_t = jnp.dot(mv_ref[0, cols, :], p, preferred_element_type=F32)
                z[rows, z_cols] = ((o_t / den).T * _silu_of_half(proj[rows, g_cols])).astype(BF16)

            units.append((scores, outputs))
    return units


def _skewed(units, skew):
    results = []
    first = lambda scores: lambda: results.append(scores())
    second = lambda outputs: lambda: outputs(*results.pop(0))
    thunks, pending = [], []
    for scores, outputs in units:
        thunks.append(first(scores))
        pending.append(second(outputs))
        if len(pending) > skew:
            thunks.append(pending.pop(0))
    return thunks + pending


def _interleave(*lists):
    lists = [l for l in lists if l]
    pos = [0] * len(lists)
    merged = []
    for _ in range(sum(len(l) for l in lists)):
        i = min((i for i in range(len(lists)) if pos[i] < len(lists[i])),
                key=lambda i: pos[i] / len(lists[i]))
        merged.append(lists[i][pos[i]])
        pos[i] += 1
    return merged


def _run(thunks):
    for thunk in thunks:
        thunk()


def _project_thunks(x_ref, ng_ref, win_ref, hn_s, proj_s, r0, n_rows):
    rows = slice(r0, r0 + n_rows)

    def norm():
        hn_s[rows, :] = _rms(x_ref[0, rows, :], ng_ref[...]).astype(BF16)

    def chunk(c0):
        def run():
            proj_s[rows, c0:c0 + PROJ_CHUNK] = jnp.dot(
                hn_s[rows, :], win_ref[:, c0:c0 + PROJ_CHUNK], preferred_element_type=F32)
        return run

    return [norm] + [chunk(c0) for c0 in range(0, D_IN, PROJ_CHUNK)]


def _out_project_thunks(x_ref, z, wout_ref, xo_ref, r0, n_rows):
    rows = slice(r0, r0 + n_rows)

    def chunk(c0):
        def run():
            cols = slice(c0, c0 + OUT_CHUNK)
            xo_ref[0, rows, cols] = x_ref[0, rows, cols] + jnp.dot(
                z[rows, :], wout_ref[:, cols], preferred_element_type=F32)
        return run

    return [chunk(c0) for c0 in range(0, D_MODEL, OUT_CHUNK)]


def _gate_tiles(wa, wx):
    per_tile = GATE_TILE // BLOCK_W
    eye = jnp.eye(per_tile, dtype=wa.dtype)

    def dense(w):
        w5 = w.reshape(N_A, N_GATE_TILES, per_tile, BLOCK_W, BLOCK_W)
        return jnp.einsum("ljicd,ik->ljickd", w5, eye).reshape(N_A, N_GATE_TILES, GATE_TILE, GATE_TILE)

    return jnp.concatenate([dense(wa), dense(wx)], axis=-1).astype(BF16)


def _prepare_weights(norm_g, w_in, w_out, conv_w, conv_b, lru_wa, lru_ba, lru_wx, lru_bx, lru_lambda,
                     kv_norm_g, w_kv, sinks, final_g):
    col = jnp.arange(D_IN)
    is_gate = ((col >= D_MAIN) & (col < 2 * D_MAIN)) | (col >= 2 * D_MAIN + D_MEMW)
    half = HEAD_DIM // 2
    inv = ROPE_THETA ** (-jnp.arange(half, dtype=F32) / half)
    sign = jnp.concatenate([-jnp.ones((half,), F32), jnp.ones((half,), F32)])
    return dict(
        norm_g=norm_g.reshape(DEPTH, 1, D_MODEL),
        w_in=w_in, w_out=w_out, in_scale=jnp.where(is_gate, 0.5, 1.0).astype(F32).reshape(1, D_IN),
        conv_w=conv_w, conv_b=conv_b.reshape(N_A, 1, D_MAIN),
        w_gate=_gate_tiles(0.5 * lru_wa, 0.5 * lru_wx),
        ba=0.5 * lru_ba.reshape(N_A, 1, D_MAIN), bx=0.5 * lru_bx.reshape(N_A, 1, D_MAIN),
        lam=lru_lambda.reshape(N_A, 1, D_MAIN),
        sinks=sinks, sinks_col=sinks.reshape(DEPTH - N_A, N_HEADS, 1),
        kv_norm_g=kv_norm_g.reshape(1, D_MODEL), w_kv=w_kv.astype(BF16),
        final_g=final_g.reshape(1, D_MODEL),
        inv128=jnp.tile(inv, LANES // half).reshape(1, LANES),
        sign128=jnp.tile(sign, LANES // HEAD_DIM).reshape(1, LANES))


def _rope_table_kernel(inv_ref, sign_ref, cos_ref, sin_ref):
    rows = cos_ref.shape[0]
    bases = rows // WINDOW
    inv = inv_ref[...]
    offset = lax.broadcasted_iota(jnp.int32, (WINDOW, LANES), 0).astype(F32) * inv
    cos_o, sin_o = jnp.cos(offset), jnp.sin(offset)
    base_pos = pl.program_id(0) * rows + WINDOW * lax.broadcasted_iota(jnp.int32, (bases, LANES), 0)
    base = base_pos.astype(F32) * inv
    cos_b, sin_b = jnp.cos(base), jnp.sin(base)
    for s in range(bases):
        blk = slice(WINDOW * s, WINDOW * (s + 1))
        cb, sb = cos_b[s:s + 1, :], sin_b[s:s + 1, :]
        cos_ref[blk, :] = cb * cos_o - sb * sin_o
        sin_ref[blk, :] = (sb * cos_o + cb * sin_o) * sign_ref[...]


def _mem_kv_kernel(m_ref, g_ref, w_ref, wi_ref, scale_ref, wo_ref, inv_ref, sign_ref,
                   k_ref, v_ref, kb_ref, vb_ref, wi_o_ref, wo_o_ref, cos_ref, sin_ref):
    _cast_next_weights(wi_ref, scale_ref, wo_ref, wi_o_ref, wo_o_ref)
    _rope_table_kernel(inv_ref, sign_ref, cos_ref, sin_ref)
    batch = m_ref.shape[0]
    mn = _rms(m_ref[...].reshape(batch * N_MEM, D_MODEL), g_ref[0]).astype(BF16)
    kv = jnp.dot(mn, w_ref[0].astype(BF16), preferred_element_type=F32)
    for b in range(batch):
        rows = slice(N_MEM * b, N_MEM * (b + 1))
        k, v = kv[rows, :D_MEMW], kv[rows, D_MEMW:]
        for hh in range(MEM_HEADS):
            cols = slice(MEM_HD * hh, MEM_HD * (hh + 1))
            k_ref[0, b, pl.ds(hh, N_MEM, stride=MEM_HEADS), :] = k[:, cols]
            v_ref[0, b, pl.ds(hh, N_MEM, stride=MEM_HEADS), :] = v[:, cols]
        kb_ref[0, b] = k.astype(BF16)
        vb_ref[0, b] = v.T.astype(BF16)


def _mem_kv(mem_prompt, mem_norm_g, w_mem_kv, w_in, in_scale, w_out, inv128, sign128, seq):
    batch = mem_prompt.shape[0]
    layer_blk = lambda *shape: pl.BlockSpec((1,) + shape, lambda l: (l,) + (0,) * len(shape))
    steps = DEPTH
    return pl.pallas_call(
        _mem_kv_kernel,
        grid=(steps,),
        in_specs=[_const_spec((batch, N_MEM, D_MODEL)), layer_blk(1, D_MODEL),
                  layer_blk(D_MODEL, 2 * D_MEMW),
                  pl.BlockSpec((None, D_MODEL // steps, D_IN), lambda l: (0, l, 0)), _const_spec((1, D_IN)),
                  pl.BlockSpec((None, D_CAT // steps, D_MODEL), lambda l: (0, l, 0)),
                  _const_spec((1, LANES)), _const_spec((1, LANES))],
        out_specs=[layer_blk(batch, N_MEM * MEM_HEADS, MEM_HD), layer_blk(batch, N_MEM * MEM_HEADS, MEM_HD),
                   layer_blk(batch, N_MEM, D_MEMW), layer_blk(batch, D_MEMW, N_MEM),
                   pl.BlockSpec((D_MODEL // steps, D_IN), lambda l: (l, 0)),
                   pl.BlockSpec((D_CAT // steps, D_MODEL), lambda l: (l, 0)),
                   pl.BlockSpec((seq // steps, LANES), lambda l: (l, 0)),
                   pl.BlockSpec((seq // steps, LANES), lambda l: (l, 0))],
        out_shape=[jax.ShapeDtypeStruct((DEPTH, batch, N_MEM * MEM_HEADS, MEM_HD), F32)] * 2
                  + [jax.ShapeDtypeStruct((DEPTH, batch, N_MEM, D_MEMW), BF16),
                     jax.ShapeDtypeStruct((DEPTH, batch, D_MEMW, N_MEM), BF16),
                     jax.ShapeDtypeStruct((D_MODEL, D_IN), BF16), jax.ShapeDtypeStruct((D_CAT, D_MODEL), BF16)]
                  + [jax.ShapeDtypeStruct((seq, LANES), F32)] * 2,
        compiler_params=pltpu.CompilerParams(vmem_limit_bytes=VMEM_LIMIT_BYTES),
        name="mem_kv",
    )(mem_prompt, mem_norm_g.reshape(DEPTH, 1, D_MODEL), w_mem_kv, w_in, in_scale, w_out, inv128, sign128)


def _prompt_lru_kernel(tile, make_side, x_ref, ng_ref, win_ref, wout_ref, cw_ref, cb_ref, wg_ref, ba_ref,
                       bx_ref, lam_ref, mk_ref, mv_ref, xo_ref, hl_ref, ct_ref,
                       tails, a3, b3, h3, hc, z, hn_s, proj_s, uc_s):
    half = tile // 2
    t = pl.program_id(1)
    side_begin, side_thunks, side_end = make_side(dict(
        ng=ng_ref, win=win_ref, wout=wout_ref, cw=cw_ref, cb=cb_ref, wg=wg_ref, ba=ba_ref, bx=bx_ref, lam=lam_ref))

    @pl.when(t == 0)
    def _():
        tails[...] = jnp.zeros((SUBLANES, D_MAIN), F32)
        hc[...] = jnp.zeros((SUBLANES, LANES), F32)

    side_begin()

    carry = [hc[...]]


    project = lambda r0: _project_thunks(x_ref, ng_ref, win_ref, hn_s, proj_s, r0, half)

    def gates(r0):
        rows = slice(r0, r0 + half)
        g0, gn = r0 // SUBLANES, half // SUBLANES

        def conv():
            u = proj_s[rows, 0:D_MAIN]
            first_row = lax.broadcasted_iota(jnp.int32, (half, D_MAIN), 0) == 0
            acc = u * cw_ref[0:1, :]
            for k in range(1, CONV_W):
                delayed = jnp.where(first_row, tails[k - 1:k, :], pltpu.roll(acc, 1, 0))
                tails[k - 1:k, :] = acc[half - 1:half, :]
                acc = delayed + u * cw_ref[k:k + 1, :]
            uc_s[rows, :] = acc + cb_ref[...]

        def gate(j):
            def run():
                sl = slice(GATE_TILE * j, GATE_TILE * (j + 1))
                uc = uc_s[rows, sl]
                pre = jnp.dot(uc.astype(BF16), wg_ref[j], preferred_element_type=F32)
                a, b = _rglru_coeffs(uc, pre, ba_ref[:, sl], bx_ref[:, sl], _decay_exponent(lam_ref[:, sl]))
                for jj in range(GATE_TILE // LANES):
                    slab = j * (GATE_TILE // LANES) + jj
                    srows = slice(SUBLANES * slab, SUBLANES * (slab + 1))
                    cols = slice(LANES * jj, LANES * (jj + 1))
                    a3[g0:g0 + gn, srows, :] = a[:, cols].reshape(gn, SUBLANES, LANES)
                    b3[g0:g0 + gn, srows, :] = b[:, cols].reshape(gn, SUBLANES, LANES)
            return run

        return [conv] + [gate(j) for j in range(N_GATE_TILES)]

    def scan(r0):
        def piece(first, count):
            def run():
                h = carry[0]
                for g in range(first, first + count):
                    for r in range(SUBLANES):
                        step = pl.ds(r, N_SLABS, stride=SUBLANES)
                        h = a3[g, step, :] * h + b3[g, step, :]
                        h3[g, step, :] = h
                carry[0] = h
            return run

        g0, gn = r0 // SUBLANES, half // SUBLANES
        count = max(gn // SCAN_CHUNKS, 1)
        return [piece(g, count) for g in range(g0, g0 + gn, count)]

    def gating(r0):
        def slab(j):
            def run():
                g0, gn = r0 // SUBLANES, half // SUBLANES
                y = h3[g0:g0 + gn, SUBLANES * j:SUBLANES * (j + 1), :].reshape(half, LANES)
                g_main = proj_s[r0:r0 + half, D_MAIN + LANES * j:D_MAIN + LANES * (j + 1)]
                z[r0:r0 + half, LANES * j:LANES * (j + 1)] = (y * _silu_of_half(g_main)).astype(BF16)
            return run
        return [slab(j) for j in range(N_SLABS)]

    out_project = lambda r0: _out_project_thunks(x_ref, z, wout_ref, xo_ref, r0, half)
    mem =lambda r0: _skewed(_mem_units(proj_s, mk_ref, mv_ref, z, r0, half), ATTN_SKEW)
    first, second = 0, half
    n_conv_in = 1 + D_MAIN // PROJ_CHUNK
    p_first, p_second = project(first), project(second)
    _run(p_first[:n_conv_in])
    _run(_interleave(p_first[n_conv_in:], gates(first)))
    _run(p_second[:n_conv_in])
    _run(_interleave(p_second[n_conv_in:], gates(second), mem(first), scan(first)))
    _run(_interleave(gating(first) + out_project(first), mem(second), scan(second), side_thunks))
    _run(gating(second))
    _run(out_project(second))

    h = carry[0]
    hc[...] = h

    @pl.when(t == pl.num_programs(1) - 1)
    def _():
        hl_ref[0] = h
        ct_ref[0] = proj_s[tile - SUBLANES:tile, 0:D_MAIN]

    side_end()


def _prompt_lru_part(x, W, w_in_b, w_out_b, layer, mk, mv_t, tile=256):
    batch, seq, _ = x.shape
    tok = lambda b, t: (b, t, 0)
    per_b = lambda b, t: (b, 0, 0)
    mem = lambda b, t: (layer, b, 0, 0)
    in_specs = [
        pl.BlockSpec((1, tile, D_MODEL), tok),
        _layer_spec((1, D_MODEL), layer), _const_spec((D_MODEL, D_IN)), _const_spec((D_CAT, D_MODEL)),
        _layer_spec((CONV_W, D_MAIN), layer), _layer_spec((1, D_MAIN), layer),
        _layer_spec((N_GATE_TILES, GATE_TILE, 2 * GATE_TILE), layer),
        _layer_spec((1, D_MAIN), layer), _layer_spec((1, D_MAIN), layer), _layer_spec((1, D_MAIN), layer),
        pl.BlockSpec((None, 1, N_MEM, D_MEMW), mem), pl.BlockSpec((None, 1, D_MEMW, N_MEM), mem),
    ]
    args = [x, W["norm_g"], w_in_b, w_out_b, W["conv_w"], W["conv_b"], W["w_gate"],
            W["ba"], W["bx"], W["lam"], mk, mv_t]
    out_specs = [pl.BlockSpec((1, tile, D_MODEL), tok),
                 pl.BlockSpec((1, SUBLANES, LANES), per_b),
                 pl.BlockSpec((1, SUBLANES, D_MAIN), per_b)]
    out_shape = [jax.ShapeDtypeStruct((batch, seq, D_MODEL), F32),
                 jax.ShapeDtypeStruct((batch, SUBLANES, LANES), F32),
                 jax.ShapeDtypeStruct((batch, SUBLANES, D_MAIN), F32)]
    groups = tile // SUBLANES
    return dict(
        body=functools.partial(_prompt_lru_kernel, tile), grid=(batch, seq // tile),
        in_specs=in_specs, args=args, out_specs=out_specs, out_shape=out_shape,
        scratch=[pltpu.VMEM((SUBLANES, D_MAIN), F32),
                 pltpu.VMEM((groups, SUBLANES * N_SLABS, LANES), F32),
                 pltpu.VMEM((groups, SUBLANES * N_SLABS, LANES), F32),
                 pltpu.VMEM((groups, SUBLANES * N_SLABS, LANES), F32),
                 pltpu.VMEM((SUBLANES, LANES), F32),
                 pltpu.VMEM((tile, D_CAT), BF16),
                 pltpu.VMEM((tile, D_MODEL), BF16),
                 pltpu.VMEM((tile, D_IN), F32),
                 pltpu.VMEM((tile, D_MAIN), F32)],
        name="lru")


def _sink_softmax(s, sink, axis=-1):
    m = jnp.maximum(jnp.max(s, axis=axis, keepdims=True), sink)
    p = jnp.exp2(s - m)
    den = jnp.sum(p, axis=axis, keepdims=True) + jnp.exp2(sink - m)
    return p, den


def _prompt_swa_kernel(make_kv, final, tile, sink_row, make_side, *refs):
    it = iter(refs)
    take = lambda n: [next(it) for _ in range(n)]
    x_ref, ng_ref, win_ref, wout_ref, mk_ref, mv_ref, sink_ref = take(7)
    if make_kv:
        kvg_ref, wkv_ref = take(2)
    else:
        kp_ref, kc_ref, vp_ref, vc_ref = take(4)
    cos_ref, sin_ref = take(2)
    fg_ref = take(1)[0] if final else None
    xo_ref, = take(1)
    if make_kv:
        ko_ref, vo_ref = take(2)
    if final:
        y_ref, = take(1)
    z, qz, kcat, vcat_t, hn_s, proj_s = take(6)
    half = tile // 2
    t = pl.program_id(1)
    side_begin, side_thunks, side_end = make_side(dict(ng=ng_ref, win=win_ref, wout=wout_ref, fg=fg_ref))
    side_begin()

    def rope_queries(r0):
        rows = slice(r0, r0 + half)

        def slab(j):
            def run():
                cos = cos_ref[rows, :]
                sin = sin_ref[rows, :]
                lane = lax.broadcasted_iota(jnp.int32, (half, LANES), 1)
                lo = lane < HEAD_DIM
                first_half = (lane % HEAD_DIM) < (HEAD_DIM // 2)
                c = j // (GROUP // 2)
                xq = proj_s[rows, LANES * j:LANES * (j + 1)] * (HEAD_DIM ** -0.5 * LOG2_E)
                r32 = pltpu.roll(xq, HEAD_DIM // 2, 1)
                r96 = pltpu.roll(xq, LANES - HEAD_DIM // 2, 1)
                r64 = pltpu.roll(xq, HEAD_DIM, 1)
                stay = xq * cos + jnp.where(first_half, r96, r32) * sin
                move = r64 * cos + jnp.where(first_half, r32, r96) * sin
                keep = lo if c == 0 else ~lo
                even, odd = (stay, move) if c == 0 else (move, stay)
                qz[rows, 2 * LANES * j:2 * LANES * j + LANES] = jnp.where(keep, even, 0.0).astype(BF16)
                qz[rows, 2 * LANES * j + LANES:2 * LANES * (j + 1)] = jnp.where(keep, odd, 0.0).astype(BF16)
            return run

        return [slab(j) for j in range(N_SLABS)]

    if make_kv:
        @pl.when(t == 0)
        def _():
            kcat[0:WINDOW, :] = jnp.zeros((WINDOW, D_KV), BF16)
            vcat_t[:, 0:WINDOW] = jnp.zeros((D_KV, WINDOW), BF16)

        @pl.when(t > 0)
        def _():
            kcat[0:WINDOW, :] = kcat[tile:tile + WINDOW, :]
            vcat_t[:, 0:WINDOW] = vcat_t[:, tile:tile + WINDOW]

        def shared_kv(r0):
            def run():
                rows = slice(r0, r0 + half)
                kvn = _rms(x_ref[0, rows, :], kvg_ref[...]).astype(BF16)
                kv = jnp.dot(kvn, wkv_ref[...], preferred_element_type=F32)
                k = _rope(kv[:, :D_KV], cos_ref[rows, :], sin_ref[rows, :])
                v = kv[:, D_KV:]
                ko_ref[0, rows, :] = k
                vo_ref[0, rows, :] = v
                kcat[WINDOW + r0:WINDOW + r0 + half, :] = k.astype(BF16)
                vcat_t[:, WINDOW + r0:WINDOW + r0 + half] = v.T.astype(BF16)
            return [run]
    else:
        kcat[0:WINDOW, :] = kp_ref[0].astype(BF16)
        kcat[WINDOW:, :] = kc_ref[0].astype(BF16)
        vcat_t[:, 0:WINDOW] = vp_ref[0].T.astype(BF16)
        vcat_t[:, WINDOW:] = vc_ref[0].T.astype(BF16)
        shared_kv = lambda r0: []

    key_r = lax.broadcasted_iota(jnp.int32, (WINDOW, 2 * WINDOW), 0)
    query = lax.broadcasted_iota(jnp.int32, (WINDOW, 2 * WINDOW), 1) % WINDOW
    from_prev = key_r > query

    def scores(n, j):
        rows = slice(WINDOW * n, WINDOW * (n + 1))
        keys = kcat[WINDOW * n:WINDOW * (n + 2), :]
        q2 = jnp.concatenate([qz[rows, 2 * LANES * j:2 * LANES * j + LANES],
                              qz[rows, 2 * LANES * j + LANES:2 * LANES * (j + 1)]], axis=0)
        s = lax.dot_general(keys, q2, _NT, preferred_element_type=F32)
        s_prev, s_own = s[:WINDOW, :], s[WINDOW:, :]
        if n == 0:
            s_prev = jnp.where(t > 0, s_prev, NEG)
        visible = jnp.where(from_prev, s_prev, s_own)
        p0, den0 = _sink_softmax(visible[:, :WINDOW], sink_ref[sink_row, 2 * j] * LOG2_E, axis=0)
        p1, den1 = _sink_softmax(visible[:, WINDOW:], sink_ref[sink_row, 2 * j + 1] * LOG2_E, axis=0)
        p = jnp.concatenate([p0, p1], axis=1).astype(BF16)
        zero = jnp.zeros_like(p)
        return jnp.concatenate([jnp.where(from_prev, p, zero), jnp.where(from_prev, zero, p)], axis=0), den0, den1

    def outputs(n, j, p, den0, den1):
        rows = slice(WINDOW * n, WINDOW * (n + 1))
        c = j // (GROUP // 2)
        vals_t = vcat_t[HEAD_DIM * c:HEAD_DIM * (c + 1), WINDOW * n:WINDOW * (n + 2)]
        o_t = jnp.dot(vals_t, p, preferred_element_type=F32)
        y_t = jnp.concatenate([o_t[:, :WINDOW] / den0, o_t[:, WINDOW:] / den1], axis=0)
        g_main = proj_s[rows, D_MAIN + LANES * j:D_MAIN + LANES * (j + 1)]
        z[rows, LANES * j:LANES * (j + 1)] = (y_t.T * _silu_of_half(g_main)).astype(BF16)

    def attend(r0):
        units = [(functools.partial(scores, n, j), functools.partial(outputs, n, j))
                 for n in range(r0 // WINDOW, (r0 + half) // WINDOW) for j in range(N_SLABS)]
        return _skewed(units + _mem_units(proj_s, mk_ref, mv_ref, z, r0, half), ATTN_SKEW)

    def out_project(r0):
        rows = slice(r0, r0 + half)

        def final_norm():
            y_ref[0, rows, :] = _rms(xo_ref[0, rows, :], fg_ref[...])

        return _out_project_thunks(x_ref, z, wout_ref, xo_ref, r0, half) + ([final_norm] if final else [])

    project = lambda r0: _project_thunks(x_ref, ng_ref, win_ref, hn_s, proj_s, r0, half)
    first, second = 0, half
    _run(project(first) + shared_kv(first))
    _run(_interleave(project(second) + shared_kv(second), rope_queries(first)))
    _run(_interleave(attend(first), rope_queries(second)))
    _run(_interleave(attend(second), out_project(first), side_thunks))
    _run(out_project(second))
    side_end()


def _prompt_swa_part(x, W, w_in_b, w_out_b, layer, mk, mv_t, kv, cos, sin, final_g=None, tile=256):
    batch, seq, _ = x.shape
    final = final_g is not None
    make_kv = kv is None
    blocks = tile // WINDOW
    tok = lambda b, t: (b, t, 0)
    mem = lambda b, t: (layer, b, 0, 0)
    prev_blk = lambda b, t: (b, jnp.maximum(t * blocks - 1, 0), 0)
    in_specs = [
        pl.BlockSpec((1, tile, D_MODEL), tok),
        _layer_spec((1, D_MODEL), layer), _const_spec((D_MODEL, D_IN)), _const_spec((D_CAT, D_MODEL)),
        pl.BlockSpec((None, 1, N_MEM, D_MEMW), mem), pl.BlockSpec((None, 1, D_MEMW, N_MEM), mem),
        pl.BlockSpec(memory_space=pltpu.SMEM)]
    args = [x, W["norm_g"], w_in_b, w_out_b, mk, mv_t, W["sinks"]]
    if make_kv:
        in_specs += [_const_spec((1, D_MODEL)), _const_spec((D_MODEL, 2 * D_KV))]
        args += [W["kv_norm_g"], W["w_kv"]]
    else:
        k, v = kv
        in_specs += [pl.BlockSpec((1, WINDOW, D_KV), prev_blk), pl.BlockSpec((1, tile, D_KV), tok),
                     pl.BlockSpec((1, WINDOW, D_KV), prev_blk), pl.BlockSpec((1, tile, D_KV), tok)]
        args += [k, k, v, v]
    in_specs += [pl.BlockSpec((tile, LANES), lambda b, t: (t, 0)), pl.BlockSpec((tile, LANES), lambda b, t: (t, 0))]
    args += [cos, sin]
    out_specs = [pl.BlockSpec((1, tile, D_MODEL), tok)]
    out_shape = [jax.ShapeDtypeStruct((batch, seq, D_MODEL), F32)]
    if make_kv:
        out_specs += [pl.BlockSpec((1, tile, D_KV), tok)] * 2
        out_shape += [jax.ShapeDtypeStruct((batch, seq, D_KV), F32)] * 2
    if final:
        in_specs.append(_const_spec((1, D_MODEL)))
        args.append(final_g)
        out_specs.append(pl.BlockSpec((1, tile, D_MODEL), tok))
        out_shape.append(jax.ShapeDtypeStruct((batch, seq, D_MODEL), F32))
    return dict(
        body=functools.partial(_prompt_swa_kernel, make_kv, final, tile, layer - N_A), grid=(batch, seq // tile),
        in_specs=in_specs, args=args, out_specs=out_specs, out_shape=out_shape,
        scratch=[pltpu.VMEM((tile, D_CAT), BF16),
                 pltpu.VMEM((tile, 2 * D_MAIN), BF16),
                 pltpu.VMEM((WINDOW + tile, D_KV), BF16),
                 pltpu.VMEM((D_KV, WINDOW + tile), BF16),
                 pltpu.VMEM((tile, D_MODEL), BF16),
                 pltpu.VMEM((tile, D_IN), F32)],
        name="swa" + ("_kv" if make_kv else "") + ("_final" if final else ""))


def _sample_mem_unit(b, bb, qm_s, mk_ref, mv_ref, om_s):
    def scores():
        row = lax.broadcasted_iota(jnp.int32, (SUBLANES, MEM_HD), 0)
        q = qm_s[pl.ds(b, 1), :]
        qh = jnp.zeros((SUBLANES, MEM_HD), F32)
        for hh in range(MEM_HEADS):
            q_head = jnp.broadcast_to(q[:, MEM_HD * hh:MEM_HD * (hh + 1)], (SUBLANES, MEM_HD))
            qh = jnp.where(row == hh, q_head, qh)
        s = lax.dot_general(qh.astype(BF16), mk_ref[bb].astype(BF16), _NT,
                            preferred_element_type=F32) * (MEM_HD ** -0.5 * LOG2_E)
        srow = lax.broadcasted_iota(jnp.int32, s.shape, 0)
        scol = lax.broadcasted_iota(jnp.int32, s.shape, 1)
        s = jnp.where((scol % MEM_HEADS) == srow, s, NEG)
        m = jnp.max(s, axis=-1, keepdims=True)
        p = jnp.exp2(s - m)
        return p.astype(BF16), jnp.sum(p, axis=-1, keepdims=True)

    def outputs(p, den):
        o = jnp.dot(p, mv_ref[bb].astype(BF16), preferred_element_type=F32) / den
        om_s[pl.ds(b, 1), :] = jnp.concatenate([o[hh:hh + 1, :] for hh in range(MEM_HEADS)], axis=1)

    return scores, outputs


def _sample_lru_side(seqs, step, n_steps, w, x_ref, h0_ref, c0_ref, mk_ref, mv_ref, xo_ref, ho_ref, co_ref,
                     z_s, qm_s, gm_s, om_s):
    def begin():
        @pl.when(step == 0)
        def _():
            hn = _rms(x_ref[...], w["ng"][...]).astype(BF16)
            proj = jnp.dot(hn, w["win"][...], preferred_element_type=F32)
            u = proj[:, :D_MAIN]
            taps = [c0_ref[k] for k in range(CONV_W - 1)] + [u]
            uc = w["cb"][...] + taps[0] * w["cw"][0:1, :]
            for k in range(1, CONV_W):
                uc = uc + taps[k] * w["cw"][k:k + 1, :]
            for k in range(CONV_W - 1):
                co_ref[k] = taps[k + 1]
            decay_c = _decay_exponent(w["lam"][...])
            ucb = uc.astype(BF16)
            for j in range(N_GATE_TILES):
                sl = slice(GATE_TILE * j, GATE_TILE * (j + 1))
                pre = jnp.dot(ucb[:, sl], w["wg"][j], preferred_element_type=F32)
                a, b = _rglru_coeffs(uc[:, sl], pre, w["ba"][:, sl], w["bx"][:, sl], decay_c[:, sl])
                h = a * h0_ref[:, sl] + b
                ho_ref[:, sl] = h
                z_s[:, sl] = (h * _silu_of_half(
                    proj[:, D_MAIN + GATE_TILE * j:D_MAIN + GATE_TILE * (j + 1)])).astype(BF16)
            qm_s[...] = proj[:, 2 * D_MAIN:2 * D_MAIN + D_MEMW]
            gm_s[...] = _silu_of_half(proj[:, 2 * D_MAIN + D_MEMW:])

    thunks = _skewed([_sample_mem_unit(step * seqs + bb, bb, qm_s, mk_ref, mv_ref, om_s) for bb in range(seqs)],
                     ATTN_SKEW)

    def end():
        @pl.when(step == n_steps - 1)
        def _():
            z_s[:, D_MAIN:] = (om_s[...] * gm_s[...]).astype(BF16)
            xo_ref[...] = x_ref[...] + jnp.dot(z_s[...], w["wout"][...], preferred_element_type=F32)

    return begin, thunks, end


def _cache_spec(layer, seqs, steps_per_row):
    return pl.BlockSpec((None, seqs, N_MEM * MEM_HEADS, MEM_HD),
                        lambda b, t: (layer, b * steps_per_row + t, 0, 0))


def _whole_spec(shape):
    zeros = (0,) * len(shape)
    return pl.BlockSpec(shape, lambda *_: zeros)


def _sample_lru_part(x, layer, state_h, state_conv_t, cache_k, cache_v, grid):
    batch = x.shape[0]
    seqs = batch // (grid[0] * grid[1])
    cache_spec = _cache_spec(layer, seqs, grid[1])
    return dict(
        body=functools.partial(_sample_lru_side, seqs),
        in_specs=[_const_spec((batch, D_MODEL)),
                  _layer_spec((batch, D_MAIN), layer), _layer_spec((CONV_W - 1, batch, D_MAIN), layer),
                  cache_spec, cache_spec],
        args=[x, state_h, state_conv_t, cache_k, cache_v],
        out_specs=[_whole_spec((batch, D_MODEL)), _whole_spec((batch, D_MAIN)),
                   _whole_spec((CONV_W - 1, batch, D_MAIN))],
        out_shape=[jax.ShapeDtypeStruct((batch, D_MODEL), F32),
                   jax.ShapeDtypeStruct((batch, D_MAIN), F32),
                   jax.ShapeDtypeStruct((CONV_W - 1, batch, D_MAIN), F32)],
        scratch=[pltpu.VMEM((batch, D_CAT), BF16), pltpu.VMEM((batch, D_MEMW), F32),
                 pltpu.VMEM((batch, D_MEMW), F32), pltpu.VMEM((batch, D_MEMW), F32)])


def _sample_swa_side(first, final, seqs, step, n_steps, w, *refs):
    (x_ref, sink_ref, mk_ref, mv_ref, wk_ref, wv_ref) = refs[:6]
    pos = 6
    if first:
        kvg_ref, wkv_ref = refs[pos:pos + 2]
        pos += 2
    inv_ref, sign_ref = refs[pos:pos + 2]
    pos += 2
    xo_ref = refs[pos]
    pos += 1
    if first:
        wko_ref, wvo_ref = refs[pos:pos + 2]
        pos += 2
    if final:
        y_ref = refs[pos]
        pos += 1
    z_s, q_s, gmain_s, y_s, qm_s, gm_s, om_s, kn_s, vn_s = refs[pos:]

    def begin():
        @pl.when(step == 0)
        def _():
            x = x_ref[...]
            hn = _rms(x, w["ng"][...]).astype(BF16)
            proj = jnp.dot(hn, w["win"][...], preferred_element_type=F32)
            ang = float(PAST_LEN) * inv_ref[...]
            cos = jnp.cos(ang)
            sin = jnp.sin(ang) * sign_ref[...]
            for j in range(N_SLABS):
                cols = slice(LANES * j, LANES * (j + 1))
                q_s[:, cols] = _rope(proj[:, cols], cos, sin)
            gmain_s[...] = _silu_of_half(proj[:, D_MAIN:2 * D_MAIN])
            qm_s[...] = proj[:, 2 * D_MAIN:2 * D_MAIN + D_MEMW]
            gm_s[...] = _silu_of_half(proj[:, 2 * D_MAIN + D_MEMW:])
            if first:
                kvn = _rms(x, kvg_ref[...]).astype(BF16)
                kv = jnp.dot(kvn, wkv_ref[...], preferred_element_type=F32)
                kn_s[...] = _rope(kv[:, :D_KV], cos, sin).T
                vn_s[...] = kv[:, D_KV:].T

    row = lax.broadcasted_iota(jnp.int32, (N_HEADS, LANES), 0)
    lane = lax.broadcasted_iota(jnp.int32, (N_HEADS, LANES), 1)
    kv_half = (lane // HEAD_DIM) == (row // GROUP)
    in_place = (row % 2) == (row // GROUP)
    lane1 = lax.broadcasted_iota(jnp.int32, (1, LANES), 1)
    seq_lane = lax.broadcasted_iota(jnp.int32, kn_s.shape, 1)
    slot = lax.broadcasted_iota(jnp.int32, (D_KV, WINDOW), 1)

    def slide(win_t, new_t_s, b):
        if new_t_s.shape[1] == WINDOW:
            col = pltpu.roll(new_t_s[...], WINDOW - 1 - b, 1)
        else:
            col = jnp.sum(jnp.where(seq_lane == b, new_t_s[...], 0.0), axis=1, keepdims=True)
        return jnp.where(slot == WINDOW - 1, col, pltpu.roll(win_t, WINDOW - 1, 1))

    def window_unit(b, bb):
        def scores():
            if first:
                kw_t = slide(wk_ref[bb], kn_s, b)
                wko_ref[bb] = kw_t
            else:
                kw_t = wk_ref[bb]
            q = jnp.broadcast_to(q_s[pl.ds(b, 1), :], (N_HEADS, D_MAIN))
            e = jnp.zeros((N_HEADS, LANES), F32)
            for j in range(N_SLABS):
                e = jnp.where((row // 2) == j, q[:, LANES * j:LANES * (j + 1)], e)
            qh = jnp.where(kv_half, jnp.where(in_place, e, pltpu.roll(e, HEAD_DIM, 1)), 0.0).astype(BF16)
            s = jnp.dot(qh, kw_t.astype(BF16), preferred_element_type=F32) * (HEAD_DIM ** -0.5 * LOG2_E)
            p, den = _sink_softmax(s, sink_ref[...] * LOG2_E)
            return p.astype(BF16), den

        def outputs(p, den):
            if first:
                vw_t = slide(wv_ref[bb], vn_s, b)
                wvo_ref[bb] = vw_t
            else:
                vw_t = wv_ref[bb]
            o = lax.dot_general(p, vw_t.astype(BF16), _NT, preferred_element_type=F32) / den
            f = jnp.where(in_place, o, pltpu.roll(o, HEAD_DIM, 1))
            y_s[pl.ds(b, 1), :] = jnp.concatenate(
                [jnp.where(lane1 < HEAD_DIM, f[2 * j:2 * j + 1, :], f[2 * j + 1:2 * j + 2, :])
                 for j in range(N_SLABS)], axis=1)

        return scores, outputs

    units = []
    for bb in range(seqs):
        units.append(window_unit(step * seqs + bb, bb))
        units.append(_sample_mem_unit(step * seqs + bb, bb, qm_s, mk_ref, mv_ref, om_s))

    def end():
        @pl.when(step == n_steps - 1)
        def _():
            z_s[:, :D_MAIN] = (y_s[...] * gmain_s[...]).astype(BF16)
            z_s[:, D_MAIN:] = (om_s[...] * gm_s[...]).astype(BF16)
            xn = x_ref[...] + jnp.dot(z_s[...], w["wout"][...], preferred_element_type=F32)
            xo_ref[...] = xn
            if final:
                y_ref[...] = _rms(xn, w["fg"][...])

    return begin, _skewed(units, ATTN_SKEW), end


def _sample_swa_part(x, W, layer, cache_k, cache_v, win_k_t, win_v_t, first, final, grid):
    batch = x.shape[0]
    seqs = batch // (grid[0] * grid[1])
    cache_spec = _cache_spec(layer, seqs, grid[1])
    win_spec = pl.BlockSpec((seqs, D_KV, WINDOW), lambda b, t: (b * grid[1] + t, 0, 0))
    in_specs = [_const_spec((batch, D_MODEL)), _layer_spec((N_HEADS, 1), layer - N_A),
                cache_spec, cache_spec, win_spec, win_spec]
    args = [x, W["sinks_col"], cache_k, cache_v, win_k_t, win_v_t]
    if first:
        in_specs += [_const_spec((1, D_MODEL)), _const_spec((D_MODEL, 2 * D_KV))]
        args += [W["kv_norm_g"], W["w_kv"]]
    in_specs += [_const_spec((1, LANES)), _const_spec((1, LANES))]
    args += [W["inv128"], W["sign128"]]
    out_specs = [_whole_spec((batch, D_MODEL))]
    out_shape = [jax.ShapeDtypeStruct((batch, D_MODEL), F32)]
    if first:
        out_specs += [win_spec, win_spec]
        out_shape += [jax.ShapeDtypeStruct((batch, D_KV, WINDOW), F32)] * 2
    if final:
        out_specs.append(_whole_spec((batch, D_MODEL)))
        out_shape.append(jax.ShapeDtypeStruct((batch, D_MODEL), F32))
    return dict(
        body=functools.partial(_sample_swa_side, first, final, seqs),
        in_specs=in_specs, args=args, out_specs=out_specs, out_shape=out_shape,
        scratch=[pltpu.VMEM((batch, D_CAT), BF16), pltpu.VMEM((batch, D_MAIN), F32),
                 pltpu.VMEM((batch, D_MAIN), F32), pltpu.VMEM((batch, D_MAIN), F32),
                 pltpu.VMEM((batch, D_MEMW), F32), pltpu.VMEM((batch, D_MEMW), F32),
                 pltpu.VMEM((batch, D_MEMW), F32),
                 pltpu.VMEM((D_KV, batch), F32), pltpu.VMEM((D_KV, batch), F32)])


def _cast_next_weights(wi_ref, scale_ref, wo_ref, wi_o_ref, wo_o_ref):
    wi_o_ref[...] = (wi_ref[...] * scale_ref[...]).astype(BF16)
    wo_o_ref[...] = wo_ref[...].astype(BF16)


def _cast_part(W, next_layer, grid):
    steps = grid[0] * grid[1]
    slab = lambda b, t: (b * grid[1] + t, 0)
    layer_slab = lambda b, t: (next_layer, b * grid[1] + t, 0)
    return dict(
        body=_cast_next_weights,
        in_specs=[pl.BlockSpec((None, D_MODEL // steps, D_IN), layer_slab), _const_spec((1, D_IN)),
                  pl.BlockSpec((None, D_CAT // steps, D_MODEL), layer_slab)],
        args=[W["w_in"], W["in_scale"], W["w_out"]],
        out_specs=[pl.BlockSpec((D_MODEL // steps, D_IN), slab), pl.BlockSpec((D_CAT // steps, D_MODEL), slab)],
        out_shape=[jax.ShapeDtypeStruct((D_MODEL, D_IN), BF16), jax.ShapeDtypeStruct((D_CAT, D_MODEL), BF16)],
        scratch=[])


def _layer_kernel(prompt_body, sample_body, extra_body, n_prompt, n_sample, n_extra, *refs):
    it = iter(refs)
    take = lambda n: [next(it) for _ in range(n)]
    p_in, s_in, e_in = take(n_prompt[0]), take(n_sample[0]), take(n_extra[0])
    p_out, s_out, e_out = take(n_prompt[1]), take(n_sample[1]), take(n_extra[1])
    p_scr, s_scr = take(n_prompt[2]), take(n_sample[2])
    if extra_body is not None:
        extra_body(*e_in, *e_out)
    step = pl.program_id(0) * pl.num_programs(1) + pl.program_id(1)
    n_steps = pl.num_programs(0) * pl.num_programs(1)
    make_side = lambda weights: sample_body(step, n_steps, weights, *s_in, *s_out, *s_scr)
    prompt_body(make_side, *p_in, *p_out, *p_scr)


def _layer_call(prompt, sample, extra=None):
    extra = extra or dict(body=None, in_specs=[], args=[], out_specs=[], out_shape=[], scratch=[])
    parts = (prompt, sample, extra)
    counts = lambda part: (len(part["in_specs"]), len(part["out_specs"]), len(part["scratch"]))
    gather = lambda key: [item for part in parts for item in part[key]]
    outs = pl.pallas_call(
        functools.partial(_layer_kernel, prompt["body"], sample["body"], extra["body"],
                          counts(prompt), counts(sample), counts(extra)),
        grid=prompt["grid"],
        in_specs=gather("in_specs"), out_specs=gather("out_specs"), out_shape=gather("out_shape"),
        scratch_shapes=gather("scratch"),
        compiler_params=pltpu.CompilerParams(
            dimension_semantics=("arbitrary", "arbitrary"), vmem_limit_bytes=VMEM_LIMIT_BYTES),
        name="layer_" + prompt["name"],
    )(*gather("args"))
    n_p, n_s = len(prompt["out_specs"]), len(sample["out_specs"])
    return outs[:n_p], outs[n_p:n_p + n_s], outs[n_p + n_s:]


def _trunks(xp, xs, W, mkb, mvb_t, cache_mem_k, cache_mem_v, state_h, state_conv, win_k, win_v, tile=512):
    pb, seq, _ = xp.shape
    sb = xs.shape[0]
    grid = (pb, seq // tile)
    cos, sin = W["rope"]
    xs = xs.reshape(sb, D_MODEL)
    ck = cache_mem_k.reshape(DEPTH, sb, N_MEM * MEM_HEADS, MEM_HD)
    cv = cache_mem_v.reshape(DEPTH, sb, N_MEM * MEM_HEADS, MEM_HD)
    conv_t = jnp.transpose(state_conv, (0, 2, 1, 3))
    wk = jnp.transpose(win_k, (0, 2, 3, 1)).reshape(sb, D_KV, WINDOW)
    wv = jnp.transpose(win_v, (0, 2, 3, 1)).reshape(sb, D_KV, WINDOW)
    hp, cp, hs, cs = [], [], [], []
    w_in_b, w_out_b = W["w_in0"], W["w_out0"]
    cast_next = lambda l: _cast_part(W, l + 1, grid) if l + 1 < DEPTH else None
    for l in range(N_A):
        p_out, s_out, (w_in_b, w_out_b) = _layer_call(
            _prompt_lru_part(xp, W, w_in_b, w_out_b, l, mkb, mvb_t, tile=tile),
            _sample_lru_part(xs, l, state_h, conv_t, ck, cv, grid), cast_next(l))
        xp, h_last, conv_tail = p_out
        hp.append(h_last.reshape(pb, D_MAIN))
        cp.append(conv_tail[:, SUBLANES - (CONV_W - 1):, :])
        xs, h, c = s_out
        hs.append(h)
        cs.append(c)
    yp = ys = kv = None
    for l in range(N_A, DEPTH):
        first, last = l == N_A, l == DEPTH - 1
        p_out, s_out, next_w = _layer_call(
            _prompt_swa_part(xp, W, w_in_b, w_out_b, l, mkb, mvb_t, kv, cos, sin,
                             final_g=W["final_g"] if last else None, tile=tile),
            _sample_swa_part(xs, W, l, ck, cv, wk, wv, first, last, grid), cast_next(l))
        if not last:
            w_in_b, w_out_b = next_w
        xp, xs = p_out[0], s_out[0]
        if first:
            kv = (p_out[1], p_out[2])
            wk, wv = s_out[1], s_out[2]
        if last:
            yp, ys = p_out[-1], s_out[-1]
    k, v = kv
    win_kp = k[:, seq - WINDOW:, :].reshape(pb, WINDOW, N_KV, HEAD_DIM)
    win_vp = v[:, seq - WINDOW:, :].reshape(pb, WINDOW, N_KV, HEAD_DIM)
    unslide = lambda w_t: jnp.transpose(w_t.reshape(sb, N_KV, HEAD_DIM, WINDOW), (0, 3, 1, 2))
    prompt_out = (yp, jnp.stack(hp), jnp.stack(cp), win_kp, win_vp)
    sample_out = (ys.reshape(sb, 1, D_MODEL), jnp.stack(hs), jnp.transpose(jnp.stack(cs), (0, 2, 1, 3)),
                  unslide(wk), unslide(wv))
    return prompt_out, sample_out


def kernel(x_prompt, x_sample, cache_mem_k, cache_mem_v, state_lru_h, state_conv, cache_win_k, cache_win_v, mem_prompt, norm_g, w_in, w_out, mem_norm_g, w_mem_kv, conv_w, conv_b, lru_wa, lru_ba, lru_wx, lru_bx, lru_lambda, kv_norm_g, w_kv, sinks, final_g):
    W = _prepare_weights(norm_g, w_in, w_out, conv_w, conv_b, lru_wa, lru_ba, lru_wx, lru_bx, lru_lambda, kv_norm_g, w_kv, sinks, final_g)
    mk, mv, mkb, mvb, w_in0, w_out0, cos, sin = _mem_kv(
        mem_prompt, mem_norm_g, w_mem_kv, w_in, W["in_scale"], w_out, W["inv128"], W["sign128"], x_prompt.shape[1])
    W = dict(W, w_in0=w_in0, w_out0=w_out0, rope=(cos, sin))
    (y_p, h_p, conv_p, wk_p, wv_p), (y_s, h_s, conv_s, wk_s, wv_s) = _trunks(
        x_prompt, x_sample, W, mkb, mvb, cache_mem_k, cache_mem_v, state_lru_h, state_conv,
        cache_win_k, cache_win_v)
    batch = x_prompt.shape[0]
    mk_p = mk.reshape(DEPTH, batch, N_MEM, MEM_HEADS, MEM_HD)
    mv_p = mv.reshape(DEPTH, batch, N_MEM, MEM_HEADS, MEM_HD)
    return (y_p, y_s, mk_p, mv_p, h_p, conv_p, wk_p, wv_p, h_s, conv_s, wk_s, wv_s)
```
